```python
import jax, jax.numpy as jnp
from jax import lax
import numpy as np

D_MODEL = 1024
BATCH = 2
SEQ = 8192
DEPTH = 2

GRID_W = 64
CTX_LEN = 256
N_MIXERS = 2
EPS = 1e-6
D_RNN = D_MODEL
RG_BLOCKS = 8
RG_BW = D_RNN // RG_BLOCKS
CONV_W = 4
LRU_C = 8.0
HEAD_DIM = 128
N_HEADS = D_MODEL // HEAD_DIM
N_KV = 2
GROUPS = N_HEADS // N_KV
Q_BLOCK = 128
ROPE_THETA = 10000.0
ROPE_AXIS_DIM = HEAD_DIM // 2
ROPE_NFREQ = ROPE_AXIS_DIM // 2
D_FF = 2816
N_EXPERTS = 8
TOP_K = 2
D_FF_EXPERT = 3584
MOE_BLOCK = 128
N_A = (DEPTH + 1) // 2
N_B = DEPTH // 2
N_DENSE = (DEPTH + 1) // 2
N_MOE = DEPTH // 2

kernel_name = "hybrid_rglru_gqa_moe_diffusion_block"


def rmsnorm(x, g):
    xf = x.astype(jnp.float32)
    y = xf * lax.rsqrt(jnp.mean(xf * xf, axis=-1, keepdims=True) + EPS)
    return (y * g).astype(x.dtype)


def modulate(h, shift, scale):
    return h * (1 + scale) + shift


def dwconv_centered(x, w, b):
    L = x.shape[1]
    left = CONV_W // 2
    right = CONV_W - 1 - left
    xp = jnp.pad(x, ((0, 0), (left, right), (0, 0)))
    y = b
    for k in range(CONV_W):
        y = y + xp[:, k:k + L] * w[k]
    return y


def block_diag(x, w):
    B, L, _ = x.shape
    xb = x.reshape(B, L, RG_BLOCKS, RG_BW)
    return jnp.einsum('blnc,ncd->blnd', xb, w).reshape(B, L, RG_BLOCKS * RG_BW)


def lru_coeffs(xc, w_a, b_a, w_i, b_i, lam):
    xf = xc.astype(jnp.float32)
    r = jax.nn.sigmoid(block_diag(xf, w_a) + b_a)
    i = jax.nn.sigmoid(block_diag(xf, w_i) + b_i)
    log_a = -LRU_C * r * jax.nn.softplus(-lam.astype(jnp.float32))
    a = jnp.exp(log_a)
    mult = jnp.sqrt(-jnp.expm1(2.0 * log_a))
    return a, mult * (i * xf)


def linear_scan(a, b, h0, reverse):
    def combine(e1, e2):
        a1, b1 = e1
        a2, b2 = e2
        return a1 * a2, a2 * b1 + b2
    A, Bc = lax.associative_scan(combine, (a, b), reverse=reverse, axis=1)
    return A * h0[:, None, :] + Bc


def rglru_mixer(h_lat, h_ctx, w_in, conv_w, conv_b, w_a, b_a, w_i, b_i, lam, w_out, need_ctx):
    xb_l, gb_l = jnp.split(h_lat @ w_in, 2, axis=-1)
    xb_c, gb_c = jnp.split(h_ctx @ w_in, 2, axis=-1)
    xc_l = dwconv_centered(xb_l, conv_w, conv_b)
    xc_c = dwconv_centered(xb_c, conv_w, conv_b)
    ys_l, ys_c = [], []
    for d, reverse in enumerate((False, True)):
        a_c, b_c = lru_coeffs(xc_c, w_a[d], b_a[d], w_i[d], b_i[d], lam[d])
        h_c = linear_scan(a_c, b_c, jnp.zeros_like(a_c[:, 0]), reverse)
        h0 = h_c[:, 0] if reverse else h_c[:, -1]
        a_l, b_l = lru_coeffs(xc_l, w_a[d], b_a[d], w_i[d], b_i[d], lam[d])
        ys_l.append(linear_scan(a_l, b_l, h0, reverse))
        ys_c.append(h_c)
    y_l = (ys_l[0] + ys_l[1]).astype(h_lat.dtype)
    out_lat = (y_l * jax.nn.gelu(gb_l)) @ w_out
    out_ctx = None
    if need_ctx:
        y_c = (ys_c[0] + ys_c[1]).astype(h_ctx.dtype)
        out_ctx = (y_c * jax.nn.gelu(gb_c)) @ w_out
    return out_lat, out_ctx


def axial_rope_angles(L):
    rows = L // GRID_W
    row = jnp.broadcast_to(jnp.arange(rows)[:, None], (rows, GRID_W)).reshape(L).astype(jnp.float32)
    col = jnp.broadcast_to(jnp.arange(GRID_W)[None, :], (rows, GRID_W)).reshape(L).astype(jnp.float32)
    freqs = ROPE_THETA ** (-jnp.arange(ROPE_NFREQ, dtype=jnp.float32) / ROPE_NFREQ)
    return row[:, None] * freqs, col[:, None] * freqs


def rotate(x, ang):
    cos = jnp.cos(ang)[None, :, None, :].astype(x.dtype)
    sin = jnp.sin(ang)[None, :, None, :].astype(x.dtype)
    x1, x2 = x[..., :ROPE_NFREQ], x[..., ROPE_NFREQ:]
    return jnp.concatenate([x1 * cos - x2 * sin, x2 * cos + x1 * sin], axis=-1)


def apply_axial_rope(x, ang_row, ang_col):
    return jnp.concatenate([rotate(x[..., :ROPE_AXIS_DIM], ang_row),
                            rotate(x[..., ROPE_AXIS_DIM:], ang_col)], axis=-1)


def gqa_attend(q, k, v):
    s = jnp.einsum('bqkgd,bskd->bkgqs', q, k).astype(jnp.float32) * (HEAD_DIM ** -0.5)
    p = jax.nn.softmax(s, axis=-1).astype(v.dtype)
    return jnp.einsum('bkgqs,bskd->bqkgd', p, v)


def attn_mixer(h_lat, h_ctx, w_qkv, q_g, k_g, w_o, need_ctx):
    B, L, _ = h_lat.shape
    C = h_ctx.shape[1]
    nq = N_HEADS * HEAD_DIM
    nkv = N_KV * HEAD_DIM
    q_l, k_l, v_l = jnp.split(h_lat @ w_qkv, [nq, nq + nkv], axis=-1)
    q_l = rmsnorm(q_l.reshape(B, L, N_HEADS, HEAD_DIM), q_g)
    k_l = rmsnorm(k_l.reshape(B, L, N_KV, HEAD_DIM), k_g)
    v_l = v_l.reshape(B, L, N_KV, HEAD_DIM)
    ang_row, ang_col = axial_rope_angles(L)
    q_l = apply_axial_rope(q_l, ang_row, ang_col)
    k_l = apply_axial_rope(k_l, ang_row, ang_col)
    if need_ctx:
        q_c, k_c, v_c = jnp.split(h_ctx @ w_qkv, [nq, nq + nkv], axis=-1)
    else:
        k_c, v_c = jnp.split(h_ctx @ w_qkv[:, nq:], [nkv], axis=-1)
    k_c = rmsnorm(k_c.reshape(B, C, N_KV, HEAD_DIM), k_g)
    v_c = v_c.reshape(B, C, N_KV, HEAD_DIM)
    k_all = jnp.concatenate([k_c, k_l], axis=1)
    v_all = jnp.concatenate([v_c, v_l], axis=1)
    nb = L // Q_BLOCK
    qb = q_l.reshape(B, nb, Q_BLOCK, N_KV, GROUPS, HEAD_DIM).transpose(1, 0, 2, 3, 4, 5)
    o = lax.map(lambda qblk: gqa_attend(qblk, k_all, v_all), qb)
    o = o.transpose(1, 0, 2, 3, 4, 5).reshape(B, L, nq)
    out_lat = o @ w_o
    out_ctx = None
    if need_ctx:
        q_c = rmsnorm(q_c.reshape(B, C, N_HEADS, HEAD_DIM), q_g).reshape(B, C, N_KV, GROUPS, HEAD_DIM)
        out_ctx = gqa_attend(q_c, k_c, v_c).reshape(B, C, nq) @ w_o
    return out_lat, out_ctx


def swiglu(h, w1, w3, w2):
    return (jax.nn.silu(h @ w1) * (h @ w3)) @ w2


def moe_ffn(h, router, w1, w3, w2):
    shp = h.shape
    xt = h.reshape(-1, shp[-1])
    T = xt.shape[0]
    logits = (xt @ router).astype(jnp.float32)
    top_logit, top_idx = lax.top_k(logits, TOP_K)
    gates = jax.nn.softmax(top_logit, axis=-1).astype(h.dtype)
    TK = T * TOP_K
    slot_e = top_idx.reshape(TK)
    slot_tok = jnp.arange(TK, dtype=jnp.int32) // TOP_K
    slot_w = gates.reshape(TK)
    order = jnp.argsort(slot_e)
    e_sorted = slot_e[order]
    counts = jnp.bincount(slot_e, length=N_EXPERTS)
    padded = ((counts + MOE_BLOCK - 1) // MOE_BLOCK) * MOE_BLOCK
    start = jnp.cumsum(counts) - counts
    pend = jnp.cumsum(padded)
    pstart = pend - padded
    dest = pstart[e_sorted] + (jnp.arange(TK) - start[e_sorted])
    n_blocks = -(-TK // MOE_BLOCK) + N_EXPERTS
    P = n_blocks * MOE_BLOCK
    buf_tok = jnp.zeros((P,), jnp.int32).at[dest].set(slot_tok[order])
    buf_w = jnp.zeros((P,), h.dtype).at[dest].set(slot_w[order])
    blk_e = jnp.minimum(jnp.searchsorted(pend, jnp.arange(n_blocks) * MOE_BLOCK, side='right'), N_EXPERTS - 1)

    def run(args):
        tok, w, e = args
        xb = xt[tok]
        hid = jax.nn.silu(xb @ w1[e]) * (xb @ w3[e])
        return (hid @ w2[e]) * w[:, None]

    y = lax.map(run, (buf_tok.reshape(n_blocks, MOE_BLOCK), buf_w.reshape(n_blocks, MOE_BLOCK), blk_e))
    out = jnp.zeros_like(xt).at[buf_tok].add(y.reshape(P, shp[-1]))
    return out.reshape(shp)


def setup_inputs(seed: int = 0) -> dict:
    key = jax.random.key(seed)
    ks = jax.random.split(key, 32)
    f32 = jnp.float32

    def nrm(k, shape, fan_in, mult=1.0):
        return jax.random.normal(k, shape, f32) * (mult * fan_in ** -0.5)

    def gain(k, shape):
        return 1.0 + 0.1 * jax.random.normal(k, shape, f32)

    def bias(k, shape):
        return 0.02 * jax.random.normal(k, shape, f32)

    u = jax.random.uniform(ks[16], (N_A, 2, D_RNN), f32, minval=0.9, maxval=0.999)
    s = u ** (1.0 / LRU_C)
    rg_lam = jnp.log(s) - jnp.log1p(-s)
    qkv_out = (N_HEADS + 2 * N_KV) * HEAD_DIM
    return {
        "x": jax.random.normal(ks[0], (BATCH, SEQ, D_MODEL), f32),
        "c": jax.random.normal(ks[1], (BATCH, D_MODEL), f32),
        "ctx": jax.random.normal(ks[2], (BATCH, CTX_LEN, D_MODEL), f32),
        "c_ctx": jax.random.normal(ks[3], (D_MODEL,), f32),
        "ada_w": nrm(ks[4], (DEPTH, D_MODEL, 6 * D_MODEL), D_MODEL, 0.5),
        "ada_b": bias(ks[5], (DEPTH, 6 * D_MODEL)),
        "norm1_g": gain(ks[6], (DEPTH, D_MODEL)),
        "norm2_g": gain(ks[7], (DEPTH, D_MODEL)),
        "rg_w_in": nrm(ks[8], (N_A, D_MODEL, 2 * D_RNN), D_MODEL),
        "rg_conv_w": nrm(ks[9], (N_A, CONV_W, D_RNN), CONV_W),
        "rg_conv_b": bias(ks[10], (N_A, D_RNN)),
        "rg_w_a": nrm(ks[11], (N_A, 2, RG_BLOCKS, RG_BW, RG_BW), RG_BW),
        "rg_b_a": bias(ks[12], (N_A, 2, D_RNN)),
        "rg_w_i": nrm(ks[13], (N_A, 2, RG_BLOCKS, RG_BW, RG_BW), RG_BW),
        "rg_b_i": bias(ks[14], (N_A, 2, D_RNN)),
        "rg_lam": rg_lam,
        "rg_w_out": nrm(ks[15], (N_A, D_RNN, D_MODEL), D_RNN),
        "attn_w_qkv": nrm(ks[17], (N_B, D_MODEL, qkv_out), D_MODEL),
        "attn_q_g": gain(ks[18], (N_B, HEAD_DIM)),
        "attn_k_g": gain(ks[19], (N_B, HEAD_DIM)),
        "attn_w_o": nrm(ks[20], (N_B, N_HEADS * HEAD_DIM, D_MODEL), N_HEADS * HEAD_DIM),
        "ffn_w1": nrm(ks[21], (N_DENSE, D_MODEL, D_FF), D_MODEL),
        "ffn_w3": nrm(ks[22], (N_DENSE, D_MODEL, D_FF), D_MODEL),
        "ffn_w2": nrm(ks[23], (N_DENSE, D_FF, D_MODEL), D_FF),
        "moe_router": nrm(ks[24], (N_MOE, D_MODEL, N_EXPERTS), D_MODEL),
        "moe_w1": nrm(ks[25], (N_MOE, N_EXPERTS, D_MODEL, D_FF_EXPERT), D_MODEL),
        "moe_w3": nrm(ks[26], (N_MOE, N_EXPERTS, D_MODEL, D_FF_EXPERT), D_MODEL),
        "moe_w2": nrm(ks[27], (N_MOE, N_EXPERTS, D_FF_EXPERT, D_MODEL), D_FF_EXPERT),
        "final_g": gain(ks[28], (D_MODEL,)),
    }


def reference(x, c, ctx, c_ctx, ada_w, ada_b, norm1_g, norm2_g,
              rg_w_in, rg_conv_w, rg_conv_b, rg_w_a, rg_b_a, rg_w_i, rg_b_i, rg_lam, rg_w_out,
              attn_w_qkv, attn_q_g, attn_k_g, attn_w_o,
              ffn_w1, ffn_w3, ffn_w2,
              moe_router, moe_w1, moe_w3, moe_w2, final_g):
    silu_c = jax.nn.silu(c)
    silu_cc = jax.nn.silu(c_ctx)
    for i in range(DEPTH):
        need_ctx = i < DEPTH - 1
        j = i // N_MIXERS
        mod_l = (silu_c @ ada_w[i] + ada_b[i])[:, None, :]
        sh1, sc1, g1, sh2, sc2, g2 = jnp.split(mod_l, 6, axis=-1)
        mod_c = silu_cc @ ada_w[i] + ada_b[i]
        csh1, csc1, cg1, csh2, csc2, cg2 = jnp.split(mod_c, 6, axis=-1)
        h_l = modulate(rmsnorm(x, norm1_g[i]), sh1, sc1)
        h_c = modulate(rmsnorm(ctx, norm1_g[i]), csh1, csc1)
        if i % N_MIXERS == 0:
            m_l, m_c = rglru_mixer(h_l, h_c, rg_w_in[j], rg_conv_w[j], rg_conv_b[j], rg_w_a[j], rg_b_a[j],
                                   rg_w_i[j], rg_b_i[j], rg_lam[j], rg_w_out[j], need_ctx)
        else:
            m_l, m_c = attn_mixer(h_l, h_c, attn_w_qkv[j], attn_q_g[j], attn_k_g[j], attn_w_o[j], need_ctx)
        x = x + g1 * m_l
        h_l = modulate(rmsnorm(x, norm2_g[i]), sh2, sc2)
        if i % 2 == 0:
            ffn = lambda h: swiglu(h, ffn_w1[j], ffn_w3[j], ffn_w2[j])
        else:
            ffn = lambda h: moe_ffn(h, moe_router[j], moe_w1[j], moe_w3[j], moe_w2[j])
        x = x + g2 * ffn(h_l)
        if need_ctx:
            ctx = ctx + cg1 * m_c
            h_c = modulate(rmsnorm(ctx, norm2_g[i]), csh2, csc2)
            ctx = ctx + cg2 * ffn(h_c)
    return rmsnorm(x, final_g)
```

```python
import functools

import jax
import jax.numpy as jnp
from jax import lax
from jax.experimental import pallas as pl
from jax.experimental.pallas import tpu as pltpu

F32 = jnp.float32
BF16 = jnp.bfloat16

_EPS = 1e-6
_HEAD_DIM = 128
_GRID_W = 64
_ROPE_THETA = 10000.0
_ROPE_NFREQ = _HEAD_DIM // 4
_LRU_C = 8.0
_CONV_W = 4
_LANES = 128
_SUBLANES = 8
_LOG2E = 1.4426950408889634
_MIB = 1024 * 1024


def _params(semantics, vmem_mib):
    return pltpu.CompilerParams(dimension_semantics=semantics, vmem_limit_bytes=vmem_mib * _MIB)


def _tile(n, pref):
    if n <= pref:
        return n
    t = pref
    while n % t:
        t //= 2
    return t


def _dot(a, b):
    return jnp.dot(a, b, preferred_element_type=F32)


def _split_bf16(a):
    hi = a.astype(BF16)
    lo = (a - hi.astype(F32)).astype(BF16)
    return hi, lo


def _dot3(a, b):
    ah, al = _split_bf16(a)
    bh, bl = _split_bf16(b)
    return _dot(ah, bh) + (_dot(al, bh) + _dot(ah, bl))


def _rms_mod(x, g, shift, scale):
    y = x * lax.rsqrt(jnp.mean(x * x, axis=-1, keepdims=True) + _EPS)
    return (y * g) * (1.0 + scale) + shift


def _sigmoid(x):
    return 1.0 / (1.0 + jnp.exp(-x))


def _ada_kernel(c_ref, w_ref, b_ref, o_ref):
    c = c_ref[...]
    s = c * _sigmoid(c)
    o_ref[...] = _dot3(s, w_ref[...]) + b_ref[...]


def _ada_call(c_rows, ada_w, ada_b):
    depth, d, n = ada_w.shape
    rows = c_rows.shape[0]
    tn = _tile(n, 1536)
    return pl.pallas_call(
        _ada_kernel,
        grid=(depth, n // tn),
        in_specs=[
            pl.BlockSpec((rows, d), lambda i, j: (0, 0)),
            pl.BlockSpec((None, d, tn), lambda i, j: (i, 0, j)),
            pl.BlockSpec((None, 1, tn), lambda i, j: (i, 0, j)),
        ],
        out_specs=pl.BlockSpec((None, rows, tn), lambda i, j: (i, 0, j)),
        out_shape=jax.ShapeDtypeStruct((depth, rows, n), F32),
        compiler_params=_params(("parallel", "parallel"), 40),
        name="ada_mod",
    )(c_rows, ada_w, ada_b.reshape(depth, 1, n))


def _nmm_kernel(x_ref, g_ref, mod_ref, w_ref, o_ref, *, sh, sc):
    h = _rms_mod(x_ref[...], g_ref[...], mod_ref[sh:sh + 1, :], mod_ref[sc:sc + 1, :])
    o_ref[...] = _dot(h.astype(BF16), w_ref[...]).astype(o_ref.dtype)


def _nmm_call(x, g, mod, w, sh, sc, out_dtype):
    b, l, d = x.shape
    n = w.shape[1]
    tm = _tile(l, 512)
    return pl.pallas_call(
        functools.partial(_nmm_kernel, sh=sh, sc=sc),
        grid=(b, l // tm),
        in_specs=[
            pl.BlockSpec((None, tm, d), lambda i, t: (i, t, 0)),
            pl.BlockSpec((1, d), lambda i, t: (0, 0)),
            pl.BlockSpec((None, 6, d), lambda i, t: (i, 0, 0)),
            pl.BlockSpec((d, n), lambda i, t: (0, 0)),
        ],
        out_specs=pl.BlockSpec((None, tm, n), lambda i, t: (i, t, 0)),
        out_shape=jax.ShapeDtypeStruct((b, l, n), out_dtype),
        compiler_params=_params(("parallel", "parallel"), 48),
        name="norm_mod_matmul",
    )(x, g.reshape(1, d), mod, w)


def _log1p_pos(e):
    u = 1.0 + e
    den = jnp.where(u == 1.0, 1.0, u - 1.0)
    return jnp.where(u == 1.0, e, jnp.log(u) * (e / den))


def _softplus(x):
    return jnp.maximum(x, 0.0) + _log1p_pos(jnp.exp(-jnp.abs(x)))


def _expm1_neg(x):
    u = jnp.exp(x)
    lu = jnp.log(jnp.where(u == 0.0, 1.0, u))
    den = jnp.where(lu == 0.0, 1.0, lu)
    r = (u - 1.0) * (x / den)
    return jnp.where(u == 1.0, x, jnp.where(u == 0.0, -1.0, r))


def _scan_tile(a, b, reverse):
    n = a.shape[0]
    row = lax.broadcasted_iota(jnp.int32, a.shape, 0)
    s = 1
    while s < n:
        if reverse:
            a_sh = pltpu.roll(a, n - s, axis=0)
            b_sh = pltpu.roll(b, n - s, axis=0)
            m = row < (n - s)
        else:
            a_sh = pltpu.roll(a, s, axis=0)
            b_sh = pltpu.roll(b, s, axis=0)
            m = row >= s
        b = jnp.where(m, a * b_sh + b, b)
        a = jnp.where(m, a * a_sh, a)
        s *= 2
    return a, b


def _lru_kernel(*refs, reverse, tl, nt, fuse):
    if fuse:
        (xm_ref, xp_ref, xn_ref, cw_ref, cb_ref, wa_ref, ba_ref, wi_ref, bi_ref, lam_ref, h0_ref,
         yo_ref, gb_ref, out_ref, hl_ref, carry_ref) = refs
    else:
        (xm_ref, xp_ref, xn_ref, cw_ref, cb_ref, wa_ref, ba_ref, wi_ref, bi_ref, lam_ref, h0_ref,
         out_ref, hl_ref, carry_ref) = refs
    t = pl.program_id(2)
    tt = (nt - 1 - t) if reverse else t

    @pl.when(t == 0)
    def _():
        carry_ref[...] = h0_ref[...]

    xm = xm_ref[...]
    prev = jnp.where(tt > 0, xp_ref[...], 0.0)
    nxt = jnp.where(tt < nt - 1, xn_ref[...], 0.0)
    xpad = jnp.concatenate([prev, xm, nxt], axis=0)
    cw = cw_ref[...]
    left = _CONV_W // 2
    xc = cb_ref[...]
    for k in range(_CONV_W):
        off = _SUBLANES - left + k
        xc = xc + xpad[off:off + tl, :] * cw[k:k + 1, :]

    xcb = xc.astype(BF16)
    r = _sigmoid(_dot(xcb, wa_ref[...]) + ba_ref[...])
    i = _sigmoid(_dot(xcb, wi_ref[...]) + bi_ref[...])
    log_a = (-_LRU_C * r) * _softplus(-lam_ref[...])
    a = jnp.exp(log_a)
    mult = jnp.sqrt(-_expm1_neg(2.0 * log_a))
    b = mult * (i * xc)

    a_cum, b_cum = _scan_tile(a, b, reverse)
    hs = a_cum * carry_ref[...] + b_cum
    new_carry = hs[0:1, :] if reverse else hs[tl - 1:tl, :]
    carry_ref[...] = new_carry
    hl_ref[...] = new_carry
    if fuse:
        gate = jax.nn.gelu(gb_ref[...], approximate=True)
        out_ref[...] = ((hs + yo_ref[...]) * gate).astype(out_ref.dtype)
    else:
        out_ref[...] = hs


def _lru_call(xg, conv_w, conv_b, w_a, b_a, w_i, b_i, lam, h0, y_other, *, reverse):
    b, l, c2 = xg.shape
    c = c2 // 2
    nb = c // _LANES
    tl = _tile(l, 512)
    nt = l // tl
    hb = tl // _SUBLANES
    nh = l // _SUBLANES
    fuse = y_other is not None

    def tmap(t):
        return (nt - 1 - t) if reverse else t

    vec = lambda: pl.BlockSpec((1, _LANES), lambda i, j, t: (0, j))
    in_specs = [
        pl.BlockSpec((None, tl, _LANES), lambda i, j, t: (i, tmap(t), j)),
        pl.BlockSpec((None, _SUBLANES, _LANES), lambda i, j, t: (i, jnp.maximum(tmap(t) * hb - 1, 0), j)),
        pl.BlockSpec((None, _SUBLANES, _LANES), lambda i, j, t: (i, jnp.minimum((tmap(t) + 1) * hb, nh - 1), j)),
        pl.BlockSpec((_CONV_W, _LANES), lambda i, j, t: (0, j)),
        vec(),
        pl.BlockSpec((None, _LANES, _LANES), lambda i, j, t: (j, 0, 0)),
        vec(),
        pl.BlockSpec((None, _LANES, _LANES), lambda i, j, t: (j, 0, 0)),
        vec(),
        vec(),
        pl.BlockSpec((None, 1, _LANES), lambda i, j, t: (i, 0, j)),
    ]
    args = [xg, xg, xg, conv_w, conv_b.reshape(1, c), w_a, b_a.reshape(1, c), w_i, b_i.reshape(1, c),
            lam.reshape(1, c), h0]
    if fuse:
        in_specs += [
            pl.BlockSpec((None, tl, _LANES), lambda i, j, t: (i, tmap(t), j)),
            pl.BlockSpec((None, tl, _LANES), lambda i, j, t: (i, tmap(t), nb + j)),
        ]
        args += [y_other, xg]
    out_dtype = BF16 if fuse else F32
    return pl.pallas_call(
        functools.partial(_lru_kernel, reverse=reverse, tl=tl, nt=nt, fuse=fuse),
        grid=(b, nb, nt),
        in_specs=in_specs,
        out_specs=[
            pl.BlockSpec((None, tl, _LANES), lambda i, j, t: (i, tmap(t), j)),
            pl.BlockSpec((None, 1, _LANES), lambda i, j, t: (i, 0, j)),
        ],
        out_shape=[jax.ShapeDtypeStruct((b, l, c), out_dtype), jax.ShapeDtypeStruct((b, 1, c), F32)],
        scratch_shapes=[pltpu.VMEM((1, _LANES), F32)],
        compiler_params=_params(("parallel", "parallel", "arbitrary"), 32),
        name="lru_rev" if reverse else "lru_fwd",
    )(*args)


def _proj_res_kernel(z_ref, w_ref, x_ref, mod_ref, o_ref, *, gi):
    o_ref[...] = x_ref[...] + mod_ref[gi:gi + 1, :] * _dot(z_ref[...], w_ref[...])


def _proj_res_call(z, w, x, mod, gi):
    b, l, d = x.shape
    k = z.shape[2]
    tm = _tile(l, 1024)
    return pl.pallas_call(
        functools.partial(_proj_res_kernel, gi=gi),
        grid=(b, l // tm),
        in_specs=[
            pl.BlockSpec((None, tm, k), lambda i, t: (i, t, 0)),
            pl.BlockSpec((k, d), lambda i, t: (0, 0)),
            pl.BlockSpec((None, tm, d), lambda i, t: (i, t, 0)),
            pl.BlockSpec((None, 6, d), lambda i, t: (i, 0, 0)),
        ],
        out_specs=pl.BlockSpec((None, tm, d), lambda i, t: (i, t, 0)),
        out_shape=jax.ShapeDtypeStruct((b, l, d), F32),
        compiler_params=_params(("parallel", "parallel"), 48),
        name="proj_residual",
    )(z, w, x, mod)


def _ffn_kernel(x_ref, g_ref, mod_ref, w1_ref, w3_ref, w2_ref, o_ref, h_ref, acc_ref, *, nf):
    f = pl.program_id(2)

    @pl.when(f == 0)
    def _():
        h = _rms_mod(x_ref[...], g_ref[...], mod_ref[3:4, :], mod_ref[4:5, :])
        h_ref[...] = h.astype(BF16)
        acc_ref[...] = jnp.zeros_like(acc_ref)

    h = h_ref[...]
    h1 = _dot(h, w1_ref[...])
    h3 = _dot(h, w3_ref[...])
    hid = (h1 * _sigmoid(h1)) * h3
    acc_ref[...] += _dot(hid.astype(BF16), w2_ref[...])

    @pl.when(f == nf - 1)
    def _():
        o_ref[...] = x_ref[...] + mod_ref[5:6, :] * acc_ref[...]


def _ffn_call(x, g, mod, w1, w3, w2):
    b, l, d = x.shape
    ff = w1.shape[1]
    tm = _tile(l, 1024)
    tf = 256 if ff % 256 == 0 else ff
    nf = ff // tf
    return pl.pallas_call(
        functools.partial(_ffn_kernel, nf=nf),
        grid=(b, l // tm, nf),
        in_specs=[
            pl.BlockSpec((None, tm, d), lambda i, t, f: (i, t, 0)),
            pl.BlockSpec((1, d), lambda i, t, f: (0, 0)),
            pl.BlockSpec((None, 6, d), lambda i, t, f: (i, 0, 0)),
            pl.BlockSpec((d, tf), lambda i, t, f: (0, f)),
            pl.BlockSpec((d, tf), lambda i, t, f: (0, f)),
            pl.BlockSpec((tf, d), lambda i, t, f: (f, 0)),
        ],
        out_specs=pl.BlockSpec((None, tm, d), lambda i, t, f: (i, t, 0)),
        out_shape=jax.ShapeDtypeStruct((b, l, d), F32),
        scratch_shapes=[pltpu.VMEM((tm, d), BF16), pltpu.VMEM((tm, d), F32)],
        compiler_params=_params(("parallel", "parallel", "arbitrary"), 48),
        name="dense_swiglu",
    )(x, g.reshape(1, d), mod, w1, w3, w2)


def _rope_tables(l):
    pos = jnp.arange(l)
    row = (pos // _GRID_W).astype(F32)
    col = (pos % _GRID_W).astype(F32)
    freqs = _ROPE_THETA ** (-jnp.arange(_ROPE_NFREQ, dtype=F32) / _ROPE_NFREQ)
    ar = row[:, None] * freqs
    ac = col[:, None] * freqs
    cos = jnp.concatenate([jnp.cos(ar), jnp.cos(ar), jnp.cos(ac), jnp.cos(ac)], axis=-1)
    sin = jnp.concatenate([-jnp.sin(ar), jnp.sin(ar), -jnp.sin(ac), jnp.sin(ac)], axis=-1)
    return cos, sin


def _qkv_kernel(x_ref, g_ref, mod_ref, w_ref, qg_ref, kg_ref, cos_ref, sin_ref, q_ref, k_ref, v_ref,
                *, n_q, n_kv, q_scale):
    h = _rms_mod(x_ref[...], g_ref[...], mod_ref[0:1, :], mod_ref[1:2, :])
    qkv = _dot(h.astype(BF16), w_ref[...])
    cos = cos_ref[...]
    sin = sin_ref[...]
    lane = lax.broadcasted_iota(jnp.int32, cos.shape, 1)
    first = (lane % (2 * _ROPE_NFREQ)) < _ROPE_NFREQ

    def norm_rope(v, gain):
        y = v * lax.rsqrt(jnp.mean(v * v, axis=-1, keepdims=True) + _EPS) * gain
        partner = jnp.where(first, pltpu.roll(y, _HEAD_DIM - _ROPE_NFREQ, axis=1),
                            pltpu.roll(y, _ROPE_NFREQ, axis=1))
        return y * cos + partner * sin

    for hh in range(n_q):
        sl = slice(hh * _HEAD_DIM, (hh + 1) * _HEAD_DIM)
        q_ref[:, sl] = (norm_rope(qkv[:, sl], qg_ref[...]) * q_scale).astype(q_ref.dtype)
    for hh in range(n_kv):
        src = slice((n_q + hh) * _HEAD_DIM, (n_q + hh + 1) * _HEAD_DIM)
        dst = slice(hh * _HEAD_DIM, (hh + 1) * _HEAD_DIM)
        k_ref[:, dst] = norm_rope(qkv[:, src], kg_ref[...]).astype(k_ref.dtype)
    v0 = (n_q + n_kv) * _HEAD_DIM
    v_ref[...] = qkv[:, v0:v0 + n_kv * _HEAD_DIM].astype(v_ref.dtype)


def _qkv_call(x, g, mod, w, q_g, k_g, cos, sin, n_q, n_kv):
    b, l, d = x.shape
    n = w.shape[1]
    tm = _tile(l, 512)
    nq = n_q * _HEAD_DIM
    nkv = n_kv * _HEAD_DIM
    q_scale = (_HEAD_DIM ** -0.5) * _LOG2E
    return pl.pallas_call(
        functools.partial(_qkv_kernel, n_q=n_q, n_kv=n_kv, q_scale=q_scale),
        grid=(b, l // tm),
        in_specs=[
            pl.BlockSpec((None, tm, d), lambda i, t: (i, t, 0)),
            pl.BlockSpec((1, d), lambda i, t: (0, 0)),
            pl.BlockSpec((None, 6, d), lambda i, t: (i, 0, 0)),
            pl.BlockSpec((d, n), lambda i, t: (0, 0)),
            pl.BlockSpec((1, _HEAD_DIM), lambda i, t: (0, 0)),
            pl.BlockSpec((1, _HEAD_DIM), lambda i, t: (0, 0)),
            pl.BlockSpec((tm, _HEAD_DIM), lambda i, t: (t, 0)),
            pl.BlockSpec((tm, _HEAD_DIM), lambda i, t: (t, 0)),
        ],
        out_specs=[
            pl.BlockSpec((None, tm, nq), lambda i, t: (i, t, 0)),
            pl.BlockSpec((None, tm, nkv), lambda i, t: (i, t, 0)),
            pl.BlockSpec((None, tm, nkv), lambda i, t: (i, t, 0)),
        ],
        out_shape=[
            jax.ShapeDtypeStruct((b, l, nq), BF16),
            jax.ShapeDtypeStruct((b, l, nkv), BF16),
            jax.ShapeDtypeStruct((b, l, nkv), BF16),
        ],
        compiler_params=_params(("parallel", "parallel"), 48),
        name="qkv_norm_rope",
    )(x, g.reshape(1, d), mod, w, q_g.reshape(1, _HEAD_DIM), k_g.reshape(1, _HEAD_DIM), cos, sin)


def _lane_tile(x, reps):
    return x if reps == 1 else jnp.concatenate([x] * reps, axis=1)


def _attn_kernel(q_ref, k_ref, v_ref, o_ref, m_ref, l_ref, acc_ref, *, tq, tk, groups, n_chunks):
    q = jnp.concatenate([q_ref[:, g * _HEAD_DIM:(g + 1) * _HEAD_DIM] for g in range(groups)], axis=0)
    m_ref[...] = jnp.full_like(m_ref, -jnp.inf)
    l_ref[...] = jnp.zeros_like(l_ref)
    acc_ref[...] = jnp.zeros_like(acc_ref)

    def body(c, carry):
        start = pl.multiple_of(c * tk, tk)
        k = k_ref[pl.ds(start, tk), :]
        v = v_ref[pl.ds(start, tk), :]
        s = lax.dot_general(q, k, (((1,), (1,)), ((), ())), preferred_element_type=F32)
        m_prev = m_ref[...]
        m_next = jnp.maximum(m_prev, jnp.max(s, axis=1, keepdims=True))
        alpha = jnp.exp2(m_prev - m_next)
        p = jnp.exp2(s - _lane_tile(m_next, tk // _LANES))
        l_ref[...] = alpha * l_ref[...] + jnp.sum(p, axis=1, keepdims=True)
        acc_ref[...] = alpha * acc_ref[...] + _dot(p.astype(BF16), v)
        m_ref[...] = m_next
        return carry

    lax.fori_loop(0, n_chunks, body, 0)
    o = acc_ref[...] / l_ref[...]
    for g in range(groups):
        o_ref[:, g * _HEAD_DIM:(g + 1) * _HEAD_DIM] = o[g * tq:(g + 1) * tq, :].astype(o_ref.dtype)


def _attn_call(q, k, v, n_kv):
    b, l, nq = q.shape
    s = k.shape[1]
    groups = nq // (n_kv * _HEAD_DIM)
    gw = groups * _HEAD_DIM
    tq = _tile(l, 256)
    tk = 768 if s % 768 == 0 else _tile(s, 512)
    n_chunks = s // tk
    m_rows = groups * tq
    return pl.pallas_call(
        functools.partial(_attn_kernel, tq=tq, tk=tk, groups=groups, n_chunks=n_chunks),
        grid=(b, n_kv, l // tq),
        in_specs=[
            pl.BlockSpec((None, tq, gw), lambda i, j, t: (i, t, j)),
            pl.BlockSpec((None, s, _HEAD_DIM), lambda i, j, t: (i, 0, j)),
            pl.BlockSpec((None, s, _HEAD_DIM), lambda i, j, t: (i, 0, j)),
        ],
        out_specs=pl.BlockSpec((None, tq, gw), lambda i, j, t: (i, t, j)),
        out_shape=jax.ShapeDtypeStruct((b, l, nq), BF16),
        scratch_shapes=[
            pltpu.VMEM((m_rows, _LANES), F32),
            pltpu.VMEM((m_rows, _LANES), F32),
            pltpu.VMEM((m_rows, _HEAD_DIM), F32),
        ],
        compiler_params=_params(("parallel", "parallel", "arbitrary"), 48),
        name="gqa_flash_attention",
    )(q, k, v)


_INFO_E0, _INFO_E1, _INFO_R0, _INFO_R1, _INFO_G0, _INFO_G1 = range(6)


def _route_kernel(x_ref, g_ref, mod_ref, rw_ref, h_ref, info_ref, cnt_ref, carry_ref, *, n_exp, tm):
    step = pl.program_id(0)

    @pl.when(step == 0)
    def _():
        carry_ref[...] = jnp.zeros_like(carry_ref)

    h = _rms_mod(x_ref[...], g_ref[...], mod_ref[3:4, :], mod_ref[4:5, :])
    h_ref[...] = h
    logits = _dot3(h, rw_ref[...])
    lane = lax.broadcasted_iota(jnp.int32, logits.shape, 1)
    neg = jnp.float32(-jnp.inf)
    lg = jnp.where(lane < n_exp, logits, neg)
    m0 = jnp.max(lg, axis=1, keepdims=True)
    i0 = jnp.min(jnp.where(lg == m0, lane, _LANES), axis=1, keepdims=True)
    oh0 = lane == i0
    lg1 = jnp.where(oh0, neg, lg)
    m1 = jnp.max(lg1, axis=1, keepdims=True)
    i1 = jnp.min(jnp.where(lg1 == m1, lane, _LANES), axis=1, keepdims=True)
    oh1 = lane == i1
    e = jnp.exp(m1 - m0)
    g0 = 1.0 / (1.0 + e)
    g1 = e / (1.0 + e)

    chosen = jnp.where(oh0 | oh1, 1.0, 0.0)
    r = lax.broadcasted_iota(jnp.int32, (tm, tm), 0)
    cidx = lax.broadcasted_iota(jnp.int32, (tm, tm), 1)
    tri = jnp.where(r > cidx, 1.0, 0.0).astype(BF16)
    before = _dot(tri, chosen.astype(BF16)) + carry_ref[...]
    r0 = jnp.sum(jnp.where(oh0, before, 0.0), axis=1, keepdims=True)
    r1 = jnp.sum(jnp.where(oh1, before, 0.0), axis=1, keepdims=True)
    total = carry_ref[...] + jnp.sum(chosen, axis=0, keepdims=True)
    carry_ref[...] = total
    cnt_ref[...] = total

    info = jnp.zeros(logits.shape, F32)
    for idx, val in ((_INFO_E0, i0.astype(F32)), (_INFO_E1, i1.astype(F32)), (_INFO_R0, r0), (_INFO_R1, r1),
                     (_INFO_G0, g0), (_INFO_G1, g1)):
        info = jnp.where(lane == idx, val, info)
    info_ref[...] = info


def _route_call(x, g, mod, router):
    b, l, d = x.shape
    n_exp = router.shape[1]
    t = b * l
    tm = _tile(l, 512)
    per_b = l // tm
    rw = jnp.zeros((d, _LANES), F32).at[:, :n_exp].set(router)
    return pl.pallas_call(
        functools.partial(_route_kernel, n_exp=n_exp, tm=tm),
        grid=(t // tm,),
        in_specs=[
            pl.BlockSpec((tm, d), lambda i: (i, 0)),
            pl.BlockSpec((1, d), lambda i: (0, 0)),
            pl.BlockSpec((None, 6, d), lambda i: (i // per_b, 0, 0)),
            pl.BlockSpec((d, _LANES), lambda i: (0, 0)),
        ],
        out_specs=[
            pl.BlockSpec((tm, d), lambda i: (i, 0)),
            pl.BlockSpec((tm, _LANES), lambda i: (i, 0)),
            pl.BlockSpec((1, _LANES), lambda i: (0, 0)),
        ],
        out_shape=[
            jax.ShapeDtypeStruct((t, d), F32),
            jax.ShapeDtypeStruct((t, _LANES), F32),
            jax.ShapeDtypeStruct((1, _LANES), F32),
        ],
        scratch_shapes=[pltpu.VMEM((1, _LANES), F32)],
        compiler_params=_params(("arbitrary",), 40),
        name="moe_route",
    )(x.reshape(t, d), g.reshape(1, d), mod, rw)


def _pos_tiles(pos0, pos1, tt):
    n = pos0.shape[0] // tt
    p = jnp.stack([pos0.reshape(n, tt // _LANES, _LANES), pos1.reshape(n, tt // _LANES, _LANES)], axis=1)
    return p.reshape(n * 2 * tt // _LANES, _LANES)


def _row_copy(src, src_row, dst, dst_row, sem):
    return pltpu.make_async_copy(src.at[pl.ds(src_row, 1)], dst.at[pl.ds(dst_row, 1)], sem)


def _dispatch_kernel(pos_ref, h_ref, xs_in_ref, xs_ref, pos_smem, sem_s, sem_d, *, tt):
    del xs_in_ref
    cp = pltpu.make_async_copy(pos_ref, pos_smem, sem_s)
    cp.start()
    cp.wait()
    half = tt // _LANES

    def body(r, carry):
        hi = lax.shift_right_logical(r, 7)
        lo = lax.bitwise_and(r, _LANES - 1)
        _row_copy(h_ref, r, xs_ref, pos_smem[hi, lo], sem_d).start()
        _row_copy(h_ref, r, xs_ref, pos_smem[half + hi, lo], sem_d).start()
        return carry

    lax.fori_loop(0, tt, body, 0)
    for _ in range(2 * tt):
        _row_copy(h_ref, 0, xs_ref, 0, sem_d).wait()


def _dispatch_call(h, pos_t, p_rows, tt):
    t, d = h.shape
    xs0 = jnp.zeros((p_rows, d), F32)
    return pl.pallas_call(
        functools.partial(_dispatch_kernel, tt=tt),
        grid=(t // tt,),
        in_specs=[
            pl.BlockSpec((2 * tt // _LANES, _LANES), lambda i: (i, 0)),
            pl.BlockSpec((tt, d), lambda i: (i, 0)),
            pl.BlockSpec(memory_space=pl.ANY),
        ],
        out_specs=pl.BlockSpec(memory_space=pl.ANY),
        out_shape=jax.ShapeDtypeStruct((p_rows, d), F32),
        scratch_shapes=[
            pltpu.SMEM((2 * tt // _LANES, _LANES), jnp.int32),
            pltpu.SemaphoreType.DMA(()),
            pltpu.SemaphoreType.DMA(()),
        ],
        input_output_aliases={2: 0},
        compiler_params=_params(("arbitrary",), 32),
        name="moe_dispatch",
    )(pos_t, h, xs0)


def _moe_kernel(be_ref, bs_ref, bv_ref, x_ref, w1_ref, w3_ref, w2_ref, o_ref, xb_ref, acc_ref, *, nf):
    del be_ref, bs_ref
    i = pl.program_id(0)
    f = pl.program_id(1)

    @pl.when(bv_ref[i] > 0)
    def _():
        @pl.when(f == 0)
        def _():
            xb_ref[...] = x_ref[...].astype(BF16)
            acc_ref[...] = jnp.zeros_like(acc_ref)

        xb = xb_ref[...]
        h1 = _dot(xb, w1_ref[...])
        h3 = _dot(xb, w3_ref[...])
        hid = (h1 * _sigmoid(h1)) * h3
        acc_ref[...] += _dot(hid.astype(BF16), w2_ref[...])

        @pl.when(f == nf - 1)
        def _():
            o_ref[...] = acc_ref[...]


def _moe_call(blk_e, blk_src, blk_valid, xs, w1, w3, w2, tmm):
    p_rows, d = xs.shape
    ff = w1.shape[2]
    tf = 896 if ff % 896 == 0 else _tile(ff, 512)
    nf = ff // tf
    nblk = p_rows // tmm

    def fsel(i, f, bv):
        return jnp.where(bv[i] > 0, f, nf - 1)

    grid_spec = pltpu.PrefetchScalarGridSpec(
        num_scalar_prefetch=3,
        grid=(nblk, nf),
        in_specs=[
            pl.BlockSpec((tmm, d), lambda i, f, be, bs, bv: (bs[i], 0)),
            pl.BlockSpec((None, d, tf), lambda i, f, be, bs, bv: (be[i], 0, fsel(i, f, bv))),
            pl.BlockSpec((None, d, tf), lambda i, f, be, bs, bv: (be[i], 0, fsel(i, f, bv))),
            pl.BlockSpec((None, tf, d), lambda i, f, be, bs, bv: (be[i], fsel(i, f, bv), 0)),
        ],
        out_specs=pl.BlockSpec((tmm, d), lambda i, f, be, bs, bv: (bs[i], 0)),
        scratch_shapes=[pltpu.VMEM((tmm, d), BF16), pltpu.VMEM((tmm, d), F32)],
    )
    return pl.pallas_call(
        functools.partial(_moe_kernel, nf=nf),
        grid_spec=grid_spec,
        out_shape=jax.ShapeDtypeStruct((p_rows, d), F32),
        compiler_params=_params(("arbitrary", "arbitrary"), 52),
        name="moe_experts",
    )(blk_e, blk_src, blk_valid, xs, w1, w3, w2)


def _combine_kernel(pos_ref, x_ref, mod_ref, info_ref, fg_ref, y_ref, o_ref, pos_smem, ybuf, sem_s, sem_d,
                    *, tt):
    cp = pltpu.make_async_copy(pos_ref, pos_smem, sem_s)
    cp.start()
    cp.wait()
    half = tt // _LANES

    def body(r, carry):
        hi = lax.shift_right_logical(r, 7)
        lo = lax.bitwise_and(r, _LANES - 1)
        _row_copy(y_ref, pos_smem[hi, lo], ybuf.at[0], r, sem_d).start()
        _row_copy(y_ref, pos_smem[half + hi, lo], ybuf.at[1], r, sem_d).start()
        return carry

    lax.fori_loop(0, tt, body, 0)
    for _ in range(2 * tt):
        _row_copy(y_ref, 0, ybuf.at[0], 0, sem_d).wait()

    info = info_ref[...]
    g0 = info[:, _INFO_G0:_INFO_G0 + 1]
    g1 = info[:, _INFO_G1:_INFO_G1 + 1]
    ffn = g0 * ybuf[0] + g1 * ybuf[1]
    xo = x_ref[...] + mod_ref[5:6, :] * ffn
    o_ref[...] = xo * lax.rsqrt(jnp.mean(xo * xo, axis=-1, keepdims=True) + _EPS) * fg_ref[...]


def _combine_call(pos_t, x, mod, info, final_g, y, tt):
    b, l, d = x.shape
    t = b * l
    per_b = l // tt
    return pl.pallas_call(
        functools.partial(_combine_kernel, tt=tt),
        grid=(t // tt,),
        in_specs=[
            pl.BlockSpec((2 * tt // _LANES, _LANES), lambda i: (i, 0)),
            pl.BlockSpec((tt, d), lambda i: (i, 0)),
            pl.BlockSpec((None, 6, d), lambda i: (i // per_b, 0, 0)),
            pl.BlockSpec((tt, _LANES), lambda i: (i, 0)),
            pl.BlockSpec((1, d), lambda i: (0, 0)),
            pl.BlockSpec(memory_space=pl.ANY),
        ],
        out_specs=pl.BlockSpec((tt, d), lambda i: (i, 0)),
        out_shape=jax.ShapeDtypeStruct((t, d), F32),
        scratch_shapes=[
            pltpu.SMEM((2 * tt // _LANES, _LANES), jnp.int32),
            pltpu.VMEM((2, tt, d), F32),
            pltpu.SemaphoreType.DMA(()),
            pltpu.SemaphoreType.DMA(()),
        ],
        compiler_params=_params(("arbitrary",), 40),
        name="moe_combine_norm",
    )(pos_t, x.reshape(t, d), mod, info, final_g.reshape(1, d), y).reshape(b, l, d)


def _moe_layer(x, g, mod, router, w1, w3, w2, final_g):
    b, l, d = x.shape
    t = b * l
    n_exp = router.shape[1]
    tmm = 512 if t >= 4096 else 128
    tt = _tile(l, 512)
    h, info, counts = _route_call(x, g, mod, router)

    counts = counts[0, :n_exp].astype(jnp.int32)
    padded = ((counts + tmm - 1) // tmm) * tmm
    pend = jnp.cumsum(padded)
    gstart = pend - padded
    e0 = info[:, _INFO_E0].astype(jnp.int32)
    e1 = info[:, _INFO_E1].astype(jnp.int32)
    pos0 = gstart[e0] + info[:, _INFO_R0].astype(jnp.int32)
    pos1 = gstart[e1] + info[:, _INFO_R1].astype(jnp.int32)
    pos_t = _pos_tiles(pos0, pos1, tt)

    nblk = (t * 2) // tmm + n_exp
    p_rows = nblk * tmm
    bstart = jnp.arange(nblk, dtype=jnp.int32) * tmm
    used = bstart < pend[-1]
    last = pend[-1] // tmm - 1
    bidx = jnp.where(used, jnp.arange(nblk, dtype=jnp.int32), last)
    blk_e = jnp.minimum(jnp.searchsorted(pend, bidx * tmm, side="right"), n_exp - 1).astype(jnp.int32)
    blk_valid = jnp.where(used, jnp.clip(counts[blk_e] - (bidx * tmm - gstart[blk_e]), 0, tmm), 0).astype(jnp.int32)

    xs = _dispatch_call(h, pos_t, p_rows, tt)
    y = _moe_call(blk_e, bidx.astype(jnp.int32), blk_valid, xs, w1.astype(BF16), w3.astype(BF16), w2.astype(BF16),
                  tmm)
    return _combine_call(pos_t, x, mod, info, final_g, y, tt)


def kernel(x, c, ctx, c_ctx, ada_w, ada_b, norm1_g, norm2_g, rg_w_in, rg_conv_w, rg_conv_b, rg_w_a, rg_b_a,
           rg_w_i, rg_b_i, rg_lam, rg_w_out, attn_w_qkv, attn_q_g, attn_k_g, attn_w_o, ffn_w1, ffn_w3, ffn_w2,
           moe_router, moe_w1, moe_w3, moe_w2, final_g):
    b, l, d = x.shape
    n_ctx = ctx.shape[1]
    c_dim = rg_w_out.shape[1]
    n_q = attn_w_o.shape[1] // _HEAD_DIM
    n_kv = (attn_w_qkv.shape[2] // _HEAD_DIM - n_q) // 2

    rows = 16
    c_rows = jnp.zeros((rows, d), F32).at[:b].set(c).at[b].set(c_ctx)
    mod = _ada_call(c_rows, ada_w, ada_b).reshape(ada_w.shape[0], rows, 6, d)
    mod_l = [mod[i, :b] for i in range(2)]
    mod_c = [jnp.broadcast_to(mod[i, b], (b, 6, d)) for i in range(2)]

    w_in = rg_w_in[0].astype(BF16)
    w_out = rg_w_out[0].astype(BF16)
    w_a = rg_w_a[0].astype(BF16)
    w_i = rg_w_i[0].astype(BF16)
    xg_l = _nmm_call(x, norm1_g[0], mod_l[0], w_in, 0, 1, F32)
    xg_c = _nmm_call(ctx, norm1_g[0], mod_c[0], w_in, 0, 1, F32)
    zero_h = jnp.zeros((b, 1, c_dim), F32)

    def lru(xg, h0, y_other, direction, reverse):
        return _lru_call(xg, rg_conv_w[0], rg_conv_b[0], w_a[direction], rg_b_a[0, direction], w_i[direction],
                         rg_b_i[0, direction], rg_lam[0, direction], h0, y_other, reverse=reverse)

    yc_rev, h0_rev = lru(xg_c, zero_h, None, 1, True)
    z_c, h0_fwd = lru(xg_c, zero_h, yc_rev, 0, False)
    yl_rev, _ = lru(xg_l, h0_rev, None, 1, True)
    z_l, _ = lru(xg_l, h0_fwd, yl_rev, 0, False)
    x = _proj_res_call(z_l, w_out, x, mod_l[0], 2)
    ctx = _proj_res_call(z_c, w_out, ctx, mod_c[0], 2)
    f1, f3, f2 = ffn_w1[0].astype(BF16), ffn_w3[0].astype(BF16), ffn_w2[0].astype(BF16)
    x = _ffn_call(x, norm2_g[0], mod_l[0], f1, f3, f2)
    ctx = _ffn_call(ctx, norm2_g[0], mod_c[0], f1, f3, f2)

    w_qkv = attn_w_qkv[0].astype(BF16)
    cos, sin = _rope_tables(l)
    q, k_l, v_l = _qkv_call(x, norm1_g[1], mod_l[1], w_qkv, attn_q_g[0], attn_k_g[0], cos, sin, n_q, n_kv)
    ones = jnp.ones((n_ctx, _HEAD_DIM), F32)
    _, k_c, v_c = _qkv_call(ctx, norm1_g[1], mod_c[1], w_qkv, attn_q_g[0], attn_k_g[0], ones, jnp.zeros_like(ones),
                            n_q, n_kv)
    k_all = jnp.concatenate([k_c, k_l], axis=1)
    v_all = jnp.concatenate([v_c, v_l], axis=1)
    o = _attn_call(q, k_all, v_all, n_kv)
    x = _proj_res_call(o, attn_w_o[0].astype(BF16), x, mod_l[1], 2)
    return _moe_layer(x, norm2_g[1], mod_l[1], moe_router[0], moe_w1[0], moe_w3[0], moe_w2[0], final_g)
```

```python
import functools

import jax
import jax.numpy as jnp
from jax import lax
from jax.experimental import pallas as pl
from jax.experimental.pallas import tpu as pltpu

F32 = jnp.float32
BF16 = jnp.bfloat16

_EPS = 1e-6
_HEAD_DIM = 128
_GRID_W = 64
_ROPE_THETA = 10000.0
_ROPE_NFREQ = _HEAD_DIM // 4
_LRU_C = 8.0
_CONV_W = 4
_LANES = 128
_SUBLANES = 8
_LOG2E = 1.4426950408889634
_MIB = 1024 * 1024


def _params(semantics, vmem_mib):
    return pltpu.CompilerParams(dimension_semantics=semantics, vmem_limit_bytes=vmem_mib * _MIB)


def _tile(n, pref):
    if n <= pref:
        return n
    t = pref
    while n % t:
        t //= 2
    return t


def _dot(a, b):
    return jnp.dot(a, b, preferred_element_type=F32)


def _split_bf16(a):
    hi = a.astype(BF16)
    lo = (a - hi.astype(F32)).astype(BF16)
    return hi, lo


def _dot3(a, b):
    ah, al = _split_bf16(a)
    bh, bl = _split_bf16(b)
    return _dot(ah, bh) + (_dot(al, bh) + _dot(ah, bl))


def _rms_mod(x, g, shift, scale):
    y = x * lax.rsqrt(jnp.mean(x * x, axis=-1, keepdims=True) + _EPS)
    return (y * g) * (1.0 + scale) + shift


def _sigmoid(x):
    return 1.0 / (1.0 + jnp.exp(-x))


def _ada_kernel(c_ref, w_ref, b_ref, o_ref):
    c = c_ref[...]
    s = c * _sigmoid(c)
    o_ref[...] = _dot3(s, w_ref[...]) + b_ref[...]


def _ada_call(c_rows, ada_w, ada_b):
    depth, d, n = ada_w.shape
    rows = c_rows.shape[0]
    tn = _tile(n, 1536)
    return pl.pallas_call(
        _ada_kernel,
        grid=(depth, n // tn),
        in_specs=[
            pl.BlockSpec((rows, d), lambda i, j: (0, 0)),
            pl.BlockSpec((None, d, tn), lambda i, j: (i, 0, j)),
            pl.BlockSpec((None, 1, tn), lambda i, j: (i, 0, j)),
        ],
        out_specs=pl.BlockSpec((None, rows, tn), lambda i, j: (i, 0, j)),
        out_shape=jax.ShapeDtypeStruct((depth, rows, n), F32),
        compiler_params=_params(("parallel", "parallel"), 40),
        name="ada_mod",
    )(c_rows, ada_w, ada_b.reshape(depth, 1, n))


def _nmm_kernel(x_ref, g_ref, mod_ref, w_ref, o_ref, *, sh, sc):
    h = _rms_mod(x_ref[...], g_ref[...], mod_ref[sh:sh + 1, :], mod_ref[sc:sc + 1, :])
    o_ref[...] = _dot(h.astype(BF16), w_ref[...]).astype(o_ref.dtype)


def _nmm_call(x, g, mod, w, sh, sc, out_dtype):
    b, l, d = x.shape
    n = w.shape[1]
    tm = _tile(l, 512)
    return pl.pallas_call(
        functools.partial(_nmm_kernel, sh=sh, sc=sc),
        grid=(b, l // tm),
        in_specs=[
            pl.BlockSpec((None, tm, d), lambda i, t: (i, t, 0)),
            pl.BlockSpec((1, d), lambda i, t: (0, 0)),
            pl.BlockSpec((None, 6, d), lambda i, t: (i, 0, 0)),
            pl.BlockSpec((d, n), lambda i, t: (0, 0)),
        ],
        out_specs=pl.BlockSpec((None, tm, n), lambda i, t: (i, t, 0)),
        out_shape=jax.ShapeDtypeStruct((b, l, n), out_dtype),
        compiler_params=_params(("parallel", "parallel"), 48),
        name="norm_mod_matmul",
    )(x, g.reshape(1, d), mod, w)


def _log1p_pos(e):
    u = 1.0 + e
    den = jnp.where(u == 1.0, 1.0, u - 1.0)
    return jnp.where(u == 1.0, e, jnp.log(u) * (e * (1.0 / den)))


def _softplus(x):
    return jnp.maximum(x, 0.0) + _log1p_pos(jnp.exp(-jnp.abs(x)))


def _expm1_neg(x, u):
    lu = jnp.log(jnp.where(u == 0.0, 1.0, u))
    den = jnp.where(lu == 0.0, 1.0, lu)
    r = (u - 1.0) * (x * (1.0 / den))
    return jnp.where(u == 1.0, x, jnp.where(u == 0.0, -1.0, r))


def _scan_tile(a, b, reverse):
    n = a.shape[0]
    row = lax.broadcasted_iota(jnp.int32, a.shape, 0)
    s = 1
    while s < n:
        if s < _SUBLANES:
            if reverse:
                a_sh = pltpu.roll(a, n - s, axis=0)
                b_sh = pltpu.roll(b, n - s, axis=0)
                m = row < (n - s)
            else:
                a_sh = pltpu.roll(a, s, axis=0)
                b_sh = pltpu.roll(b, s, axis=0)
                m = row >= s
            b = jnp.where(m, a * b_sh + b, b)
            a = jnp.where(m, a * a_sh, a)
        elif reverse:
            a_cur, b_cur = a[:n - s, :], b[:n - s, :]
            b = jnp.concatenate([a_cur * b[s:, :] + b_cur, b[n - s:, :]], axis=0)
            a = jnp.concatenate([a_cur * a[s:, :], a[n - s:, :]], axis=0)
        else:
            a_cur, b_cur = a[s:, :], b[s:, :]
            b = jnp.concatenate([b[:s, :], a_cur * b[:n - s, :] + b_cur], axis=0)
            a = jnp.concatenate([a[:s, :], a_cur * a[:n - s, :]], axis=0)
        s *= 2
    return a, b


def _lru_kernel(*refs, reverse, tl, nt, fuse):
    if fuse:
        (xm_ref, xp_ref, xn_ref, cw_ref, cb_ref, wa_ref, ba_ref, wi_ref, bi_ref, lam_ref, h0_ref,
         yo_ref, gb_ref, out_ref, hl_ref, carry_ref) = refs
    else:
        (xm_ref, xp_ref, xn_ref, cw_ref, cb_ref, wa_ref, ba_ref, wi_ref, bi_ref, lam_ref, h0_ref,
         out_ref, hl_ref, carry_ref) = refs
    t = pl.program_id(2)
    tt = (nt - 1 - t) if reverse else t

    @pl.when(t == 0)
    def _():
        carry_ref[...] = h0_ref[...]

    xm = xm_ref[...]
    prev = jnp.where(tt > 0, xp_ref[...], 0.0)
    nxt = jnp.where(tt < nt - 1, xn_ref[...], 0.0)
    row8 = lax.broadcasted_iota(jnp.int32, prev.shape, 0)

    def tap(offset):
        if offset < 0:
            s = -offset
            y = pltpu.roll(xm, s, axis=0)
            head = jnp.where(row8 < s, pltpu.roll(prev, s, axis=0), y[:_SUBLANES, :])
            return jnp.concatenate([head, y[_SUBLANES:, :]], axis=0)
        y = pltpu.roll(xm, tl - offset, axis=0)
        tail = jnp.where(row8 >= _SUBLANES - offset, pltpu.roll(nxt, _SUBLANES - offset, axis=0),
                         y[tl - _SUBLANES:, :])
        return jnp.concatenate([y[:tl - _SUBLANES, :], tail], axis=0)

    cw = cw_ref[...]
    left = _CONV_W // 2
    xc = cb_ref[...]
    for k in range(_CONV_W):
        xk = xm if k == left else tap(k - left)
        xc = xc + xk * cw[k:k + 1, :]

    xcb = xc.astype(BF16)
    r = _sigmoid(_dot(xcb, wa_ref[...]) + ba_ref[...])
    i = _sigmoid(_dot(xcb, wi_ref[...]) + bi_ref[...])
    log_a = (-_LRU_C * r) * _softplus(-lam_ref[...])
    a = jnp.exp(log_a)
    mult = jnp.sqrt(-_expm1_neg(2.0 * log_a, a * a))
    b = mult * (i * xc)

    a_cum, b_cum = _scan_tile(a, b, reverse)
    hs = a_cum * carry_ref[...] + b_cum
    new_carry = hs[0:1, :] if reverse else hs[tl - 1:tl, :]
    carry_ref[...] = new_carry
    hl_ref[...] = new_carry
    if fuse:
        gate = jax.nn.gelu(gb_ref[...], approximate=True)
        out_ref[...] = ((hs + yo_ref[...]) * gate).astype(out_ref.dtype)
    else:
        out_ref[...] = hs


def _lru_call(xg, conv_w, conv_b, w_a, b_a, w_i, b_i, lam, h0, y_other, *, reverse):
    b, l, c2 = xg.shape
    c = c2 // 2
    nb = c // _LANES
    tl = _tile(l, 512)
    nt = l // tl
    hb = tl // _SUBLANES
    nh = l // _SUBLANES
    fuse = y_other is not None

    def tmap(t):
        return (nt - 1 - t) if reverse else t

    vec = lambda: pl.BlockSpec((1, _LANES), lambda i, j, t: (0, j))
    in_specs = [
        pl.BlockSpec((None, tl, _LANES), lambda i, j, t: (i, tmap(t), j)),
        pl.BlockSpec((None, _SUBLANES, _LANES), lambda i, j, t: (i, jnp.maximum(tmap(t) * hb - 1, 0), j)),
        pl.BlockSpec((None, _SUBLANES, _LANES), lambda i, j, t: (i, jnp.minimum((tmap(t) + 1) * hb, nh - 1), j)),
        pl.BlockSpec((_CONV_W, _LANES), lambda i, j, t: (0, j)),
        vec(),
        pl.BlockSpec((None, _LANES, _LANES), lambda i, j, t: (j, 0, 0)),
        vec(),
        pl.BlockSpec((None, _LANES, _LANES), lambda i, j, t: (j, 0, 0)),
        vec(),
        vec(),
        pl.BlockSpec((None, 1, _LANES), lambda i, j, t: (i, 0, j)),
    ]
    args = [xg, xg, xg, conv_w, conv_b.reshape(1, c), w_a, b_a.reshape(1, c), w_i, b_i.reshape(1, c),
            lam.reshape(1, c), h0]
    if fuse:
        in_specs += [
            pl.BlockSpec((None, tl, _LANES), lambda i, j, t: (i, tmap(t), j)),
            pl.BlockSpec((None, tl, _LANES), lambda i, j, t: (i, tmap(t), nb + j)),
        ]
        args += [y_other, xg]
    out_dtype = BF16 if fuse else F32
    return pl.pallas_call(
        functools.partial(_lru_kernel, reverse=reverse, tl=tl, nt=nt, fuse=fuse),
        grid=(b, nb, nt),
        in_specs=in_specs,
        out_specs=[
            pl.BlockSpec((None, tl, _LANES), lambda i, j, t: (i, tmap(t), j)),
            pl.BlockSpec((None, 1, _LANES), lambda i, j, t: (i, 0, j)),
        ],
        out_shape=[jax.ShapeDtypeStruct((b, l, c), out_dtype), jax.ShapeDtypeStruct((b, 1, c), F32)],
        scratch_shapes=[pltpu.VMEM((1, _LANES), F32)],
        compiler_params=_params(("parallel", "parallel", "arbitrary"), 32),
        name="lru_rev" if reverse else "lru_fwd",
    )(*args)


def _proj_res_kernel(z_ref, w_ref, x_ref, mod_ref, o_ref, *, gi):
    o_ref[...] = x_ref[...] + mod_ref[gi:gi + 1, :] * _dot(z_ref[...], w_ref[...])


def _proj_res_call(z, w, x, mod, gi):
    b, l, d = x.shape
    k = z.shape[2]
    tm = _tile(l, 1024)
    return pl.pallas_call(
        functools.partial(_proj_res_kernel, gi=gi),
        grid=(b, l // tm),
        in_specs=[
            pl.BlockSpec((None, tm, k), lambda i, t: (i, t, 0)),
            pl.BlockSpec((k, d), lambda i, t: (0, 0)),
            pl.BlockSpec((None, tm, d), lambda i, t: (i, t, 0)),
            pl.BlockSpec((None, 6, d), lambda i, t: (i, 0, 0)),
        ],
        out_specs=pl.BlockSpec((None, tm, d), lambda i, t: (i, t, 0)),
        out_shape=jax.ShapeDtypeStruct((b, l, d), F32),
        compiler_params=_params(("parallel", "parallel"), 48),
        name="proj_residual",
    )(z, w, x, mod)


def _ffn_kernel(x_ref, g_ref, mod_ref, w1_ref, w3_ref, w2_ref, o_ref):
    x = x_ref[...]
    h = _rms_mod(x, g_ref[...], mod_ref[3:4, :], mod_ref[4:5, :]).astype(BF16)
    h1 = _dot(h, w1_ref[...])
    h3 = _dot(h, w3_ref[...])
    hid = (h1 * _sigmoid(h1)) * h3
    o_ref[...] = x + mod_ref[5:6, :] * _dot(hid.astype(BF16), w2_ref[...])


def _ffn_call(x, g, mod, w1, w3, w2):
    b, l, d = x.shape
    ff = w1.shape[1]
    tm = _tile(l, 512)
    resident = pl.Buffered(1)
    return pl.pallas_call(
        _ffn_kernel,
        grid=(b, l // tm),
        in_specs=[
            pl.BlockSpec((None, tm, d), lambda i, t: (i, t, 0)),
            pl.BlockSpec((1, d), lambda i, t: (0, 0)),
            pl.BlockSpec((None, 6, d), lambda i, t: (i, 0, 0)),
            pl.BlockSpec((d, ff), lambda i, t: (0, 0), pipeline_mode=resident),
            pl.BlockSpec((d, ff), lambda i, t: (0, 0), pipeline_mode=resident),
            pl.BlockSpec((ff, d), lambda i, t: (0, 0), pipeline_mode=resident),
        ],
        out_specs=pl.BlockSpec((None, tm, d), lambda i, t: (i, t, 0)),
        out_shape=jax.ShapeDtypeStruct((b, l, d), F32),
        compiler_params=_params(("parallel", "parallel"), 56),
        name="dense_swiglu",
    )(x, g.reshape(1, d), mod, w1, w3, w2)


def _rope_tables(l):
    pos = jnp.arange(l)
    row = (pos // _GRID_W).astype(F32)
    col = (pos % _GRID_W).astype(F32)
    freqs = _ROPE_THETA ** (-jnp.arange(_ROPE_NFREQ, dtype=F32) / _ROPE_NFREQ)
    ar = row[:, None] * freqs
    ac = col[:, None] * freqs
    cos = jnp.concatenate([jnp.cos(ar), jnp.cos(ar), jnp.cos(ac), jnp.cos(ac)], axis=-1)
    sin = jnp.concatenate([-jnp.sin(ar), jnp.sin(ar), -jnp.sin(ac), jnp.sin(ac)], axis=-1)
    return cos, sin


def _qkv_kernel(x_ref, g_ref, mod_ref, w_ref, qg_ref, kg_ref, cos_ref, sin_ref, q_ref, k_ref, v_ref,
                qn_ref, kn_ref, *, n_q, n_kv, q_scale):
    h = _rms_mod(x_ref[...], g_ref[...], mod_ref[0:1, :], mod_ref[1:2, :])
    qkv = _dot(h.astype(BF16), w_ref[...])
    cos = cos_ref[...]
    sin = sin_ref[...]
    lane = lax.broadcasted_iota(jnp.int32, cos.shape, 1)
    first = (lane % (2 * _ROPE_NFREQ)) < _ROPE_NFREQ

    def norm_rope(v, gain):
        y = v * lax.rsqrt(jnp.mean(v * v, axis=-1, keepdims=True) + _EPS) * gain
        partner = jnp.where(first, pltpu.roll(y, _HEAD_DIM - _ROPE_NFREQ, axis=1),
                            pltpu.roll(y, _ROPE_NFREQ, axis=1))
        return y * cos + partner * sin

    def max_sq_norm(val, best):
        return jnp.maximum(best, jnp.max(jnp.sum(val * val, axis=-1, keepdims=True), axis=0, keepdims=True))

    q_best = jnp.zeros((1, 1), F32)
    for hh in range(n_q):
        sl = slice(hh * _HEAD_DIM, (hh + 1) * _HEAD_DIM)
        val = norm_rope(qkv[:, sl], qg_ref[...]) * q_scale
        q_best = max_sq_norm(val, q_best)
        q_ref[:, sl] = val.astype(q_ref.dtype)
    k_best = jnp.zeros((1, 1), F32)
    for hh in range(n_kv):
        src = slice((n_q + hh) * _HEAD_DIM, (n_q + hh + 1) * _HEAD_DIM)
        dst = slice(hh * _HEAD_DIM, (hh + 1) * _HEAD_DIM)
        val = norm_rope(qkv[:, src], kg_ref[...])
        k_best = max_sq_norm(val, k_best)
        k_ref[:, dst] = val.astype(k_ref.dtype)
    v0 = (n_q + n_kv) * _HEAD_DIM
    v_ref[...] = qkv[:, v0:v0 + n_kv * _HEAD_DIM].astype(v_ref.dtype)
    qn_ref[...] = jnp.broadcast_to(q_best, qn_ref.shape)
    kn_ref[...] = jnp.broadcast_to(k_best, kn_ref.shape)


def _qkv_call(x, g, mod, w, q_g, k_g, cos, sin, n_q, n_kv):
    b, l, d = x.shape
    n = w.shape[1]
    tm = _tile(l, 512)
    nq = n_q * _HEAD_DIM
    nkv = n_kv * _HEAD_DIM
    q_scale = (_HEAD_DIM ** -0.5) * _LOG2E
    return pl.pallas_call(
        functools.partial(_qkv_kernel, n_q=n_q, n_kv=n_kv, q_scale=q_scale),
        grid=(b, l // tm),
        in_specs=[
            pl.BlockSpec((None, tm, d), lambda i, t: (i, t, 0)),
            pl.BlockSpec((1, d), lambda i, t: (0, 0)),
            pl.BlockSpec((None, 6, d), lambda i, t: (i, 0, 0)),
            pl.BlockSpec((d, n), lambda i, t: (0, 0)),
            pl.BlockSpec((1, _HEAD_DIM), lambda i, t: (0, 0)),
            pl.BlockSpec((1, _HEAD_DIM), lambda i, t: (0, 0)),
            pl.BlockSpec((tm, _HEAD_DIM), lambda i, t: (t, 0)),
            pl.BlockSpec((tm, _HEAD_DIM), lambda i, t: (t, 0)),
        ],
        out_specs=[
            pl.BlockSpec((None, tm, nq), lambda i, t: (i, t, 0)),
            pl.BlockSpec((None, tm, nkv), lambda i, t: (i, t, 0)),
            pl.BlockSpec((None, tm, nkv), lambda i, t: (i, t, 0)),
            pl.BlockSpec((None, None, _SUBLANES, _LANES), lambda i, t: (i, t, 0, 0)),
            pl.BlockSpec((None, None, _SUBLANES, _LANES), lambda i, t: (i, t, 0, 0)),
        ],
        out_shape=[
            jax.ShapeDtypeStruct((b, l, nq), BF16),
            jax.ShapeDtypeStruct((b, l, nkv), BF16),
            jax.ShapeDtypeStruct((b, l, nkv), BF16),
            jax.ShapeDtypeStruct((b, l // tm, _SUBLANES, _LANES), F32),
            jax.ShapeDtypeStruct((b, l // tm, _SUBLANES, _LANES), F32),
        ],
        compiler_params=_params(("parallel", "parallel"), 48),
        name="qkv_norm_rope",
    )(x, g.reshape(1, d), mod, w, q_g.reshape(1, _HEAD_DIM), k_g.reshape(1, _HEAD_DIM), cos, sin)


def _lane_tile(x, reps):
    return x if reps == 1 else jnp.concatenate([x] * reps, axis=1)


def _attn_kernel(q_ref, k_ref, v_ref, o_ref, m_ref, l_ref, acc_ref, *, tq, tk, groups, n_chunks):
    q = jnp.concatenate([q_ref[:, g * _HEAD_DIM:(g + 1) * _HEAD_DIM] for g in range(groups)], axis=0)
    m_ref[...] = jnp.full_like(m_ref, -jnp.inf)
    l_ref[...] = jnp.zeros_like(l_ref)
    acc_ref[...] = jnp.zeros_like(acc_ref)

    def body(c, carry):
        start = pl.multiple_of(c * tk, tk)
        k = k_ref[pl.ds(start, tk), :]
        v = v_ref[pl.ds(start, tk), :]
        s = lax.dot_general(q, k, (((1,), (1,)), ((), ())), preferred_element_type=F32)
        m_prev = m_ref[...]
        m_next = jnp.maximum(m_prev, jnp.max(s, axis=1, keepdims=True))
        alpha = jnp.exp2(m_prev - m_next)
        p = jnp.exp2(s - _lane_tile(m_next, tk // _LANES))
        l_ref[...] = alpha * l_ref[...] + jnp.sum(p, axis=1, keepdims=True)
        acc_ref[...] = alpha * acc_ref[...] + _dot(p.astype(BF16), v)
        m_ref[...] = m_next
        return carry

    lax.fori_loop(0, n_chunks, body, 0)
    o = acc_ref[...] * (1.0 / l_ref[...])
    for g in range(groups):
        o_ref[:, g * _HEAD_DIM:(g + 1) * _HEAD_DIM] = o[g * tq:(g + 1) * tq, :].astype(o_ref.dtype)


def _attn_bounded_kernel(q_ref, kt_ref, v_ref, o_ref, *, tq, groups, n_chunks):
    q = jnp.concatenate([q_ref[:, g * _HEAD_DIM:(g + 1) * _HEAD_DIM] for g in range(groups)], axis=0)
    acc = None
    for c in range(n_chunks):
        p = jnp.exp2(_dot(q, kt_ref[c])).astype(BF16)
        pv = _dot(p, v_ref[c])
        acc = pv if acc is None else acc + pv
    o = acc[:, :_HEAD_DIM] * (1.0 / acc[:, _HEAD_DIM:_HEAD_DIM + 1])
    for g in range(groups):
        o_ref[:, g * _HEAD_DIM:(g + 1) * _HEAD_DIM] = o[g * tq:(g + 1) * tq, :].astype(o_ref.dtype)


_SCORE_BOUND = 60.0
_KV_CHUNKS = (256, 128)


def _attn_bounded_call(q, k, v, n_kv):
    b, l, nq = q.shape
    s = k.shape[1]
    groups = nq // (n_kv * _HEAD_DIM)
    gw = groups * _HEAD_DIM
    tq = _tile(l, 256)
    tk = next(c for c in _KV_CHUNKS if s % c == 0)
    n_chunks = s // tk
    vw = 2 * _HEAD_DIM
    kt = k.reshape(b, n_chunks, tk, n_kv, _HEAD_DIM).transpose(0, 3, 1, 4, 2)
    v4 = v.reshape(b, n_chunks, tk, n_kv, _HEAD_DIM)
    ones = jnp.ones(v4.shape[:-1] + (1,), BF16)
    zeros = jnp.zeros(v4.shape[:-1] + (vw - _HEAD_DIM - 1,), BF16)
    vx = jnp.concatenate([v4, ones, zeros], axis=-1).transpose(0, 3, 1, 2, 4)
    return pl.pallas_call(
        functools.partial(_attn_bounded_kernel, tq=tq, groups=groups, n_chunks=n_chunks),
        grid=(b, n_kv, l // tq),
        in_specs=[
            pl.BlockSpec((None, tq, gw), lambda i, j, t: (i, t, j)),
            pl.BlockSpec((None, None, n_chunks, _HEAD_DIM, tk), lambda i, j, t: (i, j, 0, 0, 0)),
            pl.BlockSpec((None, None, n_chunks, tk, vw), lambda i, j, t: (i, j, 0, 0, 0)),
        ],
        out_specs=pl.BlockSpec((None, tq, gw), lambda i, j, t: (i, t, j)),
        out_shape=jax.ShapeDtypeStruct((b, l, nq), BF16),
        compiler_params=_params(("parallel", "parallel", "arbitrary"), 56),
        name="gqa_attention_bounded",
    )(q, kt, vx)


def _attn_call(q, k, v, n_kv):
    b, l, nq = q.shape
    s = k.shape[1]
    groups = nq // (n_kv * _HEAD_DIM)
    gw = groups * _HEAD_DIM
    tq = _tile(l, 256)
    tk = 768 if s % 768 == 0 else _tile(s, 512)
    n_chunks = s // tk
    m_rows = groups * tq
    return pl.pallas_call(
        functools.partial(_attn_kernel, tq=tq, tk=tk, groups=groups, n_chunks=n_chunks),
        grid=(b, n_kv, l // tq),
        in_specs=[
            pl.BlockSpec((None, tq, gw), lambda i, j, t: (i, t, j)),
            pl.BlockSpec((None, s, _HEAD_DIM), lambda i, j, t: (i, 0, j)),
            pl.BlockSpec((None, s, _HEAD_DIM), lambda i, j, t: (i, 0, j)),
        ],
        out_specs=pl.BlockSpec((None, tq, gw), lambda i, j, t: (i, t, j)),
        out_shape=jax.ShapeDtypeStruct((b, l, nq), BF16),
        scratch_shapes=[
            pltpu.VMEM((m_rows, _LANES), F32),
            pltpu.VMEM((m_rows, _LANES), F32),
            pltpu.VMEM((m_rows, _HEAD_DIM), F32),
        ],
        compiler_params=_params(("parallel", "parallel", "arbitrary"), 48),
        name="gqa_flash_attention",
    )(q, k, v)


_INFO_E0, _INFO_E1, _INFO_R0, _INFO_R1, _INFO_G0, _INFO_G1 = range(6)


def _route_kernel(x_ref, g_ref, mod_ref, rw_ref, h_ref, info_ref, cnt_ref, carry_ref, *, n_exp, tm):
    step = pl.program_id(0)

    @pl.when(step == 0)
    def _():
        carry_ref[...] = jnp.zeros_like(carry_ref)

    h = _rms_mod(x_ref[...], g_ref[...], mod_ref[3:4, :], mod_ref[4:5, :])
    h_ref[...] = h
    logits = _dot3(h, rw_ref[...])
    lane = lax.broadcasted_iota(jnp.int32, logits.shape, 1)
    neg = jnp.float32(-jnp.inf)
    lg = jnp.where(lane < n_exp, logits, neg)
    m0 = jnp.max(lg, axis=1, keepdims=True)
    i0 = jnp.min(jnp.where(lg == m0, lane, _LANES), axis=1, keepdims=True)
    oh0 = lane == i0
    lg1 = jnp.where(oh0, neg, lg)
    m1 = jnp.max(lg1, axis=1, keepdims=True)
    i1 = jnp.min(jnp.where(lg1 == m1, lane, _LANES), axis=1, keepdims=True)
    oh1 = lane == i1
    e = jnp.exp(m1 - m0)
    g0 = 1.0 / (1.0 + e)
    g1 = e * g0

    chosen = jnp.where(oh0 | oh1, 1.0, 0.0)
    r = lax.broadcasted_iota(jnp.int32, (tm, tm), 0)
    cidx = lax.broadcasted_iota(jnp.int32, (tm, tm), 1)
    tri = jnp.where(r > cidx, 1.0, 0.0).astype(BF16)
    before = _dot(tri, chosen.astype(BF16)) + carry_ref[...]
    r0 = jnp.sum(jnp.where(oh0, before, 0.0), axis=1, keepdims=True)
    r1 = jnp.sum(jnp.where(oh1, before, 0.0), axis=1, keepdims=True)
    total = carry_ref[...] + jnp.sum(chosen, axis=0, keepdims=True)
    carry_ref[...] = total
    cnt_ref[...] = total

    info = jnp.zeros(logits.shape, F32)
    for idx, val in ((_INFO_E0, i0.astype(F32)), (_INFO_E1, i1.astype(F32)), (_INFO_R0, r0), (_INFO_R1, r1),
                     (_INFO_G0, g0), (_INFO_G1, g1)):
        info = jnp.where(lane == idx, val, info)
    info_ref[...] = info


def _route_call(x, g, mod, router):
    b, l, d = x.shape
    n_exp = router.shape[1]
    t = b * l
    tm = _tile(l, 512)
    per_b = l // tm
    rw = jnp.zeros((d, _LANES), F32).at[:, :n_exp].set(router)
    return pl.pallas_call(
        functools.partial(_route_kernel, n_exp=n_exp, tm=tm),
        grid=(t // tm,),
        in_specs=[
            pl.BlockSpec((tm, d), lambda i: (i, 0)),
            pl.BlockSpec((1, d), lambda i: (0, 0)),
            pl.BlockSpec((None, 6, d), lambda i: (i // per_b, 0, 0)),
            pl.BlockSpec((d, _LANES), lambda i: (0, 0)),
        ],
        out_specs=[
            pl.BlockSpec((tm, d), lambda i: (i, 0)),
            pl.BlockSpec((tm, _LANES), lambda i: (i, 0)),
            pl.BlockSpec((1, _LANES), lambda i: (0, 0)),
        ],
        out_shape=[
            jax.ShapeDtypeStruct((t, d), F32),
            jax.ShapeDtypeStruct((t, _LANES), F32),
            jax.ShapeDtypeStruct((1, _LANES), F32),
        ],
        scratch_shapes=[pltpu.VMEM((1, _LANES), F32)],
        compiler_params=_params(("arbitrary",), 40),
        name="moe_route",
    )(x.reshape(t, d), g.reshape(1, d), mod, rw)


def _pos_tiles(pos0, pos1, tt):
    n = pos0.shape[0] // tt
    return jnp.stack([pos0.reshape(n, tt), pos1.reshape(n, tt)], axis=1).reshape(n * 2 * tt)


def _row_copy(src, src_row, dst, dst_row, sem):
    return pltpu.make_async_copy(src.at[pl.ds(src_row, 1)], dst.at[pl.ds(dst_row, 1)], sem)


_ROW_DMA_UNROLL = 8


def _dispatch_kernel(pos_ref, h_ref, xs_in_ref, xs_ref, pos_smem, sem_s, sem_d, *, tt):
    del xs_in_ref
    cp = pltpu.make_async_copy(pos_ref, pos_smem, sem_s)
    cp.start()
    cp.wait()

    def body(r, carry):
        _row_copy(h_ref, r, xs_ref, pos_smem[r], sem_d).start()
        _row_copy(h_ref, r, xs_ref, pos_smem[tt + r], sem_d).start()
        return carry

    lax.fori_loop(0, tt, body, 0, unroll=_ROW_DMA_UNROLL)
    for _ in range(2 * tt):
        _row_copy(h_ref, 0, xs_ref, 0, sem_d).wait()


def _dispatch_call(h, pos_t, p_rows, tt):
    t, d = h.shape
    xs0 = jnp.zeros((p_rows, d), F32)
    return pl.pallas_call(
        functools.partial(_dispatch_kernel, tt=tt),
        grid=(t // tt,),
        in_specs=[
            pl.BlockSpec((2 * tt,), lambda i: (i,)),
            pl.BlockSpec((tt, d), lambda i: (i, 0)),
            pl.BlockSpec(memory_space=pl.ANY),
        ],
        out_specs=pl.BlockSpec(memory_space=pl.ANY),
        out_shape=jax.ShapeDtypeStruct((p_rows, d), F32),
        scratch_shapes=[
            pltpu.SMEM((2 * tt,), jnp.int32),
            pltpu.SemaphoreType.DMA(()),
            pltpu.SemaphoreType.DMA(()),
        ],
        input_output_aliases={2: 0},
        compiler_params=_params(("arbitrary",), 32),
        name="moe_dispatch",
    )(pos_t, h, xs0)


def _moe_kernel(be_ref, bs_ref, bv_ref, x_ref, w1_ref, w3_ref, w2_ref, o_ref, xb_ref, acc_ref, *, nf):
    del be_ref, bs_ref
    i = pl.program_id(0)
    f = pl.program_id(1)

    @pl.when(bv_ref[i] > 0)
    def _():
        @pl.when(f == 0)
        def _():
            xb_ref[...] = x_ref[...].astype(BF16)
            acc_ref[...] = jnp.zeros_like(acc_ref)

        xb = xb_ref[...]
        h1 = _dot(xb, w1_ref[...])
        h3 = _dot(xb, w3_ref[...])
        hid = (h1 * _sigmoid(h1)) * h3
        acc_ref[...] += _dot(hid.astype(BF16), w2_ref[...])

        @pl.when(f == nf - 1)
        def _():
            o_ref[...] = acc_ref[...]


def _moe_call(blk_e, blk_src, blk_valid, xs, w1, w3, w2, tmm):
    p_rows, d = xs.shape
    ff = w1.shape[2]
    tf = 1792 if ff % 1792 == 0 else _tile(ff, 512)
    nf = ff // tf
    nblk = p_rows // tmm

    def fsel(i, f, bv):
        return jnp.where(bv[i] > 0, f, nf - 1)

    grid_spec = pltpu.PrefetchScalarGridSpec(
        num_scalar_prefetch=3,
        grid=(nblk, nf),
        in_specs=[
            pl.BlockSpec((tmm, d), lambda i, f, be, bs, bv: (bs[i], 0)),
            pl.BlockSpec((None, d, tf), lambda i, f, be, bs, bv: (be[i], 0, fsel(i, f, bv))),
            pl.BlockSpec((None, d, tf), lambda i, f, be, bs, bv: (be[i], 0, fsel(i, f, bv))),
            pl.BlockSpec((None, tf, d), lambda i, f, be, bs, bv: (be[i], fsel(i, f, bv), 0)),
        ],
        out_specs=pl.BlockSpec((tmm, d), lambda i, f, be, bs, bv: (bs[i], 0)),
        scratch_shapes=[pltpu.VMEM((tmm, d), BF16), pltpu.VMEM((tmm, d), F32)],
    )
    return pl.pallas_call(
        functools.partial(_moe_kernel, nf=nf),
        grid_spec=grid_spec,
        out_shape=jax.ShapeDtypeStruct((p_rows, d), F32),
        compiler_params=_params(("arbitrary", "arbitrary"), 58),
        name="moe_experts",
    )(blk_e, blk_src, blk_valid, xs, w1, w3, w2)


def _combine_kernel(pos_ref, x_ref, mod_ref, info_ref, fg_ref, y_ref, o_ref, pos_smem, ybuf, sem_s, sem_d,
                    *, tt):
    cp = pltpu.make_async_copy(pos_ref, pos_smem, sem_s)
    cp.start()
    cp.wait()

    def body(r, carry):
        _row_copy(y_ref, pos_smem[r], ybuf.at[0], r, sem_d).start()
        _row_copy(y_ref, pos_smem[tt + r], ybuf.at[1], r, sem_d).start()
        return carry

    lax.fori_loop(0, tt, body, 0, unroll=_ROW_DMA_UNROLL)
    for _ in range(2 * tt):
        _row_copy(y_ref, 0, ybuf.at[0], 0, sem_d).wait()

    info = info_ref[...]
    g0 = info[:, _INFO_G0:_INFO_G0 + 1]
    g1 = info[:, _INFO_G1:_INFO_G1 + 1]
    ffn = g0 * ybuf[0] + g1 * ybuf[1]
    xo = x_ref[...] + mod_ref[5:6, :] * ffn
    o_ref[...] = xo * lax.rsqrt(jnp.mean(xo * xo, axis=-1, keepdims=True) + _EPS) * fg_ref[...]


def _combine_call(pos_t, x, mod, info, final_g, y, tt):
    b, l, d = x.shape
    t = b * l
    per_b = l // tt
    return pl.pallas_call(
        functools.partial(_combine_kernel, tt=tt),
        grid=(t // tt,),
        in_specs=[
            pl.BlockSpec((2 * tt,), lambda i: (i,)),
            pl.BlockSpec((tt, d), lambda i: (i, 0)),
            pl.BlockSpec((None, 6, d), lambda i: (i // per_b, 0, 0)),
            pl.BlockSpec((tt, _LANES), lambda i: (i, 0)),
            pl.BlockSpec((1, d), lambda i: (0, 0)),
            pl.BlockSpec(memory_space=pl.ANY),
        ],
        out_specs=pl.BlockSpec((tt, d), lambda i: (i, 0)),
        out_shape=jax.ShapeDtypeStruct((t, d), F32),
        scratch_shapes=[
            pltpu.SMEM((2 * tt,), jnp.int32),
            pltpu.VMEM((2, tt, d), F32),
            pltpu.SemaphoreType.DMA(()),
            pltpu.SemaphoreType.DMA(()),
        ],
        compiler_params=_params(("arbitrary",), 40),
        name="moe_combine_norm",
    )(pos_t, x.reshape(t, d), mod, info, final_g.reshape(1, d), y).reshape(b, l, d)


def _moe_layer(x, g, mod, router, w1, w3, w2, final_g):
    b, l, d = x.shape
    t = b * l
    n_exp = router.shape[1]
    tmm = 512 if t >= 4096 else 128
    tt = _tile(l, 512)
    h, info, counts = _route_call(x, g, mod, router)

    counts = counts[0, :n_exp].astype(jnp.int32)
    padded = ((counts + tmm - 1) // tmm) * tmm
    pend = jnp.cumsum(padded)
    gstart = pend - padded
    e0 = info[:, _INFO_E0].astype(jnp.int32)
    e1 = info[:, _INFO_E1].astype(jnp.int32)
    pos0 = gstart[e0] + info[:, _INFO_R0].astype(jnp.int32)
    pos1 = gstart[e1] + info[:, _INFO_R1].astype(jnp.int32)
    pos_t = _pos_tiles(pos0, pos1, tt)

    nblk = (t * 2) // tmm + n_exp
    p_rows = nblk * tmm
    bstart = jnp.arange(nblk, dtype=jnp.int32) * tmm
    used = bstart < pend[-1]
    last = pend[-1] // tmm - 1
    bidx = jnp.where(used, jnp.arange(nblk, dtype=jnp.int32), last)
    blk_e = jnp.minimum(jnp.searchsorted(pend, bidx * tmm, side="right"), n_exp - 1).astype(jnp.int32)
    blk_valid = jnp.where(used, jnp.clip(counts[blk_e] - (bidx * tmm - gstart[blk_e]), 0, tmm), 0).astype(jnp.int32)

    xs = _dispatch_call(h, pos_t, p_rows, tt)
    y = _moe_call(blk_e, bidx.astype(jnp.int32), blk_valid, xs, w1.astype(BF16), w3.astype(BF16), w2.astype(BF16),
                  tmm)
    return _combine_call(pos_t, x, mod, info, final_g, y, tt)


def kernel(x, c, ctx, c_ctx, ada_w, ada_b, norm1_g, norm2_g, rg_w_in, rg_conv_w, rg_conv_b, rg_w_a, rg_b_a,
           rg_w_i, rg_b_i, rg_lam, rg_w_out, attn_w_qkv, attn_q_g, attn_k_g, attn_w_o, ffn_w1, ffn_w3, ffn_w2,
           moe_router, moe_w1, moe_w3, moe_w2, final_g):
    b, l, d = x.shape
    n_ctx = ctx.shape[1]
    c_dim = rg_w_out.shape[1]
    n_q = attn_w_o.shape[1] // _HEAD_DIM
    n_kv = (attn_w_qkv.shape[2] // _HEAD_DIM - n_q) // 2

    rows = 16
    c_rows = jnp.zeros((rows, d), F32).at[:b].set(c).at[b].set(c_ctx)
    mod = _ada_call(c_rows, ada_w, ada_b).reshape(ada_w.shape[0], rows, 6, d)
    mod_l = [mod[i, :b] for i in range(2)]
    mod_c = [jnp.broadcast_to(mod[i, b], (b, 6, d)) for i in range(2)]

    w_in = rg_w_in[0].astype(BF16)
    w_out = rg_w_out[0].astype(BF16)
    w_a = rg_w_a[0].astype(BF16)
    w_i = rg_w_i[0].astype(BF16)
    xg_l = _nmm_call(x, norm1_g[0], mod_l[0], w_in, 0, 1, F32)
    xg_c = _nmm_call(ctx, norm1_g[0], mod_c[0], w_in, 0, 1, F32)
    zero_h = jnp.zeros((b, 1, c_dim), F32)

    def lru(xg, h0, y_other, direction, reverse):
        return _lru_call(xg, rg_conv_w[0], rg_conv_b[0], w_a[direction], rg_b_a[0, direction], w_i[direction],
                         rg_b_i[0, direction], rg_lam[0, direction], h0, y_other, reverse=reverse)

    yc_rev, h0_rev = lru(xg_c, zero_h, None, 1, True)
    z_c, h0_fwd = lru(xg_c, zero_h, yc_rev, 0, False)
    yl_rev, _ = lru(xg_l, h0_rev, None, 1, True)
    z_l, _ = lru(xg_l, h0_fwd, yl_rev, 0, False)
    x = _proj_res_call(z_l, w_out, x, mod_l[0], 2)
    ctx = _proj_res_call(z_c, w_out, ctx, mod_c[0], 2)
    f1, f3, f2 = ffn_w1[0].astype(BF16), ffn_w3[0].astype(BF16), ffn_w2[0].astype(BF16)
    x = _ffn_call(x, norm2_g[0], mod_l[0], f1, f3, f2)
    ctx = _ffn_call(ctx, norm2_g[0], mod_c[0], f1, f3, f2)

    w_qkv = attn_w_qkv[0].astype(BF16)
    cos, sin = _rope_tables(l)
    q, k_l, v_l, qn_l, kn_l = _qkv_call(x, norm1_g[1], mod_l[1], w_qkv, attn_q_g[0], attn_k_g[0], cos, sin,
                                        n_q, n_kv)
    ones = jnp.ones((n_ctx, _HEAD_DIM), F32)
    _, k_c, v_c, _, kn_c = _qkv_call(ctx, norm1_g[1], mod_c[1], w_qkv, attn_q_g[0], attn_k_g[0], ones,
                                     jnp.zeros_like(ones), n_q, n_kv)
    k_all = jnp.concatenate([k_c, k_l], axis=1)
    v_all = jnp.concatenate([v_c, v_l], axis=1)
    score_bound = jnp.sqrt(jnp.max(qn_l) * jnp.maximum(jnp.max(kn_l), jnp.max(kn_c)))
    o = lax.cond(score_bound <= _SCORE_BOUND,
                 lambda: _attn_bounded_call(q, k_all, v_all, n_kv),
                 lambda: _attn_call(q, k_all, v_all, n_kv))
    x = _proj_res_call(o, attn_w_o[0].astype(BF16), x, mod_l[1], 2)
    return _moe_layer(x, norm2_g[1], mod_l[1], moe_router[0], moe_w1[0], moe_w3[0], moe_w2[0], final_g)
```

```python
import functools

import jax
import jax.numpy as jnp
from jax import lax
from jax.experimental import pallas as pl
from jax.experimental.pallas import tpu as pltpu

F32 = jnp.float32
BF16 = jnp.bfloat16

_EPS = 1e-6
_HEAD_DIM = 128
_GRID_W = 64
_ROPE_THETA = 10000.0
_ROPE_NFREQ = _HEAD_DIM // 4
_LRU_C = 8.0
_CONV_W = 4
_LANES = 128
_SUBLANES = 8
_LOG2E = 1.4426950408889634
_MIB = 1024 * 1024


def _params(semantics, vmem_mib):
    return pltpu.CompilerParams(dimension_semantics=semantics, vmem_limit_bytes=vmem_mib * _MIB)


def _tile(n, pref):
    if n <= pref:
        return n
    t = pref
    while n % t:
        t //= 2
    return t


def _dot(a, b):
    return jnp.dot(a, b, preferred_element_type=F32)


def _split_bf16(a):
    hi = a.astype(BF16)
    lo = (a - hi.astype(F32)).astype(BF16)
    return hi, lo


def _dot3(a, b):
    ah, al = _split_bf16(a)
    bh, bl = _split_bf16(b)
    return _dot(ah, bh) + (_dot(al, bh) + _dot(ah, bl))


def _rms_mod(x, g, shift, scale):
    y = x * lax.rsqrt(jnp.mean(x * x, axis=-1, keepdims=True) + _EPS)
    return (y * g) * (1.0 + scale) + shift


def _sigmoid(x):
    return 1.0 / (1.0 + jnp.exp(-x))


def _ada_kernel(c_ref, w_ref, b_ref, o_ref):
    c = c_ref[...]
    s = c * _sigmoid(c)
    o_ref[...] = _dot3(s, w_ref[...]) + b_ref[...]


def _ada_call(c_rows, ada_w, ada_b):
    depth, d, n = ada_w.shape
    rows = c_rows.shape[0]
    tn = _tile(n, 1536)
    return pl.pallas_call(
        _ada_kernel,
        grid=(depth, n // tn),
        in_specs=[
            pl.BlockSpec((rows, d), lambda i, j: (0, 0)),
            pl.BlockSpec((None, d, tn), lambda i, j: (i, 0, j)),
            pl.BlockSpec((None, 1, tn), lambda i, j: (i, 0, j)),
        ],
        out_specs=pl.BlockSpec((None, rows, tn), lambda i, j: (i, 0, j)),
        out_shape=jax.ShapeDtypeStruct((depth, rows, n), F32),
        compiler_params=_params(("parallel", "parallel"), 40),
        name="ada_mod",
    )(c_rows, ada_w, ada_b.reshape(depth, 1, n))


def _nmm_kernel(x_ref, g_ref, mod_ref, w_ref, o_ref, *, sh, sc):
    h = _rms_mod(x_ref[...], g_ref[...], mod_ref[sh:sh + 1, :], mod_ref[sc:sc + 1, :])
    o_ref[...] = _dot(h.astype(BF16), w_ref[...]).astype(o_ref.dtype)


def _nmm_call(x, g, mod, w, sh, sc, out_dtype):
    b, l, d = x.shape
    n = w.shape[1]
    tm = _tile(l, 512)
    return pl.pallas_call(
        functools.partial(_nmm_kernel, sh=sh, sc=sc),
        grid=(b, l // tm),
        in_specs=[
            pl.BlockSpec((None, tm, d), lambda i, t: (i, t, 0)),
            pl.BlockSpec((1, d), lambda i, t: (0, 0)),
            pl.BlockSpec((None, 6, d), lambda i, t: (i, 0, 0)),
            pl.BlockSpec((d, n), lambda i, t: (0, 0)),
        ],
        out_specs=pl.BlockSpec((None, tm, n), lambda i, t: (i, t, 0)),
        out_shape=jax.ShapeDtypeStruct((b, l, n), out_dtype),
        compiler_params=_params(("parallel", "parallel"), 48),
        name="norm_mod_matmul",
    )(x, g.reshape(1, d), mod, w)


def _log1p_pos(e):
    u = 1.0 + e
    den = jnp.where(u == 1.0, 1.0, u - 1.0)
    return jnp.where(u == 1.0, e, jnp.log(u) * (e * (1.0 / den)))


def _softplus(x):
    return jnp.maximum(x, 0.0) + _log1p_pos(jnp.exp(-jnp.abs(x)))


def _expm1_neg(x, u):
    lu = jnp.log(jnp.where(u == 0.0, 1.0, u))
    den = jnp.where(lu == 0.0, 1.0, lu)
    r = (u - 1.0) * (x * (1.0 / den))
    return jnp.where(u == 1.0, x, jnp.where(u == 0.0, -1.0, r))


def _scan_rows(a, b, h, reverse):
    tl = a.shape[0]
    nb = a.shape[1] // _LANES
    a3 = pltpu.einshape("t(jc)->tjc", a, j=nb)
    b3 = pltpu.einshape("t(jc)->tjc", b, j=nb)
    states = [None] * tl
    for t in (range(tl - 1, -1, -1) if reverse else range(tl)):
        h = a3[t] * h + b3[t]
        states[t] = h
    return pltpu.einshape("tjc->t(jc)", jnp.stack(states, axis=0)), h


def _lru_kernel(*refs, reverse, tl, nt, fuse):
    if fuse:
        (xm_ref, xp_ref, xn_ref, cw_ref, cb_ref, wa_ref, ba_ref, wi_ref, bi_ref, lam_ref, h0_ref,
         yo_ref, gb_ref, out_ref, hl_ref, carry_ref) = refs
    else:
        (xm_ref, xp_ref, xn_ref, cw_ref, cb_ref, wa_ref, ba_ref, wi_ref, bi_ref, lam_ref, h0_ref,
         out_ref, hl_ref, carry_ref) = refs
    t = pl.program_id(1)
    tt = (nt - 1 - t) if reverse else t

    @pl.when(t == 0)
    def _():
        carry_ref[...] = h0_ref[...]

    xm = xm_ref[...]
    prev = jnp.where(tt > 0, xp_ref[...], 0.0)
    nxt = jnp.where(tt < nt - 1, xn_ref[...], 0.0)
    row8 = lax.broadcasted_iota(jnp.int32, prev.shape, 0)

    def tap(offset):
        if offset < 0:
            s = -offset
            y = pltpu.roll(xm, s, axis=0)
            head = jnp.where(row8 < s, pltpu.roll(prev, s, axis=0), y[:_SUBLANES, :])
            return jnp.concatenate([head, y[_SUBLANES:, :]], axis=0)
        y = pltpu.roll(xm, tl - offset, axis=0)
        tail = jnp.where(row8 >= _SUBLANES - offset, pltpu.roll(nxt, _SUBLANES - offset, axis=0),
                         y[tl - _SUBLANES:, :])
        return jnp.concatenate([y[:tl - _SUBLANES, :], tail], axis=0)

    cw = cw_ref[...]
    left = _CONV_W // 2
    xc = cb_ref[...]
    for k in range(_CONV_W):
        xk = xm if k == left else tap(k - left)
        xc = xc + xk * cw[k:k + 1, :]

    xcb = xc.astype(BF16)
    nb = wa_ref.shape[0]
    blocks = [slice(j * _LANES, (j + 1) * _LANES) for j in range(nb)]
    pre_r = jnp.concatenate([_dot(xcb[:, s], wa_ref[j]) for j, s in enumerate(blocks)], axis=1)
    pre_i = jnp.concatenate([_dot(xcb[:, s], wi_ref[j]) for j, s in enumerate(blocks)], axis=1)
    r = _sigmoid(pre_r + ba_ref[...])
    i = _sigmoid(pre_i + bi_ref[...])
    log_a = (-_LRU_C * r) * _softplus(-lam_ref[...])
    a = jnp.exp(log_a)
    mult = jnp.sqrt(-_expm1_neg(2.0 * log_a, a * a))
    b = mult * (i * xc)

    hs, new_carry = _scan_rows(a, b, carry_ref[...], reverse)
    carry_ref[...] = new_carry
    hl_ref[...] = new_carry
    if fuse:
        gate = jax.nn.gelu(gb_ref[...], approximate=True)
        out_ref[...] = ((hs + yo_ref[...]) * gate).astype(out_ref.dtype)
    else:
        out_ref[...] = hs


def _lru_call(xg, conv_w, conv_b, w_a, b_a, w_i, b_i, lam, h0, y_other, *, reverse):
    b, l, c2 = xg.shape
    c = c2 // 2
    nb = c // _LANES
    tl = _tile(l, 256)
    nt = l // tl
    hb = tl // _SUBLANES
    nh = l // _SUBLANES
    fuse = y_other is not None

    def tmap(t):
        return (nt - 1 - t) if reverse else t

    vec = lambda: pl.BlockSpec((1, c), lambda i, t: (0, 0))
    gate_w = lambda: pl.BlockSpec((nb, _LANES, _LANES), lambda i, t: (0, 0, 0))
    in_specs = [
        pl.BlockSpec((None, tl, c), lambda i, t: (i, tmap(t), 0)),
        pl.BlockSpec((None, _SUBLANES, c), lambda i, t: (i, jnp.maximum(tmap(t) * hb - 1, 0), 0)),
        pl.BlockSpec((None, _SUBLANES, c), lambda i, t: (i, jnp.minimum((tmap(t) + 1) * hb, nh - 1), 0)),
        pl.BlockSpec((_CONV_W, c), lambda i, t: (0, 0)),
        vec(),
        gate_w(),
        vec(),
        gate_w(),
        vec(),
        vec(),
        pl.BlockSpec((None, nb, _LANES), lambda i, t: (i, 0, 0)),
    ]
    args = [xg, xg, xg, conv_w, conv_b.reshape(1, c), w_a, b_a.reshape(1, c), w_i, b_i.reshape(1, c),
            lam.reshape(1, c), h0]
    if fuse:
        in_specs += [
            pl.BlockSpec((None, tl, c), lambda i, t: (i, tmap(t), 0)),
            pl.BlockSpec((None, tl, c), lambda i, t: (i, tmap(t), 1)),
        ]
        args += [y_other, xg]
    out_dtype = BF16 if fuse else F32
    return pl.pallas_call(
        functools.partial(_lru_kernel, reverse=reverse, tl=tl, nt=nt, fuse=fuse),
        grid=(b, nt),
        in_specs=in_specs,
        out_specs=[
            pl.BlockSpec((None, tl, c), lambda i, t: (i, tmap(t), 0)),
            pl.BlockSpec((None, nb, _LANES), lambda i, t: (i, 0, 0)),
        ],
        out_shape=[jax.ShapeDtypeStruct((b, l, c), out_dtype), jax.ShapeDtypeStruct((b, nb, _LANES), F32)],
        scratch_shapes=[pltpu.VMEM((nb, _LANES), F32)],
        compiler_params=_params(("parallel", "arbitrary"), 48),
        name="lru_rev" if reverse else "lru_fwd",
    )(*args)


def _proj_res_kernel(z_ref, w_ref, x_ref, mod_ref, o_ref, *, gi):
    o_ref[...] = x_ref[...] + mod_ref[gi:gi + 1, :] * _dot(z_ref[...], w_ref[...])


def _proj_res_call(z, w, x, mod, gi):
    b, l, d = x.shape
    k = z.shape[2]
    tm = _tile(l, 1024)
    return pl.pallas_call(
        functools.partial(_proj_res_kernel, gi=gi),
        grid=(b, l // tm),
        in_specs=[
            pl.BlockSpec((None, tm, k), lambda i, t: (i, t, 0)),
            pl.BlockSpec((k, d), lambda i, t: (0, 0)),
            pl.BlockSpec((None, tm, d), lambda i, t: (i, t, 0)),
            pl.BlockSpec((None, 6, d), lambda i, t: (i, 0, 0)),
        ],
        out_specs=pl.BlockSpec((None, tm, d), lambda i, t: (i, t, 0)),
        out_shape=jax.ShapeDtypeStruct((b, l, d), F32),
        compiler_params=_params(("parallel", "parallel"), 48),
        name="proj_residual",
    )(z, w, x, mod)


def _ffn_kernel(x_ref, g_ref, mod_ref, w1_ref, w3_ref, w2_ref, o_ref):
    x = x_ref[...]
    h = _rms_mod(x, g_ref[...], mod_ref[3:4, :], mod_ref[4:5, :]).astype(BF16)
    h1 = _dot(h, w1_ref[...])
    h3 = _dot(h, w3_ref[...])
    hid = (h1 * _sigmoid(h1)) * h3
    o_ref[...] = x + mod_ref[5:6, :] * _dot(hid.astype(BF16), w2_ref[...])


def _ffn_call(x, g, mod, w1, w3, w2):
    b, l, d = x.shape
    ff = w1.shape[1]
    tm = _tile(l, 512)
    resident = pl.Buffered(1)
    return pl.pallas_call(
        _ffn_kernel,
        grid=(b, l // tm),
        in_specs=[
            pl.BlockSpec((None, tm, d), lambda i, t: (i, t, 0)),
            pl.BlockSpec((1, d), lambda i, t: (0, 0)),
            pl.BlockSpec((None, 6, d), lambda i, t: (i, 0, 0)),
            pl.BlockSpec((d, ff), lambda i, t: (0, 0), pipeline_mode=resident),
            pl.BlockSpec((d, ff), lambda i, t: (0, 0), pipeline_mode=resident),
            pl.BlockSpec((ff, d), lambda i, t: (0, 0), pipeline_mode=resident),
        ],
        out_specs=pl.BlockSpec((None, tm, d), lambda i, t: (i, t, 0)),
        out_shape=jax.ShapeDtypeStruct((b, l, d), F32),
        compiler_params=_params(("parallel", "parallel"), 56),
        name="dense_swiglu",
    )(x, g.reshape(1, d), mod, w1, w3, w2)


def _rope_tables(l):
    pos = jnp.arange(l)
    row = (pos // _GRID_W).astype(F32)
    col = (pos % _GRID_W).astype(F32)
    freqs = _ROPE_THETA ** (-jnp.arange(_ROPE_NFREQ, dtype=F32) / _ROPE_NFREQ)
    ar = row[:, None] * freqs
    ac = col[:, None] * freqs
    cos = jnp.concatenate([jnp.cos(ar), jnp.cos(ar), jnp.cos(ac), jnp.cos(ac)], axis=-1)
    sin = jnp.concatenate([-jnp.sin(ar), jnp.sin(ar), -jnp.sin(ac), jnp.sin(ac)], axis=-1)
    return cos, sin


def _qkv_kernel(x_ref, g_ref, mod_ref, w_ref, qg_ref, kg_ref, cos_ref, sin_ref, q_ref, k_ref, v_ref,
                qn_ref, kn_ref, *, n_q, n_kv, q_scale):
    h = _rms_mod(x_ref[...], g_ref[...], mod_ref[0:1, :], mod_ref[1:2, :])
    qkv = _dot(h.astype(BF16), w_ref[...])
    cos = cos_ref[...]
    sin = sin_ref[...]
    lane = lax.broadcasted_iota(jnp.int32, cos.shape, 1)
    first = (lane % (2 * _ROPE_NFREQ)) < _ROPE_NFREQ

    def norm_rope(v, gain):
        y = v * lax.rsqrt(jnp.mean(v * v, axis=-1, keepdims=True) + _EPS) * gain
        partner = jnp.where(first, pltpu.roll(y, _HEAD_DIM - _ROPE_NFREQ, axis=1),
                            pltpu.roll(y, _ROPE_NFREQ, axis=1))
        return y * cos + partner * sin

    def max_sq_norm(val, best):
        return jnp.maximum(best, jnp.max(jnp.sum(val * val, axis=-1, keepdims=True), axis=0, keepdims=True))

    q_best = jnp.zeros((1, 1), F32)
    for hh in range(n_q):
        sl = slice(hh * _HEAD_DIM, (hh + 1) * _HEAD_DIM)
        val = norm_rope(qkv[:, sl], qg_ref[...]) * q_scale
        q_best = max_sq_norm(val, q_best)
        q_ref[:, sl] = val.astype(q_ref.dtype)
    k_best = jnp.zeros((1, 1), F32)
    for hh in range(n_kv):
        src = slice((n_q + hh) * _HEAD_DIM, (n_q + hh + 1) * _HEAD_DIM)
        val = norm_rope(qkv[:, src], kg_ref[...])
        k_best = max_sq_norm(val, k_best)
        k_ref[hh] = val.T.astype(k_ref.dtype)
    ones_col = jnp.where(lane == 0, 1.0, 0.0).astype(v_ref.dtype)
    for hh in range(n_kv):
        src = slice((n_q + n_kv + hh) * _HEAD_DIM, (n_q + n_kv + hh + 1) * _HEAD_DIM)
        v_ref[:, 2 * hh * _HEAD_DIM:(2 * hh + 1) * _HEAD_DIM] = qkv[:, src].astype(v_ref.dtype)
        v_ref[:, (2 * hh + 1) * _HEAD_DIM:(2 * hh + 2) * _HEAD_DIM] = ones_col
    qn_ref[...] = jnp.broadcast_to(q_best, qn_ref.shape)
    kn_ref[...] = jnp.broadcast_to(k_best, kn_ref.shape)


def _qkv_call(x, g, mod, w, q_g, k_g, cos, sin, n_q, n_kv):
    b, l, d = x.shape
    n = w.shape[1]
    tm = _tile(l, 512)
    nq = n_q * _HEAD_DIM
    nkv = n_kv * _HEAD_DIM
    q_scale = (_HEAD_DIM ** -0.5) * _LOG2E
    return pl.pallas_call(
        functools.partial(_qkv_kernel, n_q=n_q, n_kv=n_kv, q_scale=q_scale),
        grid=(b, l // tm),
        in_specs=[
            pl.BlockSpec((None, tm, d), lambda i, t: (i, t, 0)),
            pl.BlockSpec((1, d), lambda i, t: (0, 0)),
            pl.BlockSpec((None, 6, d), lambda i, t: (i, 0, 0)),
            pl.BlockSpec((d, n), lambda i, t: (0, 0)),
            pl.BlockSpec((1, _HEAD_DIM), lambda i, t: (0, 0)),
            pl.BlockSpec((1, _HEAD_DIM), lambda i, t: (0, 0)),
            pl.BlockSpec((tm, _HEAD_DIM), lambda i, t: (t, 0)),
            pl.BlockSpec((tm, _HEAD_DIM), lambda i, t: (t, 0)),
        ],
        out_specs=[
            pl.BlockSpec((None, tm, nq), lambda i, t: (i, t, 0)),
            pl.BlockSpec((None, n_kv, _HEAD_DIM, tm), lambda i, t: (i, 0, 0, t)),
            pl.BlockSpec((None, tm, 2 * nkv), lambda i, t: (i, t, 0)),
            pl.BlockSpec((None, None, _SUBLANES, _LANES), lambda i, t: (i, t, 0, 0)),
            pl.BlockSpec((None, None, _SUBLANES, _LANES), lambda i, t: (i, t, 0, 0)),
        ],
        out_shape=[
            jax.ShapeDtypeStruct((b, l, nq), BF16),
            jax.ShapeDtypeStruct((b, n_kv, _HEAD_DIM, l), BF16),
            jax.ShapeDtypeStruct((b, l, 2 * nkv), BF16),
            jax.ShapeDtypeStruct((b, l // tm, _SUBLANES, _LANES), F32),
            jax.ShapeDtypeStruct((b, l // tm, _SUBLANES, _LANES), F32),
        ],
        compiler_params=_params(("parallel", "parallel"), 48),
        name="qkv_norm_rope",
    )(x, g.reshape(1, d), mod, w, q_g.reshape(1, _HEAD_DIM), k_g.reshape(1, _HEAD_DIM), cos, sin)


def _lane_tile(x, reps):
    return x if reps == 1 else jnp.concatenate([x] * reps, axis=1)


def _attn_kernel(q_ref, k_ref, v_ref, o_ref, m_ref, l_ref, acc_ref, *, tq, tk, groups, n_chunks):
    q = jnp.concatenate([q_ref[:, g * _HEAD_DIM:(g + 1) * _HEAD_DIM] for g in range(groups)], axis=0)
    m_ref[...] = jnp.full_like(m_ref, -jnp.inf)
    l_ref[...] = jnp.zeros_like(l_ref)
    acc_ref[...] = jnp.zeros_like(acc_ref)

    def body(c, carry):
        start = pl.multiple_of(c * tk, tk)
        k = k_ref[pl.ds(start, tk), :]
        v = v_ref[pl.ds(start, tk), :]
        s = lax.dot_general(q, k, (((1,), (1,)), ((), ())), preferred_element_type=F32)
        m_prev = m_ref[...]
        m_next = jnp.maximum(m_prev, jnp.max(s, axis=1, keepdims=True))
        alpha = jnp.exp2(m_prev - m_next)
        p = jnp.exp2(s - _lane_tile(m_next, tk // _LANES))
        l_ref[...] = alpha * l_ref[...] + jnp.sum(p, axis=1, keepdims=True)
        acc_ref[...] = alpha * acc_ref[...] + _dot(p.astype(BF16), v)
        m_ref[...] = m_next
        return carry

    lax.fori_loop(0, n_chunks, body, 0)
    o = acc_ref[...] * (1.0 / l_ref[...])
    for g in range(groups):
        o_ref[:, g * _HEAD_DIM:(g + 1) * _HEAD_DIM] = o[g * tq:(g + 1) * tq, :].astype(o_ref.dtype)


def _attn_bounded_kernel(q_ref, kt_ref, v_ref, o_ref, *, tq, tk, groups, n_chunks):
    q = jnp.concatenate([q_ref[:, g * _HEAD_DIM:(g + 1) * _HEAD_DIM] for g in range(groups)], axis=0)
    acc = None
    for c in range(n_chunks):
        p = jnp.exp2(_dot(q, kt_ref[:, c * tk:(c + 1) * tk])).astype(BF16)
        pv = _dot(p, v_ref[c * tk:(c + 1) * tk, :])
        acc = pv if acc is None else acc + pv
    o = acc[:, :_HEAD_DIM] * (1.0 / acc[:, _HEAD_DIM:_HEAD_DIM + 1])
    for g in range(groups):
        o_ref[:, g * _HEAD_DIM:(g + 1) * _HEAD_DIM] = o[g * tq:(g + 1) * tq, :].astype(o_ref.dtype)


_SCORE_BOUND = 60.0
_KV_CHUNKS = (256, 128)


def _attn_bounded_call(q, kt, vx, n_kv):
    b, l, nq = q.shape
    s = kt.shape[3]
    groups = nq // (n_kv * _HEAD_DIM)
    gw = groups * _HEAD_DIM
    tq = _tile(l, 256)
    tk = next(c for c in _KV_CHUNKS if s % c == 0)
    n_chunks = s // tk
    vw = 2 * _HEAD_DIM
    return pl.pallas_call(
        functools.partial(_attn_bounded_kernel, tq=tq, tk=tk, groups=groups, n_chunks=n_chunks),
        grid=(b, n_kv, l // tq),
        in_specs=[
            pl.BlockSpec((None, tq, gw), lambda i, j, t: (i, t, j)),
            pl.BlockSpec((None, None, _HEAD_DIM, s), lambda i, j, t: (i, j, 0, 0)),
            pl.BlockSpec((None, s, vw), lambda i, j, t: (i, 0, j)),
        ],
        out_specs=pl.BlockSpec((None, tq, gw), lambda i, j, t: (i, t, j)),
        out_shape=jax.ShapeDtypeStruct((b, l, nq), BF16),
        compiler_params=_params(("parallel", "parallel", "arbitrary"), 56),
        name="gqa_attention_bounded",
    )(q, kt, vx)


def _attn_call(q, kt, vx, n_kv):
    b, l, nq = q.shape
    s = kt.shape[3]
    k = kt.transpose(0, 3, 1, 2).reshape(b, s, n_kv * _HEAD_DIM)
    v = vx.reshape(b, s, n_kv, 2 * _HEAD_DIM)[..., :_HEAD_DIM].reshape(b, s, n_kv * _HEAD_DIM)
    groups = nq // (n_kv * _HEAD_DIM)
    gw = groups * _HEAD_DIM
    tq = _tile(l, 256)
    tk = 768 if s % 768 == 0 else _tile(s, 512)
    n_chunks = s // tk
    m_rows = groups * tq
    return pl.pallas_call(
        functools.partial(_attn_kernel, tq=tq, tk=tk, groups=groups, n_chunks=n_chunks),
        grid=(b, n_kv, l // tq),
        in_specs=[
            pl.BlockSpec((None, tq, gw), lambda i, j, t: (i, t, j)),
            pl.BlockSpec((None, s, _HEAD_DIM), lambda i, j, t: (i, 0, j)),
            pl.BlockSpec((None, s, _HEAD_DIM), lambda i, j, t: (i, 0, j)),
        ],
        out_specs=pl.BlockSpec((None, tq, gw), lambda i, j, t: (i, t, j)),
        out_shape=jax.ShapeDtypeStruct((b, l, nq), BF16),
        scratch_shapes=[
            pltpu.VMEM((m_rows, _LANES), F32),
            pltpu.VMEM((m_rows, _LANES), F32),
            pltpu.VMEM((m_rows, _HEAD_DIM), F32),
        ],
        compiler_params=_params(("parallel", "parallel", "arbitrary"), 48),
        name="gqa_flash_attention",
    )(q, k, v)


_INFO_E0, _INFO_E1, _INFO_R0, _INFO_R1, _INFO_G0, _INFO_G1 = range(6)


def _route_kernel(x_ref, g_ref, mod_ref, rw_ref, h_ref, info_ref, slots_ref, cnt_ref, carry_ref, *, n_exp, tm):
    step = pl.program_id(0)

    @pl.when(step == 0)
    def _():
        carry_ref[...] = jnp.zeros_like(carry_ref)

    h = _rms_mod(x_ref[...], g_ref[...], mod_ref[3:4, :], mod_ref[4:5, :])
    h_ref[...] = h
    logits = _dot3(h, rw_ref[...])
    lane = lax.broadcasted_iota(jnp.int32, logits.shape, 1)
    neg = jnp.float32(-jnp.inf)
    lg = jnp.where(lane < n_exp, logits, neg)
    m0 = jnp.max(lg, axis=1, keepdims=True)
    i0 = jnp.min(jnp.where(lg == m0, lane, _LANES), axis=1, keepdims=True)
    oh0 = lane == i0
    lg1 = jnp.where(oh0, neg, lg)
    m1 = jnp.max(lg1, axis=1, keepdims=True)
    i1 = jnp.min(jnp.where(lg1 == m1, lane, _LANES), axis=1, keepdims=True)
    oh1 = lane == i1
    e = jnp.exp(m1 - m0)
    g0 = 1.0 / (1.0 + e)
    g1 = e * g0

    chosen = jnp.where(oh0 | oh1, 1.0, 0.0)
    r = lax.broadcasted_iota(jnp.int32, (tm, tm), 0)
    cidx = lax.broadcasted_iota(jnp.int32, (tm, tm), 1)
    tri = jnp.where(r > cidx, 1.0, 0.0).astype(BF16)
    before = _dot(tri, chosen.astype(BF16)) + carry_ref[...]
    r0 = jnp.sum(jnp.where(oh0, before, 0.0), axis=1, keepdims=True)
    r1 = jnp.sum(jnp.where(oh1, before, 0.0), axis=1, keepdims=True)
    total = carry_ref[...] + jnp.sum(chosen, axis=0, keepdims=True)
    carry_ref[...] = total
    cnt_ref[...] = total

    info = jnp.zeros(logits.shape, F32)
    for idx, val in ((_INFO_E0, i0.astype(F32)), (_INFO_E1, i1.astype(F32)), (_INFO_R0, r0), (_INFO_R1, r1),
                     (_INFO_G0, g0), (_INFO_G1, g1)):
        info = jnp.where(lane == idx, val, info)
    info_ref[...] = info
    slots_ref[...] = info.T[:_SUBLANES, :].astype(jnp.int32)


def _route_call(x, g, mod, router):
    b, l, d = x.shape
    n_exp = router.shape[1]
    t = b * l
    tm = _tile(l, 512)
    per_b = l // tm
    rw = jnp.zeros((d, _LANES), F32).at[:, :n_exp].set(router)
    return pl.pallas_call(
        functools.partial(_route_kernel, n_exp=n_exp, tm=tm),
        grid=(t // tm,),
        in_specs=[
            pl.BlockSpec((tm, d), lambda i: (i, 0)),
            pl.BlockSpec((1, d), lambda i: (0, 0)),
            pl.BlockSpec((None, 6, d), lambda i: (i // per_b, 0, 0)),
            pl.BlockSpec((d, _LANES), lambda i: (0, 0)),
        ],
        out_specs=[
            pl.BlockSpec((tm, d), lambda i: (i, 0)),
            pl.BlockSpec((tm, _LANES), lambda i: (i, 0)),
            pl.BlockSpec((_SUBLANES, tm), lambda i: (0, i)),
            pl.BlockSpec((1, _LANES), lambda i: (0, 0)),
        ],
        out_shape=[
            jax.ShapeDtypeStruct((t, d), F32),
            jax.ShapeDtypeStruct((t, _LANES), F32),
            jax.ShapeDtypeStruct((_SUBLANES, t), jnp.int32),
            jax.ShapeDtypeStruct((1, _LANES), F32),
        ],
        scratch_shapes=[pltpu.VMEM((1, _LANES), F32)],
        compiler_params=_params(("arbitrary",), 40),
        name="moe_route",
    )(x.reshape(t, d), g.reshape(1, d), mod, rw)


def _pos_tiles(pos0, pos1, tt):
    n = pos0.shape[0] // tt
    return jnp.stack([pos0.reshape(n, tt), pos1.reshape(n, tt)], axis=1).reshape(n * 2 * tt)


def _row_copy(src, src_row, dst, dst_row, sem):
    return pltpu.make_async_copy(src.at[pl.ds(src_row, 1)], dst.at[pl.ds(dst_row, 1)], sem)


_ROW_DMA_UNROLL = 8


def _dispatch_kernel(pos_ref, h_ref, xs_in_ref, xs_ref, pos_smem, sem_s, sem_d, *, tt):
    del xs_in_ref
    cp = pltpu.make_async_copy(pos_ref, pos_smem, sem_s)
    cp.start()
    cp.wait()

    def body(r, carry):
        _row_copy(h_ref, r, xs_ref, pos_smem[r], sem_d).start(priority=0)
        _row_copy(h_ref, r, xs_ref, pos_smem[tt + r], sem_d).start(priority=1)
        return carry

    lax.fori_loop(0, tt, body, 0, unroll=_ROW_DMA_UNROLL)
    for _ in range(2 * tt):
        _row_copy(h_ref, 0, xs_ref, 0, sem_d).wait()


def _dispatch_call(h, pos_t, p_rows, tt):
    t, d = h.shape
    xs0 = jnp.zeros((p_rows, d), F32)
    return pl.pallas_call(
        functools.partial(_dispatch_kernel, tt=tt),
        grid=(t // tt,),
        in_specs=[
            pl.BlockSpec((2 * tt,), lambda i: (i,)),
            pl.BlockSpec((tt, d), lambda i: (i, 0)),
            pl.BlockSpec(memory_space=pl.ANY),
        ],
        out_specs=pl.BlockSpec(memory_space=pl.ANY),
        out_shape=jax.ShapeDtypeStruct((p_rows, d), F32),
        scratch_shapes=[
            pltpu.SMEM((2 * tt,), jnp.int32),
            pltpu.SemaphoreType.DMA(()),
            pltpu.SemaphoreType.DMA(()),
        ],
        input_output_aliases={2: 0},
        compiler_params=_params(("arbitrary",), 32),
        name="moe_dispatch",
    )(pos_t, h, xs0)


def _moe_kernel(be_ref, bs_ref, bv_ref, x_ref, w1_ref, w3_ref, w2_ref, o_ref, xb_ref, acc_ref, *, nf):
    del be_ref, bs_ref
    i = pl.program_id(0)
    f = pl.program_id(1)

    @pl.when(bv_ref[i] > 0)
    def _():
        @pl.when(f == 0)
        def _():
            xb_ref[...] = x_ref[...].astype(BF16)
            acc_ref[...] = jnp.zeros_like(acc_ref)

        xb = xb_ref[...]
        h1 = _dot(xb, w1_ref[...])
        h3 = _dot(xb, w3_ref[...])
        hid = (h1 * _sigmoid(h1)) * h3
        acc_ref[...] += _dot(hid.astype(BF16), w2_ref[...])

        @pl.when(f == nf - 1)
        def _():
            o_ref[...] = acc_ref[...]


def _moe_call(blk_e, blk_src, blk_valid, xs, w1, w3, w2, tmm):
    p_rows, d = xs.shape
    ff = w1.shape[2]
    tf = 1792 if ff % 1792 == 0 else _tile(ff, 512)
    nf = ff // tf
    nblk = p_rows // tmm

    def fsel(i, f, bv):
        return jnp.where(bv[i] > 0, f, nf - 1)

    grid_spec = pltpu.PrefetchScalarGridSpec(
        num_scalar_prefetch=3,
        grid=(nblk, nf),
        in_specs=[
            pl.BlockSpec((tmm, d), lambda i, f, be, bs, bv: (bs[i], 0)),
            pl.BlockSpec((None, d, tf), lambda i, f, be, bs, bv: (be[i], 0, fsel(i, f, bv))),
            pl.BlockSpec((None, d, tf), lambda i, f, be, bs, bv: (be[i], 0, fsel(i, f, bv))),
            pl.BlockSpec((None, tf, d), lambda i, f, be, bs, bv: (be[i], fsel(i, f, bv), 0)),
        ],
        out_specs=pl.BlockSpec((tmm, d), lambda i, f, be, bs, bv: (bs[i], 0)),
        scratch_shapes=[pltpu.VMEM((tmm, d), BF16), pltpu.VMEM((tmm, d), F32)],
    )
    return pl.pallas_call(
        functools.partial(_moe_kernel, nf=nf),
        grid_spec=grid_spec,
        out_shape=jax.ShapeDtypeStruct((p_rows, d), F32),
        compiler_params=_params(("arbitrary", "arbitrary"), 58),
        name="moe_experts",
    )(blk_e, blk_src, blk_valid, xs, w1, w3, w2)


def _combine_kernel(pos_ref, pos_next_ref, x_ref, mod_ref, info_ref, fg_ref, y_ref, o_ref, pos_smem, ybuf,
                    sem_s, sem_d, *, tt, n_steps):
    i = pl.program_id(0)
    slot = lax.rem(i, 2)

    def start_gathers(p_ref, s):
        cp = pltpu.make_async_copy(p_ref, pos_smem, sem_s)
        cp.start()
        cp.wait()

        def body(r, carry):
            _row_copy(y_ref, pos_smem[r], ybuf.at[s, 0], r, sem_d.at[s]).start(priority=0)
            _row_copy(y_ref, pos_smem[tt + r], ybuf.at[s, 1], r, sem_d.at[s]).start(priority=1)
            return carry

        lax.fori_loop(0, tt, body, 0, unroll=_ROW_DMA_UNROLL)

    @pl.when(i == 0)
    def _():
        start_gathers(pos_ref, 0)

    @pl.when(i + 1 < n_steps)
    def _():
        start_gathers(pos_next_ref, 1 - slot)

    for _ in range(2 * tt):
        _row_copy(y_ref, 0, ybuf.at[slot, 0], 0, sem_d.at[slot]).wait()

    info = info_ref[...]
    g0 = info[:, _INFO_G0:_INFO_G0 + 1]
    g1 = info[:, _INFO_G1:_INFO_G1 + 1]
    ffn = g0 * ybuf[slot, 0] + g1 * ybuf[slot, 1]
    xo = x_ref[...] + mod_ref[5:6, :] * ffn
    o_ref[...] = xo * lax.rsqrt(jnp.mean(xo * xo, axis=-1, keepdims=True) + _EPS) * fg_ref[...]


def _combine_call(pos_t, x, mod, info, final_g, y, tt):
    b, l, d = x.shape
    t = b * l
    per_b = l // tt
    n_steps = t // tt
    return pl.pallas_call(
        functools.partial(_combine_kernel, tt=tt, n_steps=n_steps),
        grid=(n_steps,),
        in_specs=[
            pl.BlockSpec((2 * tt,), lambda i: (i,)),
            pl.BlockSpec((2 * tt,), lambda i: (jnp.minimum(i + 1, n_steps - 1),)),
            pl.BlockSpec((tt, d), lambda i: (i, 0)),
            pl.BlockSpec((None, 6, d), lambda i: (i // per_b, 0, 0)),
            pl.BlockSpec((tt, _LANES), lambda i: (i, 0)),
            pl.BlockSpec((1, d), lambda i: (0, 0)),
            pl.BlockSpec(memory_space=pl.ANY),
        ],
        out_specs=pl.BlockSpec((tt, d), lambda i: (i, 0)),
        out_shape=jax.ShapeDtypeStruct((t, d), F32),
        scratch_shapes=[
            pltpu.SMEM((2 * tt,), jnp.int32),
            pltpu.VMEM((2, 2, tt, d), F32),
            pltpu.SemaphoreType.DMA(()),
            pltpu.SemaphoreType.DMA((2,)),
        ],
        compiler_params=_params(("arbitrary",), 40),
        name="moe_combine_norm",
    )(pos_t, pos_t, x.reshape(t, d), mod, info, final_g.reshape(1, d), y).reshape(b, l, d)


def _moe_layer(x, g, mod, router, w1, w3, w2, final_g):
    b, l, d = x.shape
    t = b * l
    n_exp = router.shape[1]
    tmm = 512 if t >= 4096 else 128
    tt = _tile(l, 512)
    h, info, slots, counts = _route_call(x, g, mod, router)

    counts = counts[0, :n_exp].astype(jnp.int32)
    padded = ((counts + tmm - 1) // tmm) * tmm
    pend = jnp.cumsum(padded)
    gstart = pend - padded
    pos0 = gstart[slots[_INFO_E0]] + slots[_INFO_R0]
    pos1 = gstart[slots[_INFO_E1]] + slots[_INFO_R1]
    pos_t = _pos_tiles(pos0, pos1, tt)

    nblk = (t * 2) // tmm + n_exp
    p_rows = nblk * tmm
    bstart = jnp.arange(nblk, dtype=jnp.int32) * tmm
    used = bstart < pend[-1]
    last = pend[-1] // tmm - 1
    bidx = jnp.where(used, jnp.arange(nblk, dtype=jnp.int32), last)
    blk_e = jnp.minimum(jnp.sum(((bidx * tmm)[:, None] >= pend[None, :]).astype(jnp.int32), axis=1), n_exp - 1)
    blk_valid = jnp.where(used, jnp.clip(counts[blk_e] - (bidx * tmm - gstart[blk_e]), 0, tmm), 0).astype(jnp.int32)

    xs = _dispatch_call(h, pos_t, p_rows, tt)
    y = _moe_call(blk_e, bidx.astype(jnp.int32), blk_valid, xs, w1.astype(BF16), w3.astype(BF16), w2.astype(BF16),
                  tmm)
    return _combine_call(pos_t, x, mod, info, final_g, y, tt)


def kernel(x, c, ctx, c_ctx, ada_w, ada_b, norm1_g, norm2_g, rg_w_in, rg_conv_w, rg_conv_b, rg_w_a, rg_b_a,
           rg_w_i, rg_b_i, rg_lam, rg_w_out, attn_w_qkv, attn_q_g, attn_k_g, attn_w_o, ffn_w1, ffn_w3, ffn_w2,
           moe_router, moe_w1, moe_w3, moe_w2, final_g):
    b, l, d = x.shape
    n_ctx = ctx.shape[1]
    c_dim = rg_w_out.shape[1]
    n_q = attn_w_o.shape[1] // _HEAD_DIM
    n_kv = (attn_w_qkv.shape[2] // _HEAD_DIM - n_q) // 2

    rows = 16
    c_rows = jnp.zeros((rows, d), F32).at[:b].set(c).at[b].set(c_ctx)
    mod = _ada_call(c_rows, ada_w, ada_b).reshape(ada_w.shape[0], rows, 6, d)
    mod_l = [mod[i, :b] for i in range(2)]
    mod_c = [jnp.broadcast_to(mod[i, b], (b, 6, d)) for i in range(2)]

    w_in = rg_w_in[0].astype(BF16)
    w_out = rg_w_out[0].astype(BF16)
    w_a = rg_w_a[0].astype(BF16)
    w_i = rg_w_i[0].astype(BF16)
    xg_l = _nmm_call(x, norm1_g[0], mod_l[0], w_in, 0, 1, F32)
    xg_c = _nmm_call(ctx, norm1_g[0], mod_c[0], w_in, 0, 1, F32)
    zero_h = jnp.zeros((b, c_dim // _LANES, _LANES), F32)

    def lru(xg, h0, y_other, direction, reverse):
        return _lru_call(xg, rg_conv_w[0], rg_conv_b[0], w_a[direction], rg_b_a[0, direction], w_i[direction],
                         rg_b_i[0, direction], rg_lam[0, direction], h0, y_other, reverse=reverse)

    yc_rev, h0_rev = lru(xg_c, zero_h, None, 1, True)
    z_c, h0_fwd = lru(xg_c, zero_h, yc_rev, 0, False)
    yl_rev, _ = lru(xg_l, h0_rev, None, 1, True)
    z_l, _ = lru(xg_l, h0_fwd, yl_rev, 0, False)
    x = _proj_res_call(z_l, w_out, x, mod_l[0], 2)
    ctx = _proj_res_call(z_c, w_out, ctx, mod_c[0], 2)
    f1, f3, f2 = ffn_w1[0].astype(BF16), ffn_w3[0].astype(BF16), ffn_w2[0].astype(BF16)
    x = _ffn_call(x, norm2_g[0], mod_l[0], f1, f3, f2)
    ctx = _ffn_call(ctx, norm2_g[0], mod_c[0], f1, f3, f2)

    w_qkv = attn_w_qkv[0].astype(BF16)
    cos, sin = _rope_tables(l)
    q, k_l, v_l, qn_l, kn_l = _qkv_call(x, norm1_g[1], mod_l[1], w_qkv, attn_q_g[0], attn_k_g[0], cos, sin,
                                        n_q, n_kv)
    ones = jnp.ones((n_ctx, _HEAD_DIM), F32)
    _, k_c, v_c, _, kn_c = _qkv_call(ctx, norm1_g[1], mod_c[1], w_qkv, attn_q_g[0], attn_k_g[0], ones,
                                     jnp.zeros_like(ones), n_q, n_kv)
    k_all = jnp.concatenate([k_c, k_l], axis=3)
    v_all = jnp.concatenate([v_c, v_l], axis=1)
    score_bound = jnp.sqrt(jnp.max(qn_l) * jnp.maximum(jnp.max(kn_l), jnp.max(kn_c)))
    o = lax.cond(score_bound <= _SCORE_BOUND,
                 lambda: _attn_bounded_call(q, k_all, v_all, n_kv),
                 lambda: _attn_call(q, k_all, v_all, n_kv))
    x = _proj_res_call(o, attn_w_o[0].astype(BF16), x, mod_l[1], 2)
    return _moe_layer(x, norm2_g[1], mod_l[1], moe_router[0], moe_w1[0], moe_w3[0], moe_w2[0], final_g)
```

```python
import functools

import jax
import jax.numpy as jnp
from jax import lax
from jax.experimental import pallas as pl
from jax.experimental.pallas import tpu as pltpu

F32 = jnp.float32
BF16 = jnp.bfloat16

_EPS = 1e-6
_HEAD_DIM = 128
_GRID_W = 64
_ROPE_THETA = 10000.0
_ROPE_NFREQ = _HEAD_DIM // 4
_LRU_C = 8.0
_CONV_W = 4
_LANES = 128
_SUBLANES = 8
_LOG2E = 1.4426950408889634
_MIB = 1024 * 1024


def _params(semantics, vmem_mib):
    return pltpu.CompilerParams(dimension_semantics=semantics, vmem_limit_bytes=vmem_mib * _MIB)


def _tile(n, pref):
    if n <= pref:
        return n
    t = pref
    while n % t:
        t //= 2
    return t


def _dot(a, b):
    return jnp.dot(a, b, preferred_element_type=F32)


def _split_bf16(a):
    hi = a.astype(BF16)
    lo = (a - hi.astype(F32)).astype(BF16)
    return hi, lo


def _dot3(a, b):
    ah, al = _split_bf16(a)
    bh, bl = _split_bf16(b)
    return _dot(ah, bh) + (_dot(al, bh) + _dot(ah, bl))


def _rms_mod(x, g, shift, scale):
    y = x * lax.rsqrt(jnp.mean(x * x, axis=-1, keepdims=True) + _EPS)
    return (y * g) * (1.0 + scale) + shift


def _sigmoid(x):
    return 1.0 / (1.0 + jnp.exp(-x))


def _ada_kernel(c_ref, w_ref, b_ref, o_ref):
    c = c_ref[...]
    s = c * _sigmoid(c)
    o_ref[...] = _dot3(s, w_ref[...]) + b_ref[...]


def _ada_call(c_rows, ada_w, ada_b):
    depth, d, n = ada_w.shape
    rows = c_rows.shape[0]
    tn = _tile(n, 1536)
    return pl.pallas_call(
        _ada_kernel,
        grid=(depth, n // tn),
        in_specs=[
            pl.BlockSpec((rows, d), lambda i, j: (0, 0)),
            pl.BlockSpec((None, d, tn), lambda i, j: (i, 0, j)),
            pl.BlockSpec((None, 1, tn), lambda i, j: (i, 0, j)),
        ],
        out_specs=pl.BlockSpec((None, rows, tn), lambda i, j: (i, 0, j)),
        out_shape=jax.ShapeDtypeStruct((depth, rows, n), F32),
        compiler_params=_params(("parallel", "parallel"), 40),
        name="ada_mod",
    )(c_rows, ada_w, ada_b.reshape(depth, 1, n))


def _nmm_kernel(x_ref, g_ref, mod_ref, w_ref, o_ref, *, sh, sc):
    h = _rms_mod(x_ref[...], g_ref[...], mod_ref[sh:sh + 1, :], mod_ref[sc:sc + 1, :])
    o_ref[...] = _dot(h.astype(BF16), w_ref[...]).astype(o_ref.dtype)


def _nmm_call(x, g, mod, w, sh, sc, out_dtype):
    b, l, d = x.shape
    n = w.shape[1]
    tm = _tile(l, 512)
    return pl.pallas_call(
        functools.partial(_nmm_kernel, sh=sh, sc=sc),
        grid=(b, l // tm),
        in_specs=[
            pl.BlockSpec((None, tm, d), lambda i, t: (i, t, 0)),
            pl.BlockSpec((1, d), lambda i, t: (0, 0)),
            pl.BlockSpec((None, 6, d), lambda i, t: (i, 0, 0)),
            pl.BlockSpec((d, n), lambda i, t: (0, 0)),
        ],
        out_specs=pl.BlockSpec((None, tm, n), lambda i, t: (i, t, 0)),
        out_shape=jax.ShapeDtypeStruct((b, l, n), out_dtype),
        compiler_params=_params(("parallel", "parallel"), 48),
        name="norm_mod_matmul",
    )(x, g.reshape(1, d), mod, w)


def _log1p_pos(e):
    u = 1.0 + e
    den = jnp.where(u == 1.0, 1.0, u - 1.0)
    return jnp.where(u == 1.0, e, jnp.log(u) * (e * (1.0 / den)))


def _softplus(x):
    return jnp.maximum(x, 0.0) + _log1p_pos(jnp.exp(-jnp.abs(x)))


def _expm1_neg(x, u):
    lu = jnp.log(jnp.where(u == 0.0, 1.0, u))
    den = jnp.where(lu == 0.0, 1.0, lu)
    r = (u - 1.0) * (x * (1.0 / den))
    return jnp.where(u == 1.0, x, jnp.where(u == 0.0, -1.0, r))


def _scan_rows(a, b, h, reverse):
    tl = a.shape[0]
    nb = a.shape[1] // _LANES
    a3 = pltpu.einshape("t(jc)->tjc", a, j=nb)
    b3 = pltpu.einshape("t(jc)->tjc", b, j=nb)
    states = [None] * tl
    for t in (range(tl - 1, -1, -1) if reverse else range(tl)):
        h = a3[t] * h + b3[t]
        states[t] = h
    return pltpu.einshape("tjc->t(jc)", jnp.stack(states, axis=0)), h


def _lru_kernel(*refs, reverse, tl, nt, fuse):
    if fuse:
        (xm_ref, xp_ref, xn_ref, cw_ref, cb_ref, wa_ref, ba_ref, wi_ref, bi_ref, lam_ref, h0_ref,
         yo_ref, gb_ref, out_ref, hl_ref, carry_ref) = refs
    else:
        (xm_ref, xp_ref, xn_ref, cw_ref, cb_ref, wa_ref, ba_ref, wi_ref, bi_ref, lam_ref, h0_ref,
         out_ref, hl_ref, carry_ref) = refs
    t = pl.program_id(1)
    tt = (nt - 1 - t) if reverse else t

    @pl.when(t == 0)
    def _():
        carry_ref[...] = h0_ref[...]

    xm = xm_ref[...]
    prev = jnp.where(tt > 0, xp_ref[...], 0.0)
    nxt = jnp.where(tt < nt - 1, xn_ref[...], 0.0)
    row8 = lax.broadcasted_iota(jnp.int32, prev.shape, 0)

    def tap(offset):
        if offset < 0:
            s = -offset
            y = pltpu.roll(xm, s, axis=0)
            head = jnp.where(row8 < s, pltpu.roll(prev, s, axis=0), y[:_SUBLANES, :])
            return jnp.concatenate([head, y[_SUBLANES:, :]], axis=0)
        y = pltpu.roll(xm, tl - offset, axis=0)
        tail = jnp.where(row8 >= _SUBLANES - offset, pltpu.roll(nxt, _SUBLANES - offset, axis=0),
                         y[tl - _SUBLANES:, :])
        return jnp.concatenate([y[:tl - _SUBLANES, :], tail], axis=0)

    cw = cw_ref[...]
    left = _CONV_W // 2
    xc = cb_ref[...]
    for k in range(_CONV_W):
        xk = xm if k == left else tap(k - left)
        xc = xc + xk * cw[k:k + 1, :]

    xcb = xc.astype(BF16)
    nb = wa_ref.shape[0]
    blocks = [slice(j * _LANES, (j + 1) * _LANES) for j in range(nb)]
    pre_r = jnp.concatenate([_dot(xcb[:, s], wa_ref[j]) for j, s in enumerate(blocks)], axis=1)
    pre_i = jnp.concatenate([_dot(xcb[:, s], wi_ref[j]) for j, s in enumerate(blocks)], axis=1)
    r = _sigmoid(pre_r + ba_ref[...])
    i = _sigmoid(pre_i + bi_ref[...])
    log_a = (-_LRU_C * r) * _softplus(-lam_ref[...])
    a = jnp.exp(log_a)
    mult = jnp.sqrt(-_expm1_neg(2.0 * log_a, a * a))
    b = mult * (i * xc)

    hs, new_carry = _scan_rows(a, b, carry_ref[...], reverse)
    carry_ref[...] = new_carry
    hl_ref[...] = new_carry
    if fuse:
        gate = jax.nn.gelu(gb_ref[...], approximate=True)
        out_ref[...] = ((hs + yo_ref[...]) * gate).astype(out_ref.dtype)
    else:
        out_ref[...] = hs


def _lru_call(xg, conv_w, conv_b, w_a, b_a, w_i, b_i, lam, h0, y_other, *, reverse):
    b, l, c2 = xg.shape
    c = c2 // 2
    nb = c // _LANES
    tl = _tile(l, 256)
    nt = l // tl
    hb = tl // _SUBLANES
    nh = l // _SUBLANES
    fuse = y_other is not None

    def tmap(t):
        return (nt - 1 - t) if reverse else t

    vec = lambda: pl.BlockSpec((1, c), lambda i, t: (0, 0))
    gate_w = lambda: pl.BlockSpec((nb, _LANES, _LANES), lambda i, t: (0, 0, 0))
    in_specs = [
        pl.BlockSpec((None, tl, c), lambda i, t: (i, tmap(t), 0)),
        pl.BlockSpec((None, _SUBLANES, c), lambda i, t: (i, jnp.maximum(tmap(t) * hb - 1, 0), 0)),
        pl.BlockSpec((None, _SUBLANES, c), lambda i, t: (i, jnp.minimum((tmap(t) + 1) * hb, nh - 1), 0)),
        pl.BlockSpec((_CONV_W, c), lambda i, t: (0, 0)),
        vec(),
        gate_w(),
        vec(),
        gate_w(),
        vec(),
        vec(),
        pl.BlockSpec((None, nb, _LANES), lambda i, t: (i, 0, 0)),
    ]
    args = [xg, xg, xg, conv_w, conv_b.reshape(1, c), w_a, b_a.reshape(1, c), w_i, b_i.reshape(1, c),
            lam.reshape(1, c), h0]
    if fuse:
        in_specs += [
            pl.BlockSpec((None, tl, c), lambda i, t: (i, tmap(t), 0)),
            pl.BlockSpec((None, tl, c), lambda i, t: (i, tmap(t), 1)),
        ]
        args += [y_other, xg]
    out_dtype = BF16 if fuse else F32
    return pl.pallas_call(
        functools.partial(_lru_kernel, reverse=reverse, tl=tl, nt=nt, fuse=fuse),
        grid=(b, nt),
        in_specs=in_specs,
        out_specs=[
            pl.BlockSpec((None, tl, c), lambda i, t: (i, tmap(t), 0)),
            pl.BlockSpec((None, nb, _LANES), lambda i, t: (i, 0, 0)),
        ],
        out_shape=[jax.ShapeDtypeStruct((b, l, c), out_dtype), jax.ShapeDtypeStruct((b, nb, _LANES), F32)],
        scratch_shapes=[pltpu.VMEM((nb, _LANES), F32)],
        compiler_params=_params(("parallel", "arbitrary"), 48),
        name="lru_rev" if reverse else "lru_fwd",
    )(*args)


def _proj_res_kernel(z_ref, w_ref, x_ref, mod_ref, o_ref, *, gi):
    o_ref[...] = x_ref[...] + mod_ref[gi:gi + 1, :] * _dot(z_ref[...], w_ref[...])


def _proj_res_call(z, w, x, mod, gi):
    b, l, d = x.shape
    k = z.shape[2]
    tm = _tile(l, 1024)
    return pl.pallas_call(
        functools.partial(_proj_res_kernel, gi=gi),
        grid=(b, l // tm),
        in_specs=[
            pl.BlockSpec((None, tm, k), lambda i, t: (i, t, 0)),
            pl.BlockSpec((k, d), lambda i, t: (0, 0)),
            pl.BlockSpec((None, tm, d), lambda i, t: (i, t, 0)),
            pl.BlockSpec((None, 6, d), lambda i, t: (i, 0, 0)),
        ],
        out_specs=pl.BlockSpec((None, tm, d), lambda i, t: (i, t, 0)),
        out_shape=jax.ShapeDtypeStruct((b, l, d), F32),
        compiler_params=_params(("parallel", "parallel"), 48),
        name="proj_residual",
    )(z, w, x, mod)


def _ffn_kernel(x_ref, g_ref, mod_ref, w1_ref, w3_ref, w2_ref, o_ref):
    x = x_ref[...]
    h = _rms_mod(x, g_ref[...], mod_ref[3:4, :], mod_ref[4:5, :]).astype(BF16)
    h1 = _dot(h, w1_ref[...])
    h3 = _dot(h, w3_ref[...])
    hid = (h1 * _sigmoid(h1)) * h3
    o_ref[...] = x + mod_ref[5:6, :] * _dot(hid.astype(BF16), w2_ref[...])


def _ffn_call(x, g, mod, w1, w3, w2):
    b, l, d = x.shape
    ff = w1.shape[1]
    tm = _tile(l, 512)
    resident = pl.Buffered(1)
    return pl.pallas_call(
        _ffn_kernel,
        grid=(b, l // tm),
        in_specs=[
            pl.BlockSpec((None, tm, d), lambda i, t: (i, t, 0)),
            pl.BlockSpec((1, d), lambda i, t: (0, 0)),
            pl.BlockSpec((None, 6, d), lambda i, t: (i, 0, 0)),
            pl.BlockSpec((d, ff), lambda i, t: (0, 0), pipeline_mode=resident),
            pl.BlockSpec((d, ff), lambda i, t: (0, 0), pipeline_mode=resident),
            pl.BlockSpec((ff, d), lambda i, t: (0, 0), pipeline_mode=resident),
        ],
        out_specs=pl.BlockSpec((None, tm, d), lambda i, t: (i, t, 0)),
        out_shape=jax.ShapeDtypeStruct((b, l, d), F32),
        compiler_params=_params(("parallel", "parallel"), 56),
        name="dense_swiglu",
    )(x, g.reshape(1, d), mod, w1, w3, w2)


def _rope_tables(l):
    pos = jnp.arange(l)
    row = (pos // _GRID_W).astype(F32)
    col = (pos % _GRID_W).astype(F32)
    freqs = _ROPE_THETA ** (-jnp.arange(_ROPE_NFREQ, dtype=F32) / _ROPE_NFREQ)
    ar = row[:, None] * freqs
    ac = col[:, None] * freqs
    cos = jnp.concatenate([jnp.cos(ar), jnp.cos(ar), jnp.cos(ac), jnp.cos(ac)], axis=-1)
    sin = jnp.concatenate([-jnp.sin(ar), jnp.sin(ar), -jnp.sin(ac), jnp.sin(ac)], axis=-1)
    return cos, sin


def _qkv_kernel(x_ref, g_ref, mod_ref, w_ref, qg_ref, kg_ref, cos_ref, sin_ref, q_ref, k_ref, v_ref,
                *, n_q, n_kv, q_scale):
    h = _rms_mod(x_ref[...], g_ref[...], mod_ref[0:1, :], mod_ref[1:2, :])
    qkv = _dot(h.astype(BF16), w_ref[...])
    cos = cos_ref[...]
    sin = sin_ref[...]
    lane = lax.broadcasted_iota(jnp.int32, cos.shape, 1)
    first = (lane % (2 * _ROPE_NFREQ)) < _ROPE_NFREQ

    def norm_rope(v, gain):
        y = v * lax.rsqrt(jnp.mean(v * v, axis=-1, keepdims=True) + _EPS) * gain
        partner = jnp.where(first, pltpu.roll(y, _HEAD_DIM - _ROPE_NFREQ, axis=1),
                            pltpu.roll(y, _ROPE_NFREQ, axis=1))
        return y * cos + partner * sin

    for hh in range(n_q):
        sl = slice(hh * _HEAD_DIM, (hh + 1) * _HEAD_DIM)
        q_ref[:, sl] = (norm_rope(qkv[:, sl], qg_ref[...]) * q_scale).astype(q_ref.dtype)
    for hh in range(n_kv):
        src = slice((n_q + hh) * _HEAD_DIM, (n_q + hh + 1) * _HEAD_DIM)
        k_ref[hh] = norm_rope(qkv[:, src], kg_ref[...]).T.astype(k_ref.dtype)
    ones_col = jnp.where(lane == 0, 1.0, 0.0).astype(v_ref.dtype)
    for hh in range(n_kv):
        src = slice((n_q + n_kv + hh) * _HEAD_DIM, (n_q + n_kv + hh + 1) * _HEAD_DIM)
        v_ref[:, 2 * hh * _HEAD_DIM:(2 * hh + 1) * _HEAD_DIM] = qkv[:, src].astype(v_ref.dtype)
        v_ref[:, (2 * hh + 1) * _HEAD_DIM:(2 * hh + 2) * _HEAD_DIM] = ones_col


_Q_SCALE = (_HEAD_DIM ** -0.5) * _LOG2E


def _qkv_call(x, g, mod, w, q_g, k_g, cos, sin, n_q, n_kv):
    b, l, d = x.shape
    n = w.shape[1]
    tm = _tile(l, 512)
    nq = n_q * _HEAD_DIM
    nkv = n_kv * _HEAD_DIM
    q_scale = _Q_SCALE
    return pl.pallas_call(
        functools.partial(_qkv_kernel, n_q=n_q, n_kv=n_kv, q_scale=q_scale),
        grid=(b, l // tm),
        in_specs=[
            pl.BlockSpec((None, tm, d), lambda i, t: (i, t, 0)),
            pl.BlockSpec((1, d), lambda i, t: (0, 0)),
            pl.BlockSpec((None, 6, d), lambda i, t: (i, 0, 0)),
            pl.BlockSpec((d, n), lambda i, t: (0, 0)),
            pl.BlockSpec((1, _HEAD_DIM), lambda i, t: (0, 0)),
            pl.BlockSpec((1, _HEAD_DIM), lambda i, t: (0, 0)),
            pl.BlockSpec((tm, _HEAD_DIM), lambda i, t: (t, 0)),
            pl.BlockSpec((tm, _HEAD_DIM), lambda i, t: (t, 0)),
        ],
        out_specs=[
            pl.BlockSpec((None, tm, nq), lambda i, t: (i, t, 0)),
            pl.BlockSpec((None, n_kv, _HEAD_DIM, tm), lambda i, t: (i, 0, 0, t)),
            pl.BlockSpec((None, tm, 2 * nkv), lambda i, t: (i, t, 0)),
        ],
        out_shape=[
            jax.ShapeDtypeStruct((b, l, nq), BF16),
            jax.ShapeDtypeStruct((b, n_kv, _HEAD_DIM, l), BF16),
            jax.ShapeDtypeStruct((b, l, 2 * nkv), BF16),
        ],
        compiler_params=_params(("parallel", "parallel"), 48),
        name="qkv_norm_rope",
    )(x, g.reshape(1, d), mod, w, q_g.reshape(1, _HEAD_DIM), k_g.reshape(1, _HEAD_DIM), cos, sin)


def _lane_tile(x, reps):
    return x if reps == 1 else jnp.concatenate([x] * reps, axis=1)


def _attn_kernel(q_ref, k_ref, v_ref, o_ref, m_ref, l_ref, acc_ref, *, tq, tk, groups, n_chunks):
    q = jnp.concatenate([q_ref[:, g * _HEAD_DIM:(g + 1) * _HEAD_DIM] for g in range(groups)], axis=0)
    m_ref[...] = jnp.full_like(m_ref, -jnp.inf)
    l_ref[...] = jnp.zeros_like(l_ref)
    acc_ref[...] = jnp.zeros_like(acc_ref)

    def body(c, carry):
        start = pl.multiple_of(c * tk, tk)
        k = k_ref[pl.ds(start, tk), :]
        v = v_ref[pl.ds(start, tk), :]
        s = lax.dot_general(q, k, (((1,), (1,)), ((), ())), preferred_element_type=F32)
        m_prev = m_ref[...]
        m_next = jnp.maximum(m_prev, jnp.max(s, axis=1, keepdims=True))
        alpha = jnp.exp2(m_prev - m_next)
        p = jnp.exp2(s - _lane_tile(m_next, tk // _LANES))
        l_ref[...] = alpha * l_ref[...] + jnp.sum(p, axis=1, keepdims=True)
        acc_ref[...] = alpha * acc_ref[...] + _dot(p.astype(BF16), v)
        m_ref[...] = m_next
        return carry

    lax.fori_loop(0, n_chunks, body, 0)
    o = acc_ref[...] * (1.0 / l_ref[...])
    for g in range(groups):
        o_ref[:, g * _HEAD_DIM:(g + 1) * _HEAD_DIM] = o[g * tq:(g + 1) * tq, :].astype(o_ref.dtype)


def _attn_bounded_kernel(q_ref, kt_ref, v_ref, o_ref, *, tq, tk, groups, n_chunks):
    q = jnp.concatenate([q_ref[:, g * _HEAD_DIM:(g + 1) * _HEAD_DIM] for g in range(groups)], axis=0)
    acc = None
    for c in range(n_chunks):
        p = jnp.exp2(_dot(q, kt_ref[:, c * tk:(c + 1) * tk])).astype(BF16)
        pv = _dot(p, v_ref[c * tk:(c + 1) * tk, :])
        acc = pv if acc is None else acc + pv
    o = acc[:, :_HEAD_DIM] * (1.0 / acc[:, _HEAD_DIM:_HEAD_DIM + 1])
    for g in range(groups):
        o_ref[:, g * _HEAD_DIM:(g + 1) * _HEAD_DIM] = o[g * tq:(g + 1) * tq, :].astype(o_ref.dtype)


_SCORE_BOUND = 60.0
_KV_CHUNKS = (256, 128)


def _attn_bounded_call(q, kt, vx, n_kv):
    b, l, nq = q.shape
    s = kt.shape[3]
    groups = nq // (n_kv * _HEAD_DIM)
    gw = groups * _HEAD_DIM
    tq = _tile(l, 256)
    tk = next(c for c in _KV_CHUNKS if s % c == 0)
    n_chunks = s // tk
    vw = 2 * _HEAD_DIM
    return pl.pallas_call(
        functools.partial(_attn_bounded_kernel, tq=tq, tk=tk, groups=groups, n_chunks=n_chunks),
        grid=(b, n_kv, l // tq),
        in_specs=[
            pl.BlockSpec((None, tq, gw), lambda i, j, t: (i, t, j)),
            pl.BlockSpec((None, None, _HEAD_DIM, s), lambda i, j, t: (i, j, 0, 0)),
            pl.BlockSpec((None, s, vw), lambda i, j, t: (i, 0, j)),
        ],
        out_specs=pl.BlockSpec((None, tq, gw), lambda i, j, t: (i, t, j)),
        out_shape=jax.ShapeDtypeStruct((b, l, nq), BF16),
        compiler_params=_params(("parallel", "parallel", "arbitrary"), 56),
        name="gqa_attention_bounded",
    )(q, kt, vx)


def _attn_call(q, kt, vx, n_kv):
    b, l, nq = q.shape
    s = kt.shape[3]
    k = kt.transpose(0, 3, 1, 2).reshape(b, s, n_kv * _HEAD_DIM)
    v = vx.reshape(b, s, n_kv, 2 * _HEAD_DIM)[..., :_HEAD_DIM].reshape(b, s, n_kv * _HEAD_DIM)
    groups = nq // (n_kv * _HEAD_DIM)
    gw = groups * _HEAD_DIM
    tq = _tile(l, 256)
    tk = 768 if s % 768 == 0 else _tile(s, 512)
    n_chunks = s // tk
    m_rows = groups * tq
    return pl.pallas_call(
        functools.partial(_attn_kernel, tq=tq, tk=tk, groups=groups, n_chunks=n_chunks),
        grid=(b, n_kv, l // tq),
        in_specs=[
            pl.BlockSpec((None, tq, gw), lambda i, j, t: (i, t, j)),
            pl.BlockSpec((None, s, _HEAD_DIM), lambda i, j, t: (i, 0, j)),
            pl.BlockSpec((None, s, _HEAD_DIM), lambda i, j, t: (i, 0, j)),
        ],
        out_specs=pl.BlockSpec((None, tq, gw), lambda i, j, t: (i, t, j)),
        out_shape=jax.ShapeDtypeStruct((b, l, nq), BF16),
        scratch_shapes=[
            pltpu.VMEM((m_rows, _LANES), F32),
            pltpu.VMEM((m_rows, _LANES), F32),
            pltpu.VMEM((m_rows, _HEAD_DIM), F32),
        ],
        compiler_params=_params(("parallel", "parallel", "arbitrary"), 48),
        name="gqa_flash_attention",
    )(q, k, v)


_INFO_E0, _INFO_E1, _INFO_R0, _INFO_R1, _INFO_G0, _INFO_G1 = range(6)


def _route_kernel(x_ref, g_ref, mod_ref, rw_ref, h_ref, info_ref, slots_ref, cnt_ref, carry_ref, tri_ref,
                  *, n_exp, tm):
    step = pl.program_id(0)

    @pl.when(step == 0)
    def _():
        carry_ref[...] = jnp.zeros_like(carry_ref)
        r = lax.broadcasted_iota(jnp.int32, (tm, tm), 0)
        cidx = lax.broadcasted_iota(jnp.int32, (tm, tm), 1)
        tri_ref[...] = jnp.where(r > cidx, 1.0, 0.0).astype(BF16)

    h = _rms_mod(x_ref[...], g_ref[...], mod_ref[3:4, :], mod_ref[4:5, :])
    h_ref[...] = pltpu.einshape("t(jc)->tjc", h, j=h_ref.shape[1])
    logits = _dot3(h, rw_ref[...])
    lane = lax.broadcasted_iota(jnp.int32, logits.shape, 1)
    neg = jnp.float32(-jnp.inf)
    lg = jnp.where(lane < n_exp, logits, neg)
    m0 = jnp.max(lg, axis=1, keepdims=True)
    i0 = jnp.min(jnp.where(lg == m0, lane, _LANES), axis=1, keepdims=True)
    oh0 = lane == i0
    lg1 = jnp.where(oh0, neg, lg)
    m1 = jnp.max(lg1, axis=1, keepdims=True)
    i1 = jnp.min(jnp.where(lg1 == m1, lane, _LANES), axis=1, keepdims=True)
    oh1 = lane == i1
    e = jnp.exp(m1 - m0)
    g0 = 1.0 / (1.0 + e)
    g1 = e * g0

    chosen = jnp.where(oh0 | oh1, 1.0, 0.0)
    before = _dot(tri_ref[...], chosen.astype(BF16)) + carry_ref[...]
    r0 = jnp.sum(jnp.where(oh0, before, 0.0), axis=1, keepdims=True)
    r1 = jnp.sum(jnp.where(oh1, before, 0.0), axis=1, keepdims=True)
    total = carry_ref[...] + jnp.sum(chosen, axis=0, keepdims=True)
    carry_ref[...] = total
    cnt_ref[...] = total

    info = jnp.zeros(logits.shape, F32)
    for idx, val in ((_INFO_E0, i0.astype(F32)), (_INFO_E1, i1.astype(F32)), (_INFO_R0, r0), (_INFO_R1, r1),
                     (_INFO_G0, g0), (_INFO_G1, g1)):
        info = jnp.where(lane == idx, val, info)
    info_ref[...] = info
    slots_ref[...] = info.T[:_SUBLANES, :].astype(jnp.int32)


def _route_call(x, g, mod, router):
    b, l, d = x.shape
    n_exp = router.shape[1]
    t = b * l
    tm = _tile(l, 512)
    per_b = l // tm
    rw = jnp.zeros((d, _LANES), F32).at[:, :n_exp].set(router)
    return pl.pallas_call(
        functools.partial(_route_kernel, n_exp=n_exp, tm=tm),
        grid=(t // tm,),
        in_specs=[
            pl.BlockSpec((tm, d), lambda i: (i, 0)),
            pl.BlockSpec((1, d), lambda i: (0, 0)),
            pl.BlockSpec((None, 6, d), lambda i: (i // per_b, 0, 0)),
            pl.BlockSpec((d, _LANES), lambda i: (0, 0)),
        ],
        out_specs=[
            pl.BlockSpec((tm, d // _LANES, _LANES), lambda i: (i, 0, 0)),
            pl.BlockSpec((tm, _LANES), lambda i: (i, 0)),
            pl.BlockSpec((_SUBLANES, tm), lambda i: (0, i)),
            pl.BlockSpec((1, _LANES), lambda i: (0, 0)),
        ],
        out_shape=[
            jax.ShapeDtypeStruct((t, d // _LANES, _LANES), F32),
            jax.ShapeDtypeStruct((t, _LANES), F32),
            jax.ShapeDtypeStruct((_SUBLANES, t), jnp.int32),
            jax.ShapeDtypeStruct((1, _LANES), F32),
        ],
        scratch_shapes=[pltpu.VMEM((1, _LANES), F32), pltpu.VMEM((tm, tm), BF16)],
        compiler_params=_params(("arbitrary",), 40),
        name="moe_route",
    )(x.reshape(t, d), g.reshape(1, d), mod, rw)


def _pos_tiles(pos0, pos1, tt):
    n = pos0.shape[0] // tt
    return jnp.stack([pos0.reshape(n, tt), pos1.reshape(n, tt)], axis=1).reshape(n * 2 * tt)


def _row_copy(src, src_row, dst, dst_row, sem):
    return pltpu.make_async_copy(src.at[pl.ds(src_row, 1)], dst.at[pl.ds(dst_row, 1)], sem)


_ROW_DMA_UNROLL = 8


def _dispatch_kernel(pos_ref, h_ref, xs_in_ref, xs_ref, pos_smem, sem_s, sem_d, *, tt):
    del xs_in_ref
    cp = pltpu.make_async_copy(pos_ref, pos_smem, sem_s)
    cp.start()
    cp.wait()

    def body(r, carry):
        _row_copy(h_ref, r, xs_ref, pos_smem[r], sem_d).start(priority=0)
        _row_copy(h_ref, r, xs_ref, pos_smem[tt + r], sem_d).start(priority=1)
        return carry

    lax.fori_loop(0, tt, body, 0, unroll=_ROW_DMA_UNROLL)
    for _ in range(2 * tt):
        _row_copy(h_ref, 0, xs_ref, 0, sem_d).wait()


def _dispatch_call(h, pos_t, p_rows, tt):
    t, nb, _ = h.shape
    xs0 = jnp.zeros((p_rows, nb, _LANES), F32)
    return pl.pallas_call(
        functools.partial(_dispatch_kernel, tt=tt),
        grid=(t // tt,),
        in_specs=[
            pl.BlockSpec((2 * tt,), lambda i: (i,)),
            pl.BlockSpec((tt, nb, _LANES), lambda i: (i, 0, 0)),
            pl.BlockSpec(memory_space=pl.ANY),
        ],
        out_specs=pl.BlockSpec(memory_space=pl.ANY),
        out_shape=jax.ShapeDtypeStruct((p_rows, nb, _LANES), F32),
        scratch_shapes=[
            pltpu.SMEM((2 * tt,), jnp.int32),
            pltpu.SemaphoreType.DMA(()),
            pltpu.SemaphoreType.DMA(()),
        ],
        input_output_aliases={2: 0},
        compiler_params=_params(("arbitrary",), 32),
        name="moe_dispatch",
    )(pos_t, h, xs0)


def _moe_kernel(be_ref, bs_ref, bv_ref, x_ref, w1_ref, w3_ref, w2_ref, o_ref, xb_ref, acc_ref, *, nf):
    del be_ref, bs_ref
    i = pl.program_id(0)
    f = pl.program_id(1)

    @pl.when(bv_ref[i] > 0)
    def _():
        @pl.when(f == 0)
        def _():
            xb_ref[...] = pltpu.einshape("tjc->t(jc)", x_ref[...]).astype(BF16)
            acc_ref[...] = jnp.zeros_like(acc_ref)

        xb = xb_ref[...]
        h1 = _dot(xb, w1_ref[...])
        h3 = _dot(xb, w3_ref[...])
        hid = (h1 * _sigmoid(h1)) * h3
        acc_ref[...] += _dot(hid.astype(BF16), w2_ref[...])

        @pl.when(f == nf - 1)
        def _():
            o_ref[...] = pltpu.einshape("t(jc)->tjc", acc_ref[...], j=o_ref.shape[1])


def _moe_call(blk_e, blk_src, blk_valid, xs, w1, w3, w2, tmm):
    p_rows, nb, _ = xs.shape
    d = nb * _LANES
    ff = w1.shape[2]
    tf = 1792 if ff % 1792 == 0 else _tile(ff, 512)
    nf = ff // tf
    nblk = p_rows // tmm

    def fsel(i, f, bv):
        return jnp.where(bv[i] > 0, f, nf - 1)

    grid_spec = pltpu.PrefetchScalarGridSpec(
        num_scalar_prefetch=3,
        grid=(nblk, nf),
        in_specs=[
            pl.BlockSpec((tmm, nb, _LANES), lambda i, f, be, bs, bv: (bs[i], 0, 0)),
            pl.BlockSpec((None, d, tf), lambda i, f, be, bs, bv: (be[i], 0, fsel(i, f, bv))),
            pl.BlockSpec((None, d, tf), lambda i, f, be, bs, bv: (be[i], 0, fsel(i, f, bv))),
            pl.BlockSpec((None, tf, d), lambda i, f, be, bs, bv: (be[i], fsel(i, f, bv), 0)),
        ],
        out_specs=pl.BlockSpec((tmm, nb, _LANES), lambda i, f, be, bs, bv: (bs[i], 0, 0)),
        scratch_shapes=[pltpu.VMEM((tmm, d), BF16), pltpu.VMEM((tmm, d), F32)],
    )
    return pl.pallas_call(
        functools.partial(_moe_kernel, nf=nf),
        grid_spec=grid_spec,
        out_shape=jax.ShapeDtypeStruct((p_rows, nb, _LANES), F32),
        compiler_params=_params(("arbitrary", "arbitrary"), 58),
        name="moe_experts",
    )(blk_e, blk_src, blk_valid, xs, w1, w3, w2)


def _combine_kernel(pos_ref, pos_next_ref, x_ref, mod_ref, info_ref, fg_ref, y_ref, o_ref, pos_smem, ybuf,
                    sem_s, sem_d, *, tt, n_steps):
    i = pl.program_id(0)
    slot = lax.rem(i, 2)

    def start_gathers(p_ref, s):
        cp = pltpu.make_async_copy(p_ref, pos_smem, sem_s)
        cp.start()
        cp.wait()

        def body(r, carry):
            _row_copy(y_ref, pos_smem[r], ybuf.at[s, 0], r, sem_d.at[s]).start(priority=0)
            _row_copy(y_ref, pos_smem[tt + r], ybuf.at[s, 1], r, sem_d.at[s]).start(priority=1)
            return carry

        lax.fori_loop(0, tt, body, 0, unroll=_ROW_DMA_UNROLL)

    @pl.when(i == 0)
    def _():
        start_gathers(pos_ref, 0)

    @pl.when(i + 1 < n_steps)
    def _():
        start_gathers(pos_next_ref, 1 - slot)

    for _ in range(2 * tt):
        _row_copy(y_ref, 0, ybuf.at[slot, 0], 0, sem_d.at[slot]).wait()

    info = info_ref[...]
    g0 = info[:, _INFO_G0:_INFO_G0 + 1]
    g1 = info[:, _INFO_G1:_INFO_G1 + 1]
    y0 = pltpu.einshape("tjc->t(jc)", ybuf[slot, 0])
    y1 = pltpu.einshape("tjc->t(jc)", ybuf[slot, 1])
    ffn = g0 * y0 + g1 * y1
    xo = x_ref[...] + mod_ref[5:6, :] * ffn
    o_ref[...] = xo * lax.rsqrt(jnp.mean(xo * xo, axis=-1, keepdims=True) + _EPS) * fg_ref[...]


def _combine_call(pos_t, x, mod, info, final_g, y, tt):
    b, l, d = x.shape
    t = b * l
    per_b = l // tt
    n_steps = t // tt
    return pl.pallas_call(
        functools.partial(_combine_kernel, tt=tt, n_steps=n_steps),
        grid=(n_steps,),
        in_specs=[
            pl.BlockSpec((2 * tt,), lambda i: (i,)),
            pl.BlockSpec((2 * tt,), lambda i: (jnp.minimum(i + 1, n_steps - 1),)),
            pl.BlockSpec((tt, d), lambda i: (i, 0)),
            pl.BlockSpec((None, 6, d), lambda i: (i // per_b, 0, 0)),
            pl.BlockSpec((tt, _LANES), lambda i: (i, 0)),
            pl.BlockSpec((1, d), lambda i: (0, 0)),
            pl.BlockSpec(memory_space=pl.ANY),
        ],
        out_specs=pl.BlockSpec((tt, d), lambda i: (i, 0)),
        out_shape=jax.ShapeDtypeStruct((t, d), F32),
        scratch_shapes=[
            pltpu.SMEM((2 * tt,), jnp.int32),
            pltpu.VMEM((2, 2, tt, d // _LANES, _LANES), F32),
            pltpu.SemaphoreType.DMA(()),
            pltpu.SemaphoreType.DMA((2,)),
        ],
        compiler_params=_params(("arbitrary",), 40),
        name="moe_combine_norm",
    )(pos_t, pos_t, x.reshape(t, d), mod, info, final_g.reshape(1, d), y).reshape(b, l, d)


def _moe_layer(x, g, mod, router, w1, w3, w2, final_g):
    b, l, d = x.shape
    t = b * l
    n_exp = router.shape[1]
    tmm = 512 if t >= 4096 else 128
    tt = _tile(l, 512)
    h, info, slots, counts = _route_call(x, g, mod, router)

    counts = counts[0, :n_exp].astype(jnp.int32)
    padded = ((counts + tmm - 1) // tmm) * tmm
    pend = jnp.cumsum(padded)
    gstart = pend - padded
    pos0 = gstart[slots[_INFO_E0]] + slots[_INFO_R0]
    pos1 = gstart[slots[_INFO_E1]] + slots[_INFO_R1]
    pos_t = _pos_tiles(pos0, pos1, tt)

    nblk = (t * 2) // tmm + n_exp
    p_rows = nblk * tmm
    bstart = jnp.arange(nblk, dtype=jnp.int32) * tmm
    used = bstart < pend[-1]
    last = pend[-1] // tmm - 1
    bidx = jnp.where(used, jnp.arange(nblk, dtype=jnp.int32), last)
    blk_e = jnp.minimum(jnp.sum(((bidx * tmm)[:, None] >= pend[None, :]).astype(jnp.int32), axis=1), n_exp - 1)
    blk_valid = jnp.where(used, jnp.clip(counts[blk_e] - (bidx * tmm - gstart[blk_e]), 0, tmm), 0).astype(jnp.int32)

    xs = _dispatch_call(h, pos_t, p_rows, tt)
    y = _moe_call(blk_e, bidx.astype(jnp.int32), blk_valid, xs, w1.astype(BF16), w3.astype(BF16), w2.astype(BF16),
                  tmm)
    return _combine_call(pos_t, x, mod, info, final_g, y, tt)


def kernel(x, c, ctx, c_ctx, ada_w, ada_b, norm1_g, norm2_g, rg_w_in, rg_conv_w, rg_conv_b, rg_w_a, rg_b_a,
           rg_w_i, rg_b_i, rg_lam, rg_w_out, attn_w_qkv, attn_q_g, attn_k_g, attn_w_o, ffn_w1, ffn_w3, ffn_w2,
           moe_router, moe_w1, moe_w3, moe_w2, final_g):
    b, l, d = x.shape
    n_ctx = ctx.shape[1]
    c_dim = rg_w_out.shape[1]
    n_q = attn_w_o.shape[1] // _HEAD_DIM
    n_kv = (attn_w_qkv.shape[2] // _HEAD_DIM - n_q) // 2

    rows = 16
    c_rows = jnp.zeros((rows, d), F32).at[:b].set(c).at[b].set(c_ctx)
    mod = _ada_call(c_rows, ada_w, ada_b).reshape(ada_w.shape[0], rows, 6, d)
    mod_l = [mod[i, :b] for i in range(2)]
    mod_c = [jnp.broadcast_to(mod[i, b], (b, 6, d)) for i in range(2)]

    w_in = rg_w_in[0].astype(BF16)
    w_out = rg_w_out[0].astype(BF16)
    w_a = rg_w_a[0].astype(BF16)
    w_i = rg_w_i[0].astype(BF16)
    xg_l = _nmm_call(x, norm1_g[0], mod_l[0], w_in, 0, 1, F32)
    xg_c = _nmm_call(ctx, norm1_g[0], mod_c[0], w_in, 0, 1, F32)
    zero_h = jnp.zeros((b, c_dim // _LANES, _LANES), F32)

    def lru(xg, h0, y_other, direction, reverse):
        return _lru_call(xg, rg_conv_w[0], rg_conv_b[0], w_a[direction], rg_b_a[0, direction], w_i[direction],
                         rg_b_i[0, direction], rg_lam[0, direction], h0, y_other, reverse=reverse)

    yc_rev, h0_rev = lru(xg_c, zero_h, None, 1, True)
    z_c, h0_fwd = lru(xg_c, zero_h, yc_rev, 0, False)
    yl_rev, _ = lru(xg_l, h0_rev, None, 1, True)
    z_l, _ = lru(xg_l, h0_fwd, yl_rev, 0, False)
    x = _proj_res_call(z_l, w_out, x, mod_l[0], 2)
    ctx = _proj_res_call(z_c, w_out, ctx, mod_c[0], 2)
    f1, f3, f2 = ffn_w1[0].astype(BF16), ffn_w3[0].astype(BF16), ffn_w2[0].astype(BF16)
    x = _ffn_call(x, norm2_g[0], mod_l[0], f1, f3, f2)
    ctx = _ffn_call(ctx, norm2_g[0], mod_c[0], f1, f3, f2)

    w_qkv = attn_w_qkv[0].astype(BF16)
    cos, sin = _rope_tables(l)
    q, k_l, v_l = _qkv_call(x, norm1_g[1], mod_l[1], w_qkv, attn_q_g[0], attn_k_g[0], cos, sin, n_q, n_kv)
    ones = jnp.ones((n_ctx, _HEAD_DIM), F32)
    _, k_c, v_c = _qkv_call(ctx, norm1_g[1], mod_c[1], w_qkv, attn_q_g[0], attn_k_g[0], ones, jnp.zeros_like(ones),
                            n_q, n_kv)
    k_all = jnp.concatenate([k_c, k_l], axis=3)
    v_all = jnp.concatenate([v_c, v_l], axis=1)
    score_bound = _HEAD_DIM * jnp.max(jnp.abs(attn_q_g[0])) * jnp.max(jnp.abs(attn_k_g[0])) * _Q_SCALE
    o = lax.cond(score_bound <= _SCORE_BOUND,
                 lambda: _attn_bounded_call(q, k_all, v_all, n_kv),
                 lambda: _attn_call(q, k_all, v_all, n_kv))
    x = _proj_res_call(o, attn_w_o[0].astype(BF16), x, mod_l[1], 2)
    return _moe_layer(x, norm2_g[1], mod_l[1], moe_router[0], moe_w1[0], moe_w3[0], moe_w2[0], final_g)
```

```python
import functools

import jax
import jax.numpy as jnp
from jax import lax
from jax.experimental import pallas as pl
from jax.experimental.pallas import tpu as pltpu

F32 = jnp.float32
BF16 = jnp.bfloat16

_EPS = 1e-6
_HEAD_DIM = 128
_GRID_W = 64
_ROPE_THETA = 10000.0
_ROPE_NFREQ = _HEAD_DIM // 4
_LRU_C = 8.0
_CONV_W = 4
_LANES = 128
_SUBLANES = 8
_LOG2E = 1.4426950408889634
_MIB = 1024 * 1024


def _params(semantics, vmem_mib):
    return pltpu.CompilerParams(dimension_semantics=semantics, vmem_limit_bytes=vmem_mib * _MIB)


def _tile(n, pref):
    if n <= pref:
        return n
    t = pref
    while n % t:
        t //= 2
    return t


def _dot(a, b):
    return jnp.dot(a, b, preferred_element_type=F32)


def _split_bf16(a):
    hi = a.astype(BF16)
    lo = (a - hi.astype(F32)).astype(BF16)
    return hi, lo


def _dot3(a, b):
    ah, al = _split_bf16(a)
    bh, bl = _split_bf16(b)
    return _dot(ah, bh) + (_dot(al, bh) + _dot(ah, bl))


def _rms_mod(x, g, shift, scale):
    y = x * lax.rsqrt(jnp.mean(x * x, axis=-1, keepdims=True) + _EPS)
    return (y * g) * (1.0 + scale) + shift


def _sigmoid(x):
    return 1.0 / (1.0 + jnp.exp(-x))


def _ada_kernel(c_ref, w_ref, b_ref, o_ref):
    c = c_ref[...]
    s = c * _sigmoid(c)
    o_ref[...] = _dot3(s, w_ref[...]) + b_ref[...]


def _ada_call(c_rows, ada_w, ada_b):
    depth, d, n = ada_w.shape
    rows = c_rows.shape[0]
    tn = _tile(n, 1536)
    return pl.pallas_call(
        _ada_kernel,
        grid=(depth, n // tn),
        in_specs=[
            pl.BlockSpec((rows, d), lambda i, j: (0, 0)),
            pl.BlockSpec((None, d, tn), lambda i, j: (i, 0, j)),
            pl.BlockSpec((None, 1, tn), lambda i, j: (i, 0, j)),
        ],
        out_specs=pl.BlockSpec((None, rows, tn), lambda i, j: (i, 0, j)),
        out_shape=jax.ShapeDtypeStruct((depth, rows, n), F32),
        compiler_params=_params(("parallel", "parallel"), 40),
        name="ada_mod",
    )(c_rows, ada_w, ada_b.reshape(depth, 1, n))


def _nmm_kernel(x_ref, g_ref, mod_ref, w_ref, o_ref, *, sh, sc):
    h = _rms_mod(x_ref[...], g_ref[...], mod_ref[sh:sh + 1, :], mod_ref[sc:sc + 1, :])
    o_ref[...] = _dot(h.astype(BF16), w_ref[...]).astype(o_ref.dtype)


def _nmm_call(x, g, mod, w, sh, sc, out_dtype):
    b, l, d = x.shape
    n = w.shape[1]
    tm = _tile(l, 512)
    return pl.pallas_call(
        functools.partial(_nmm_kernel, sh=sh, sc=sc),
        grid=(b, l // tm),
        in_specs=[
            pl.BlockSpec((None, tm, d), lambda i, t: (i, t, 0)),
            pl.BlockSpec((1, d), lambda i, t: (0, 0)),
            pl.BlockSpec((None, 6, d), lambda i, t: (i, 0, 0)),
            pl.BlockSpec((d, n), lambda i, t: (0, 0)),
        ],
        out_specs=pl.BlockSpec((None, tm, n), lambda i, t: (i, t, 0)),
        out_shape=jax.ShapeDtypeStruct((b, l, n), out_dtype),
        compiler_params=_params(("parallel", "parallel"), 48),
        name="norm_mod_matmul",
    )(x, g.reshape(1, d), mod, w)


def _log1p_pos(e):
    u = 1.0 + e
    den = jnp.where(u == 1.0, 1.0, u - 1.0)
    return jnp.where(u == 1.0, e, jnp.log(u) * (e * (1.0 / den)))


def _softplus(x):
    return jnp.maximum(x, 0.0) + _log1p_pos(jnp.exp(-jnp.abs(x)))


def _expm1_neg(x, u):
    lu = jnp.log(jnp.where(u == 0.0, 1.0, u))
    den = jnp.where(lu == 0.0, 1.0, lu)
    r = (u - 1.0) * (x * (1.0 / den))
    return jnp.where(u == 1.0, x, jnp.where(u == 0.0, -1.0, r))


def _scan_rows(a, b, h, reverse):
    tl = a.shape[0]
    nb = a.shape[1] // _LANES
    a3 = pltpu.einshape("t(jc)->tjc", a, j=nb)
    b3 = pltpu.einshape("t(jc)->tjc", b, j=nb)
    states = [None] * tl
    for t in (range(tl - 1, -1, -1) if reverse else range(tl)):
        h = a3[t] * h + b3[t]
        states[t] = h
    return pltpu.einshape("tjc->t(jc)", jnp.stack(states, axis=0)), h


def _lru_kernel(*refs, reverse, tl, nt, fuse):
    if fuse:
        (xm_ref, xp_ref, xn_ref, cw_ref, cb_ref, wa_ref, ba_ref, wi_ref, bi_ref, lam_ref, h0_ref,
         yo_ref, gb_ref, wo_ref, res_ref, mod_ref, out_ref, hl_ref, carry_ref) = refs
    else:
        (xm_ref, xp_ref, xn_ref, cw_ref, cb_ref, wa_ref, ba_ref, wi_ref, bi_ref, lam_ref, h0_ref,
         out_ref, hl_ref, carry_ref) = refs
    t = pl.program_id(1)
    tt = (nt - 1 - t) if reverse else t

    @pl.when(t == 0)
    def _():
        carry_ref[...] = h0_ref[...]

    xm = xm_ref[...]
    prev = jnp.where(tt > 0, xp_ref[...], 0.0)
    nxt = jnp.where(tt < nt - 1, xn_ref[...], 0.0)
    row8 = lax.broadcasted_iota(jnp.int32, prev.shape, 0)

    def tap(offset):
        if offset < 0:
            s = -offset
            y = pltpu.roll(xm, s, axis=0)
            head = jnp.where(row8 < s, pltpu.roll(prev, s, axis=0), y[:_SUBLANES, :])
            return jnp.concatenate([head, y[_SUBLANES:, :]], axis=0)
        y = pltpu.roll(xm, tl - offset, axis=0)
        tail = jnp.where(row8 >= _SUBLANES - offset, pltpu.roll(nxt, _SUBLANES - offset, axis=0),
                         y[tl - _SUBLANES:, :])
        return jnp.concatenate([y[:tl - _SUBLANES, :], tail], axis=0)

    cw = cw_ref[...]
    left = _CONV_W // 2
    xc = cb_ref[...]
    for k in range(_CONV_W):
        xk = xm if k == left else tap(k - left)
        xc = xc + xk * cw[k:k + 1, :]

    xcb = xc.astype(BF16)
    nb = wa_ref.shape[0]
    blocks = [slice(j * _LANES, (j + 1) * _LANES) for j in range(nb)]
    pre_r = jnp.concatenate([_dot(xcb[:, s], wa_ref[j]) for j, s in enumerate(blocks)], axis=1)
    pre_i = jnp.concatenate([_dot(xcb[:, s], wi_ref[j]) for j, s in enumerate(blocks)], axis=1)
    r = _sigmoid(pre_r + ba_ref[...])
    i = _sigmoid(pre_i + bi_ref[...])
    log_a = (-_LRU_C * r) * _softplus(-lam_ref[...])
    a = jnp.exp(log_a)
    mult = jnp.sqrt(-_expm1_neg(2.0 * log_a, a * a))
    b = mult * (i * xc)

    hs, new_carry = _scan_rows(a, b, carry_ref[...], reverse)
    carry_ref[...] = new_carry
    hl_ref[...] = new_carry
    if fuse:
        gate = jax.nn.gelu(gb_ref[...], approximate=True)
        z = ((hs + yo_ref[...]) * gate).astype(BF16)
        out_ref[...] = res_ref[...] + mod_ref[2:3, :] * _dot(z, wo_ref[...])
    else:
        out_ref[...] = hs


def _lru_call(xg, conv_w, conv_b, w_a, b_a, w_i, b_i, lam, h0, tail=None, *, reverse):
    b, l, c2 = xg.shape
    c = c2 // 2
    nb = c // _LANES
    tl = _tile(l, 256)
    nt = l // tl
    hb = tl // _SUBLANES
    nh = l // _SUBLANES
    fuse = tail is not None

    def tmap(t):
        return (nt - 1 - t) if reverse else t

    vec = lambda: pl.BlockSpec((1, c), lambda i, t: (0, 0))
    gate_w = lambda: pl.BlockSpec((nb, _LANES, _LANES), lambda i, t: (0, 0, 0))
    in_specs = [
        pl.BlockSpec((None, tl, c), lambda i, t: (i, tmap(t), 0)),
        pl.BlockSpec((None, _SUBLANES, c), lambda i, t: (i, jnp.maximum(tmap(t) * hb - 1, 0), 0)),
        pl.BlockSpec((None, _SUBLANES, c), lambda i, t: (i, jnp.minimum((tmap(t) + 1) * hb, nh - 1), 0)),
        pl.BlockSpec((_CONV_W, c), lambda i, t: (0, 0)),
        vec(),
        gate_w(),
        vec(),
        gate_w(),
        vec(),
        vec(),
        pl.BlockSpec((None, nb, _LANES), lambda i, t: (i, 0, 0)),
    ]
    args = [xg, xg, xg, conv_w, conv_b.reshape(1, c), w_a, b_a.reshape(1, c), w_i, b_i.reshape(1, c),
            lam.reshape(1, c), h0]
    d_out = c
    if fuse:
        y_other, w_out, res, mod = tail
        d_out = w_out.shape[1]
        in_specs += [
            pl.BlockSpec((None, tl, c), lambda i, t: (i, tmap(t), 0)),
            pl.BlockSpec((None, tl, c), lambda i, t: (i, tmap(t), 1)),
            pl.BlockSpec((c, d_out), lambda i, t: (0, 0), pipeline_mode=pl.Buffered(1)),
            pl.BlockSpec((None, tl, d_out), lambda i, t: (i, tmap(t), 0)),
            pl.BlockSpec((None, 6, d_out), lambda i, t: (i, 0, 0)),
        ]
        args += [y_other, xg, w_out, res, mod]
    return pl.pallas_call(
        functools.partial(_lru_kernel, reverse=reverse, tl=tl, nt=nt, fuse=fuse),
        grid=(b, nt),
        in_specs=in_specs,
        out_specs=[
            pl.BlockSpec((None, tl, d_out), lambda i, t: (i, tmap(t), 0)),
            pl.BlockSpec((None, nb, _LANES), lambda i, t: (i, 0, 0)),
        ],
        out_shape=[jax.ShapeDtypeStruct((b, l, d_out), F32), jax.ShapeDtypeStruct((b, nb, _LANES), F32)],
        scratch_shapes=[pltpu.VMEM((nb, _LANES), F32)],
        compiler_params=_params(("parallel", "arbitrary"), 48),
        name="lru_rev" if reverse else "lru_fwd",
    )(*args)


def _proj_res_kernel(z_ref, w_ref, x_ref, mod_ref, o_ref, *, gi):
    o_ref[...] = x_ref[...] + mod_ref[gi:gi + 1, :] * _dot(z_ref[...], w_ref[...])


def _proj_res_call(z, w, x, mod, gi):
    b, l, d = x.shape
    k = z.shape[2]
    tm = _tile(l, 1024)
    return pl.pallas_call(
        functools.partial(_proj_res_kernel, gi=gi),
        grid=(b, l // tm),
        in_specs=[
            pl.BlockSpec((None, tm, k), lambda i, t: (i, t, 0)),
            pl.BlockSpec((k, d), lambda i, t: (0, 0)),
            pl.BlockSpec((None, tm, d), lambda i, t: (i, t, 0)),
            pl.BlockSpec((None, 6, d), lambda i, t: (i, 0, 0)),
        ],
        out_specs=pl.BlockSpec((None, tm, d), lambda i, t: (i, t, 0)),
        out_shape=jax.ShapeDtypeStruct((b, l, d), F32),
        compiler_params=_params(("parallel", "parallel"), 48),
        name="proj_residual",
    )(z, w, x, mod)


def _ffn_kernel(x_ref, g_ref, mod_ref, w1_ref, w3_ref, w2_ref, o_ref):
    x = x_ref[...]
    h = _rms_mod(x, g_ref[...], mod_ref[3:4, :], mod_ref[4:5, :]).astype(BF16)
    h1 = _dot(h, w1_ref[...])
    h3 = _dot(h, w3_ref[...])
    hid = (h1 * _sigmoid(h1)) * h3
    o_ref[...] = x + mod_ref[5:6, :] * _dot(hid.astype(BF16), w2_ref[...])


def _ffn_call(x, g, mod, w1, w3, w2):
    b, l, d = x.shape
    ff = w1.shape[1]
    tm = _tile(l, 512)
    resident = pl.Buffered(1)
    return pl.pallas_call(
        _ffn_kernel,
        grid=(b, l // tm),
        in_specs=[
            pl.BlockSpec((None, tm, d), lambda i, t: (i, t, 0)),
            pl.BlockSpec((1, d), lambda i, t: (0, 0)),
            pl.BlockSpec((None, 6, d), lambda i, t: (i, 0, 0)),
            pl.BlockSpec((d, ff), lambda i, t: (0, 0), pipeline_mode=resident),
            pl.BlockSpec((d, ff), lambda i, t: (0, 0), pipeline_mode=resident),
            pl.BlockSpec((ff, d), lambda i, t: (0, 0), pipeline_mode=resident),
        ],
        out_specs=pl.BlockSpec((None, tm, d), lambda i, t: (i, t, 0)),
        out_shape=jax.ShapeDtypeStruct((b, l, d), F32),
        compiler_params=_params(("parallel", "parallel"), 56),
        name="dense_swiglu",
    )(x, g.reshape(1, d), mod, w1, w3, w2)


def _rope_tables(l):
    pos = jnp.arange(l)
    row = (pos // _GRID_W).astype(F32)
    col = (pos % _GRID_W).astype(F32)
    freqs = _ROPE_THETA ** (-jnp.arange(_ROPE_NFREQ, dtype=F32) / _ROPE_NFREQ)
    ar = row[:, None] * freqs
    ac = col[:, None] * freqs
    cos = jnp.concatenate([jnp.cos(ar), jnp.cos(ar), jnp.cos(ac), jnp.cos(ac)], axis=-1)
    sin = jnp.concatenate([-jnp.sin(ar), jnp.sin(ar), -jnp.sin(ac), jnp.sin(ac)], axis=-1)
    return cos, sin


def _qkv_kernel(x_ref, g_ref, mod_ref, w_ref, qg_ref, kg_ref, cos_ref, sin_ref, q_ref, k_ref, v_ref,
                *, n_q, n_kv, q_scale):
    h = _rms_mod(x_ref[...], g_ref[...], mod_ref[0:1, :], mod_ref[1:2, :])
    qkv = _dot(h.astype(BF16), w_ref[...])
    cos = cos_ref[...]
    sin = sin_ref[...]
    lane = lax.broadcasted_iota(jnp.int32, cos.shape, 1)
    first = (lane % (2 * _ROPE_NFREQ)) < _ROPE_NFREQ

    def norm_rope(v, gain):
        y = v * lax.rsqrt(jnp.mean(v * v, axis=-1, keepdims=True) + _EPS) * gain
        partner = jnp.where(first, pltpu.roll(y, _HEAD_DIM - _ROPE_NFREQ, axis=1),
                            pltpu.roll(y, _ROPE_NFREQ, axis=1))
        return y * cos + partner * sin

    for hh in range(n_q):
        sl = slice(hh * _HEAD_DIM, (hh + 1) * _HEAD_DIM)
        q_ref[:, sl] = (norm_rope(qkv[:, sl], qg_ref[...]) * q_scale).astype(q_ref.dtype)
    for hh in range(n_kv):
        src = slice((n_q + hh) * _HEAD_DIM, (n_q + hh + 1) * _HEAD_DIM)
        k_ref[hh] = norm_rope(qkv[:, src], kg_ref[...]).T.astype(k_ref.dtype)
    ones_col = jnp.where(lane == 0, 1.0, 0.0).astype(v_ref.dtype)
    for hh in range(n_kv):
        src = slice((n_q + n_kv + hh) * _HEAD_DIM, (n_q + n_kv + hh + 1) * _HEAD_DIM)
        v_ref[:, 2 * hh * _HEAD_DIM:(2 * hh + 1) * _HEAD_DIM] = qkv[:, src].astype(v_ref.dtype)
        v_ref[:, (2 * hh + 1) * _HEAD_DIM:(2 * hh + 2) * _HEAD_DIM] = ones_col


_Q_SCALE = (_HEAD_DIM ** -0.5) * _LOG2E


def _qkv_call(x, g, mod, w, q_g, k_g, cos, sin, n_q, n_kv):
    b, l, d = x.shape
    n = w.shape[1]
    tm = _tile(l, 512)
    nq = n_q * _HEAD_DIM
    nkv = n_kv * _HEAD_DIM
    q_scale = _Q_SCALE
    return pl.pallas_call(
        functools.partial(_qkv_kernel, n_q=n_q, n_kv=n_kv, q_scale=q_scale),
        grid=(b, l // tm),
        in_specs=[
            pl.BlockSpec((None, tm, d), lambda i, t: (i, t, 0)),
            pl.BlockSpec((1, d), lambda i, t: (0, 0)),
            pl.BlockSpec((None, 6, d), lambda i, t: (i, 0, 0)),
            pl.BlockSpec((d, n), lambda i, t: (0, 0)),
            pl.BlockSpec((1, _HEAD_DIM), lambda i, t: (0, 0)),
            pl.BlockSpec((1, _HEAD_DIM), lambda i, t: (0, 0)),
            pl.BlockSpec((tm, _HEAD_DIM), lambda i, t: (t, 0)),
            pl.BlockSpec((tm, _HEAD_DIM), lambda i, t: (t, 0)),
        ],
        out_specs=[
            pl.BlockSpec((None, tm, nq), lambda i, t: (i, t, 0)),
            pl.BlockSpec((None, n_kv, _HEAD_DIM, tm), lambda i, t: (i, 0, 0, t)),
            pl.BlockSpec((None, tm, 2 * nkv), lambda i, t: (i, t, 0)),
        ],
        out_shape=[
            jax.ShapeDtypeStruct((b, l, nq), BF16),
            jax.ShapeDtypeStruct((b, n_kv, _HEAD_DIM, l), BF16),
            jax.ShapeDtypeStruct((b, l, 2 * nkv), BF16),
        ],
        compiler_params=_params(("parallel", "parallel"), 48),
        name="qkv_norm_rope",
    )(x, g.reshape(1, d), mod, w, q_g.reshape(1, _HEAD_DIM), k_g.reshape(1, _HEAD_DIM), cos, sin)


def _lane_tile(x, reps):
    return x if reps == 1 else jnp.concatenate([x] * reps, axis=1)


def _attn_kernel(q_ref, k_ref, v_ref, o_ref, m_ref, l_ref, acc_ref, *, tq, tk, groups, n_chunks):
    q = jnp.concatenate([q_ref[:, g * _HEAD_DIM:(g + 1) * _HEAD_DIM] for g in range(groups)], axis=0)
    m_ref[...] = jnp.full_like(m_ref, -jnp.inf)
    l_ref[...] = jnp.zeros_like(l_ref)
    acc_ref[...] = jnp.zeros_like(acc_ref)

    def body(c, carry):
        start = pl.multiple_of(c * tk, tk)
        k = k_ref[pl.ds(start, tk), :]
        v = v_ref[pl.ds(start, tk), :]
        s = lax.dot_general(q, k, (((1,), (1,)), ((), ())), preferred_element_type=F32)
        m_prev = m_ref[...]
        m_next = jnp.maximum(m_prev, jnp.max(s, axis=1, keepdims=True))
        alpha = jnp.exp2(m_prev - m_next)
        p = jnp.exp2(s - _lane_tile(m_next, tk // _LANES))
        l_ref[...] = alpha * l_ref[...] + jnp.sum(p, axis=1, keepdims=True)
        acc_ref[...] = alpha * acc_ref[...] + _dot(p.astype(BF16), v)
        m_ref[...] = m_next
        return carry

    lax.fori_loop(0, n_chunks, body, 0)
    o = acc_ref[...] * (1.0 / l_ref[...])
    for g in range(groups):
        o_ref[:, g * _HEAD_DIM:(g + 1) * _HEAD_DIM] = o[g * tq:(g + 1) * tq, :].astype(o_ref.dtype)


def _attn_bounded_kernel(q_ref, kt_ref, v_ref, o_ref, *, tq, tk, groups, n_chunks):
    q = jnp.concatenate([q_ref[:, g * _HEAD_DIM:(g + 1) * _HEAD_DIM] for g in range(groups)], axis=0)
    acc = None
    for c in range(n_chunks):
        p = jnp.exp2(_dot(q, kt_ref[:, c * tk:(c + 1) * tk])).astype(BF16)
        pv = _dot(p, v_ref[c * tk:(c + 1) * tk, :])
        acc = pv if acc is None else acc + pv
    o = acc[:, :_HEAD_DIM] * (1.0 / acc[:, _HEAD_DIM:_HEAD_DIM + 1])
    for g in range(groups):
        o_ref[:, g * _HEAD_DIM:(g + 1) * _HEAD_DIM] = o[g * tq:(g + 1) * tq, :].astype(o_ref.dtype)


_SCORE_BOUND = 60.0
_KV_CHUNKS = (256, 128)


def _attn_bounded_call(q, kt, vx, n_kv):
    b, l, nq = q.shape
    s = kt.shape[3]
    groups = nq // (n_kv * _HEAD_DIM)
    gw = groups * _HEAD_DIM
    tq = _tile(l, 512)
    tk = next(c for c in _KV_CHUNKS if s % c == 0)
    n_chunks = s // tk
    vw = 2 * _HEAD_DIM
    return pl.pallas_call(
        functools.partial(_attn_bounded_kernel, tq=tq, tk=tk, groups=groups, n_chunks=n_chunks),
        grid=(b, n_kv, l // tq),
        in_specs=[
            pl.BlockSpec((None, tq, gw), lambda i, j, t: (i, t, j)),
            pl.BlockSpec((None, None, _HEAD_DIM, s), lambda i, j, t: (i, j, 0, 0)),
            pl.BlockSpec((None, s, vw), lambda i, j, t: (i, 0, j)),
        ],
        out_specs=pl.BlockSpec((None, tq, gw), lambda i, j, t: (i, t, j)),
        out_shape=jax.ShapeDtypeStruct((b, l, nq), BF16),
        compiler_params=_params(("parallel", "parallel", "arbitrary"), 56),
        name="gqa_attention_bounded",
    )(q, kt, vx)


def _attn_call(q, kt, vx, n_kv):
    b, l, nq = q.shape
    s = kt.shape[3]
    k = kt.transpose(0, 3, 1, 2).reshape(b, s, n_kv * _HEAD_DIM)
    v = vx.reshape(b, s, n_kv, 2 * _HEAD_DIM)[..., :_HEAD_DIM].reshape(b, s, n_kv * _HEAD_DIM)
    groups = nq // (n_kv * _HEAD_DIM)
    gw = groups * _HEAD_DIM
    tq = _tile(l, 256)
    tk = 768 if s % 768 == 0 else _tile(s, 512)
    n_chunks = s // tk
    m_rows = groups * tq
    return pl.pallas_call(
        functools.partial(_attn_kernel, tq=tq, tk=tk, groups=groups, n_chunks=n_chunks),
        grid=(b, n_kv, l // tq),
        in_specs=[
            pl.BlockSpec((None, tq, gw), lambda i, j, t: (i, t, j)),
            pl.BlockSpec((None, s, _HEAD_DIM), lambda i, j, t: (i, 0, j)),
            pl.BlockSpec((None, s, _HEAD_DIM), lambda i, j, t: (i, 0, j)),
        ],
        out_specs=pl.BlockSpec((None, tq, gw), lambda i, j, t: (i, t, j)),
        out_shape=jax.ShapeDtypeStruct((b, l, nq), BF16),
        scratch_shapes=[
            pltpu.VMEM((m_rows, _LANES), F32),
            pltpu.VMEM((m_rows, _LANES), F32),
            pltpu.VMEM((m_rows, _HEAD_DIM), F32),
        ],
        compiler_params=_params(("parallel", "parallel", "arbitrary"), 48),
        name="gqa_flash_attention",
    )(q, k, v)


_INFO_E0, _INFO_E1, _INFO_R0, _INFO_R1, _INFO_G0, _INFO_G1 = range(6)


def _route_kernel(x_ref, g_ref, mod_ref, rw_ref, h_ref, info_ref, slots_ref, cnt_ref, carry_ref, tri_ref,
                  *, n_exp, tm):
    step = pl.program_id(0)

    @pl.when(step == 0)
    def _():
        carry_ref[...] = jnp.zeros_like(carry_ref)
        r = lax.broadcasted_iota(jnp.int32, (tm, tm), 0)
        cidx = lax.broadcasted_iota(jnp.int32, (tm, tm), 1)
        tri_ref[...] = jnp.where(r > cidx, 1.0, 0.0).astype(BF16)

    h = _rms_mod(x_ref[...], g_ref[...], mod_ref[3:4, :], mod_ref[4:5, :])
    h_ref[...] = pltpu.einshape("t(jc)->tjc", h, j=h_ref.shape[1])
    logits = _dot3(h, rw_ref[...])
    lane = lax.broadcasted_iota(jnp.int32, logits.shape, 1)
    neg = jnp.float32(-jnp.inf)
    lg = jnp.where(lane < n_exp, logits, neg)
    m0 = jnp.max(lg, axis=1, keepdims=True)
    i0 = jnp.min(jnp.where(lg == m0, lane, _LANES), axis=1, keepdims=True)
    oh0 = lane == i0
    lg1 = jnp.where(oh0, neg, lg)
    m1 = jnp.max(lg1, axis=1, keepdims=True)
    i1 = jnp.min(jnp.where(lg1 == m1, lane, _LANES), axis=1, keepdims=True)
    oh1 = lane == i1
    e = jnp.exp(m1 - m0)
    g0 = 1.0 / (1.0 + e)
    g1 = e * g0

    chosen = jnp.where(oh0 | oh1, 1.0, 0.0)
    before = _dot(tri_ref[...], chosen.astype(BF16)) + carry_ref[...]
    r0 = jnp.sum(jnp.where(oh0, before, 0.0), axis=1, keepdims=True)
    r1 = jnp.sum(jnp.where(oh1, before, 0.0), axis=1, keepdims=True)
    total = carry_ref[...] + jnp.sum(chosen, axis=0, keepdims=True)
    carry_ref[...] = total
    cnt_ref[...] = total

    info = jnp.zeros(logits.shape, F32)
    for idx, val in ((_INFO_E0, i0.astype(F32)), (_INFO_E1, i1.astype(F32)), (_INFO_R0, r0), (_INFO_R1, r1),
                     (_INFO_G0, g0), (_INFO_G1, g1)):
        info = jnp.where(lane == idx, val, info)
    info_ref[...] = info
    slots_ref[...] = info.T[:_SUBLANES, :].astype(jnp.int32)


def _route_call(x, g, mod, router):
    b, l, d = x.shape
    n_exp = router.shape[1]
    t = b * l
    tm = _tile(l, 512)
    per_b = l // tm
    rw = jnp.zeros((d, _LANES), F32).at[:, :n_exp].set(router)
    return pl.pallas_call(
        functools.partial(_route_kernel, n_exp=n_exp, tm=tm),
        grid=(t // tm,),
        in_specs=[
            pl.BlockSpec((tm, d), lambda i: (i, 0)),
            pl.BlockSpec((1, d), lambda i: (0, 0)),
            pl.BlockSpec((None, 6, d), lambda i: (i // per_b, 0, 0)),
            pl.BlockSpec((d, _LANES), lambda i: (0, 0)),
        ],
        out_specs=[
            pl.BlockSpec((tm, d // _LANES, _LANES), lambda i: (i, 0, 0)),
            pl.BlockSpec((tm, _LANES), lambda i: (i, 0)),
            pl.BlockSpec((_SUBLANES, tm), lambda i: (0, i)),
            pl.BlockSpec((1, _LANES), lambda i: (0, 0)),
        ],
        out_shape=[
            jax.ShapeDtypeStruct((t, d // _LANES, _LANES), F32),
            jax.ShapeDtypeStruct((t, _LANES), F32),
            jax.ShapeDtypeStruct((_SUBLANES, t), jnp.int32),
            jax.ShapeDtypeStruct((1, _LANES), F32),
        ],
        scratch_shapes=[pltpu.VMEM((1, _LANES), F32), pltpu.VMEM((tm, tm), BF16)],
        compiler_params=_params(("arbitrary",), 40),
        name="moe_route",
    )(x.reshape(t, d), g.reshape(1, d), mod, rw)


def _pos_tiles(pos0, pos1, tt):
    n = pos0.shape[0] // tt
    return jnp.stack([pos0.reshape(n, tt), pos1.reshape(n, tt)], axis=1).reshape(n * 2 * tt)


def _row_copy(src, src_row, dst, dst_row, sem):
    return pltpu.make_async_copy(src.at[pl.ds(src_row, 1)], dst.at[pl.ds(dst_row, 1)], sem)


_ROW_DMA_UNROLL = 8


def _dispatch_kernel(pos_ref, h_ref, xs_in_ref, xs_ref, pos_smem, sem_s, sem_d, *, tt):
    del xs_in_ref
    cp = pltpu.make_async_copy(pos_ref, pos_smem, sem_s)
    cp.start()
    cp.wait()

    def body(r, carry):
        _row_copy(h_ref, r, xs_ref, pos_smem[r], sem_d).start(priority=0)
        _row_copy(h_ref, r, xs_ref, pos_smem[tt + r], sem_d).start(priority=1)
        return carry

    lax.fori_loop(0, tt, body, 0, unroll=_ROW_DMA_UNROLL)
    for _ in range(2 * tt):
        _row_copy(h_ref, 0, xs_ref, 0, sem_d).wait()


def _dispatch_call(h, pos_t, p_rows, tt):
    t, nb, _ = h.shape
    xs0 = jnp.zeros((p_rows, nb, _LANES), F32)
    return pl.pallas_call(
        functools.partial(_dispatch_kernel, tt=tt),
        grid=(t // tt,),
        in_specs=[
            pl.BlockSpec((2 * tt,), lambda i: (i,)),
            pl.BlockSpec((tt, nb, _LANES), lambda i: (i, 0, 0)),
            pl.BlockSpec(memory_space=pl.ANY),
        ],
        out_specs=pl.BlockSpec(memory_space=pl.ANY),
        out_shape=jax.ShapeDtypeStruct((p_rows, nb, _LANES), F32),
        scratch_shapes=[
            pltpu.SMEM((2 * tt,), jnp.int32),
            pltpu.SemaphoreType.DMA(()),
            pltpu.SemaphoreType.DMA(()),
        ],
        input_output_aliases={2: 0},
        compiler_params=_params(("arbitrary",), 32),
        name="moe_dispatch",
    )(pos_t, h, xs0)


def _moe_kernel(be_ref, bs_ref, bv_ref, x_ref, w1_ref, w3_ref, w2_ref, o_ref, acc_ref):
    del be_ref, bs_ref
    i = pl.program_id(0)
    f = pl.program_id(1)

    @pl.when((i == 0) & (f == 0))
    def _():
        acc_ref[...] = jnp.zeros_like(acc_ref)

    @pl.when(bv_ref[i] > 0)
    def _():
        xb = pltpu.einshape("tjc->t(jc)", x_ref[...]).astype(BF16)
        h1 = _dot(xb, w1_ref[...])
        h3 = _dot(xb, w3_ref[...])
        hid = (h1 * _sigmoid(h1)) * h3
        acc = jnp.where(f == 0, 0.0, acc_ref[...]) + _dot(hid.astype(BF16), w2_ref[...])
        acc_ref[...] = acc
        o_ref[...] = pltpu.einshape("t(jc)->tjc", acc, j=o_ref.shape[1])


def _moe_call(blk_e, blk_src, blk_valid, xs, w1, w3, w2, tmm):
    p_rows, nb, _ = xs.shape
    d = nb * _LANES
    ff = w1.shape[2]
    tf = 1792 if ff % 1792 == 0 else _tile(ff, 512)
    nf = ff // tf
    nblk = p_rows // tmm

    def fsel(i, f, bv):
        return jnp.where(bv[i] > 0, f, nf - 1)

    grid_spec = pltpu.PrefetchScalarGridSpec(
        num_scalar_prefetch=3,
        grid=(nblk, nf),
        in_specs=[
            pl.BlockSpec((tmm, nb, _LANES), lambda i, f, be, bs, bv: (bs[i], 0, 0)),
            pl.BlockSpec((None, d, tf), lambda i, f, be, bs, bv: (be[i], 0, fsel(i, f, bv))),
            pl.BlockSpec((None, d, tf), lambda i, f, be, bs, bv: (be[i], 0, fsel(i, f, bv))),
            pl.BlockSpec((None, tf, d), lambda i, f, be, bs, bv: (be[i], fsel(i, f, bv), 0)),
        ],
        out_specs=pl.BlockSpec((tmm, nb, _LANES), lambda i, f, be, bs, bv: (bs[i], 0, 0)),
        scratch_shapes=[pltpu.VMEM((tmm, d), F32)],
    )
    return pl.pallas_call(
        _moe_kernel,
        grid_spec=grid_spec,
        out_shape=jax.ShapeDtypeStruct((p_rows, nb, _LANES), F32),
        compiler_params=_params(("arbitrary", "arbitrary"), 58),
        name="moe_experts",
    )(blk_e, blk_src, blk_valid, xs, w1, w3, w2)


def _combine_kernel(pos_ref, pos_next_ref, x_ref, mod_ref, info_ref, fg_ref, y_ref, o_ref, pos_smem, ybuf,
                    sem_s, sem_d, *, tt, n_steps):
    i = pl.program_id(0)
    slot = lax.rem(i, 2)

    def start_gathers(p_ref, s):
        cp = pltpu.make_async_copy(p_ref, pos_smem, sem_s)
        cp.start()
        cp.wait()

        def body(r, carry):
            _row_copy(y_ref, pos_smem[r], ybuf.at[s, 0], r, sem_d.at[s]).start(priority=0)
            _row_copy(y_ref, pos_smem[tt + r], ybuf.at[s, 1], r, sem_d.at[s]).start(priority=1)
            return carry

        lax.fori_loop(0, tt, body, 0, unroll=_ROW_DMA_UNROLL)

    @pl.when(i == 0)
    def _():
        start_gathers(pos_ref, 0)

    @pl.when(i + 1 < n_steps)
    def _():
        start_gathers(pos_next_ref, 1 - slot)

    for _ in range(2 * tt):
        _row_copy(y_ref, 0, ybuf.at[slot, 0], 0, sem_d.at[slot]).wait()

    info = info_ref[...]
    g0 = info[:, _INFO_G0:_INFO_G0 + 1]
    g1 = info[:, _INFO_G1:_INFO_G1 + 1]
    y0 = pltpu.einshape("tjc->t(jc)", ybuf[slot, 0])
    y1 = pltpu.einshape("tjc->t(jc)", ybuf[slot, 1])
    ffn = g0 * y0 + g1 * y1
    xo = x_ref[...] + mod_ref[5:6, :] * ffn
    o_ref[...] = xo * lax.rsqrt(jnp.mean(xo * xo, axis=-1, keepdims=True) + _EPS) * fg_ref[...]


def _combine_call(pos_t, x, mod, info, final_g, y, tt):
    b, l, d = x.shape
    t = b * l
    per_b = l // tt
    n_steps = t // tt
    return pl.pallas_call(
        functools.partial(_combine_kernel, tt=tt, n_steps=n_steps),
        grid=(n_steps,),
        in_specs=[
            pl.BlockSpec((2 * tt,), lambda i: (i,)),
            pl.BlockSpec((2 * tt,), lambda i: (jnp.minimum(i + 1, n_steps - 1),)),
            pl.BlockSpec((tt, d), lambda i: (i, 0)),
            pl.BlockSpec((None, 6, d), lambda i: (i // per_b, 0, 0)),
            pl.BlockSpec((tt, _LANES), lambda i: (i, 0)),
            pl.BlockSpec((1, d), lambda i: (0, 0)),
            pl.BlockSpec(memory_space=pl.ANY),
        ],
        out_specs=pl.BlockSpec((tt, d), lambda i: (i, 0)),
        out_shape=jax.ShapeDtypeStruct((t, d), F32),
        scratch_shapes=[
            pltpu.SMEM((2 * tt,), jnp.int32),
            pltpu.VMEM((2, 2, tt, d // _LANES, _LANES), F32),
            pltpu.SemaphoreType.DMA(()),
            pltpu.SemaphoreType.DMA((2,)),
        ],
        compiler_params=_params(("arbitrary",), 40),
        name="moe_combine_norm",
    )(pos_t, pos_t, x.reshape(t, d), mod, info, final_g.reshape(1, d), y).reshape(b, l, d)


def _moe_layer(x, g, mod, router, w1, w3, w2, final_g):
    b, l, d = x.shape
    t = b * l
    n_exp = router.shape[1]
    tmm = 512 if t >= 4096 else 128
    tt = _tile(l, 512)
    h, info, slots, counts = _route_call(x, g, mod, router)

    counts = counts[0, :n_exp].astype(jnp.int32)
    padded = ((counts + tmm - 1) // tmm) * tmm
    pend = jnp.cumsum(padded)
    gstart = pend - padded
    pos0 = gstart[slots[_INFO_E0]] + slots[_INFO_R0]
    pos1 = gstart[slots[_INFO_E1]] + slots[_INFO_R1]
    pos_t = _pos_tiles(pos0, pos1, tt)

    nblk = (t * 2) // tmm + n_exp
    p_rows = nblk * tmm
    bstart = jnp.arange(nblk, dtype=jnp.int32) * tmm
    used = bstart < pend[-1]
    last = pend[-1] // tmm - 1
    bidx = jnp.where(used, jnp.arange(nblk, dtype=jnp.int32), last)
    blk_e = jnp.minimum(jnp.sum(((bidx * tmm)[:, None] >= pend[None, :]).astype(jnp.int32), axis=1), n_exp - 1)
    blk_valid = jnp.where(used, jnp.clip(counts[blk_e] - (bidx * tmm - gstart[blk_e]), 0, tmm), 0).astype(jnp.int32)

    xs = _dispatch_call(h, pos_t, p_rows, tt)
    y = _moe_call(blk_e, bidx.astype(jnp.int32), blk_valid, xs, w1.astype(BF16), w3.astype(BF16), w2.astype(BF16),
                  tmm)
    return _combine_call(pos_t, x, mod, info, final_g, y, tt)


def kernel(x, c, ctx, c_ctx, ada_w, ada_b, norm1_g, norm2_g, rg_w_in, rg_conv_w, rg_conv_b, rg_w_a, rg_b_a,
           rg_w_i, rg_b_i, rg_lam, rg_w_out, attn_w_qkv, attn_q_g, attn_k_g, attn_w_o, ffn_w1, ffn_w3, ffn_w2,
           moe_router, moe_w1, moe_w3, moe_w2, final_g):
    b, l, d = x.shape
    n_ctx = ctx.shape[1]
    c_dim = rg_w_out.shape[1]
    n_q = attn_w_o.shape[1] // _HEAD_DIM
    n_kv = (attn_w_qkv.shape[2] // _HEAD_DIM - n_q) // 2

    rows = 16
    c_rows = jnp.zeros((rows, d), F32).at[:b].set(c).at[b].set(c_ctx)
    mod = _ada_call(c_rows, ada_w, ada_b).reshape(ada_w.shape[0], rows, 6, d)
    mod_l = [mod[i, :b] for i in range(2)]
    mod_c = [jnp.broadcast_to(mod[i, b], (b, 6, d)) for i in range(2)]

    w_in = rg_w_in[0].astype(BF16)
    w_out = rg_w_out[0].astype(BF16)
    w_a = rg_w_a[0].astype(BF16)
    w_i = rg_w_i[0].astype(BF16)
    xg_l = _nmm_call(x, norm1_g[0], mod_l[0], w_in, 0, 1, F32)
    xg_c = _nmm_call(ctx, norm1_g[0], mod_c[0], w_in, 0, 1, F32)
    zero_h = jnp.zeros((b, c_dim // _LANES, _LANES), F32)

    def lru(xg, h0, tail, direction, reverse):
        return _lru_call(xg, rg_conv_w[0], rg_conv_b[0], w_a[direction], rg_b_a[0, direction], w_i[direction],
                         rg_b_i[0, direction], rg_lam[0, direction], h0, tail, reverse=reverse)

    yc_rev, h0_rev = lru(xg_c, zero_h, None, 1, True)
    ctx, h0_fwd = lru(xg_c, zero_h, (yc_rev, w_out, ctx, mod_c[0]), 0, False)
    yl_rev, _ = lru(xg_l, h0_rev, None, 1, True)
    x, _ = lru(xg_l, h0_fwd, (yl_rev, w_out, x, mod_l[0]), 0, False)
    f1, f3, f2 = ffn_w1[0].astype(BF16), ffn_w3[0].astype(BF16), ffn_w2[0].astype(BF16)
    x = _ffn_call(x, norm2_g[0], mod_l[0], f1, f3, f2)
    ctx = _ffn_call(ctx, norm2_g[0], mod_c[0], f1, f3, f2)

    w_qkv = attn_w_qkv[0].astype(BF16)
    cos, sin = _rope_tables(l)
    q, k_l, v_l = _qkv_call(x, norm1_g[1], mod_l[1], w_qkv, attn_q_g[0], attn_k_g[0], cos, sin, n_q, n_kv)
    ones = jnp.ones((n_ctx, _HEAD_DIM), F32)
    _, k_c, v_c = _qkv_call(ctx, norm1_g[1], mod_c[1], w_qkv, attn_q_g[0], attn_k_g[0], ones, jnp.zeros_like(ones),
                            n_q, n_kv)
    k_all = jnp.concatenate([k_c, k_l], axis=3)
    v_all = jnp.concatenate([v_c, v_l], axis=1)
    score_bound = _HEAD_DIM * jnp.max(jnp.abs(attn_q_g[0])) * jnp.max(jnp.abs(attn_k_g[0])) * _Q_SCALE
    o = lax.cond(score_bound <= _SCORE_BOUND,
                 lambda: _attn_bounded_call(q, k_all, v_all, n_kv),
                 lambda: _attn_call(q, k_all, v_all, n_kv))
    x = _proj_res_call(o, attn_w_o[0].astype(BF16), x, mod_l[1], 2)
    return _moe_layer(x, norm2_g[1], mod_l[1], moe_router[0], moe_w1[0], moe_w3[0], moe_w2[0], final_g)
```

```python
import functools

import jax
import jax.numpy as jnp
from jax import lax
from jax.experimental import pallas as pl
from jax.experimental.pallas import tpu as pltpu

F32 = jnp.float32
BF16 = jnp.bfloat16

_EPS = 1e-6
_HEAD_DIM = 128
_GRID_W = 64
_ROPE_THETA = 10000.0
_ROPE_NFREQ = _HEAD_DIM // 4
_LRU_C = 8.0
_CONV_W = 4
_LANES = 128
_SUBLANES = 8
_LOG2E = 1.4426950408889634
_MIB = 1024 * 1024


def _params(semantics, vmem_mib):
    return pltpu.CompilerParams(dimension_semantics=semantics, vmem_limit_bytes=vmem_mib * _MIB)


def _tile(n, pref):
    if n <= pref:
        return n
    t = pref
    while n % t:
        t //= 2
    return t


def _dot(a, b):
    return jnp.dot(a, b, preferred_element_type=F32)


def _split_bf16(a):
    hi = a.astype(BF16)
    lo = (a - hi.astype(F32)).astype(BF16)
    return hi, lo


def _dot3(a, b):
    ah, al = _split_bf16(a)
    bh, bl = _split_bf16(b)
    return _dot(ah, bh) + (_dot(al, bh) + _dot(ah, bl))


def _rms_mod(x, g, shift, scale):
    y = x * lax.rsqrt(jnp.mean(x * x, axis=-1, keepdims=True) + _EPS)
    return (y * g) * (1.0 + scale) + shift


def _sigmoid(x):
    return 1.0 / (1.0 + jnp.exp(-x))


def _ada_kernel(c_ref, w_ref, b_ref, o_ref):
    c = c_ref[...]
    s = c * _sigmoid(c)
    o_ref[...] = _dot3(s, w_ref[...]) + b_ref[...]


def _ada_call(c_rows, ada_w, ada_b):
    depth, d, n = ada_w.shape
    rows = c_rows.shape[0]
    tn = _tile(n, 1536)
    return pl.pallas_call(
        _ada_kernel,
        grid=(depth, n // tn),
        in_specs=[
            pl.BlockSpec((rows, d), lambda i, j: (0, 0)),
            pl.BlockSpec((None, d, tn), lambda i, j: (i, 0, j)),
            pl.BlockSpec((None, 1, tn), lambda i, j: (i, 0, j)),
        ],
        out_specs=pl.BlockSpec((None, rows, tn), lambda i, j: (i, 0, j)),
        out_shape=jax.ShapeDtypeStruct((depth, rows, n), F32),
        compiler_params=_params(("parallel", "parallel"), 40),
        name="ada_mod",
    )(c_rows, ada_w, ada_b.reshape(depth, 1, n))


def _nmm_kernel(x_ref, g_ref, mod_ref, w_ref, o_ref, *, sh, sc):
    h = _rms_mod(x_ref[...], g_ref[...], mod_ref[sh:sh + 1, :], mod_ref[sc:sc + 1, :])
    o_ref[...] = _dot(h.astype(BF16), w_ref[...]).astype(o_ref.dtype)


def _nmm_call(x, g, mod, w, sh, sc, out_dtype):
    b, l, d = x.shape
    n = w.shape[1]
    tm = _tile(l, 512)
    return pl.pallas_call(
        functools.partial(_nmm_kernel, sh=sh, sc=sc),
        grid=(b, l // tm),
        in_specs=[
            pl.BlockSpec((None, tm, d), lambda i, t: (i, t, 0)),
            pl.BlockSpec((1, d), lambda i, t: (0, 0)),
            pl.BlockSpec((None, 6, d), lambda i, t: (i, 0, 0)),
            pl.BlockSpec((d, n), lambda i, t: (0, 0)),
        ],
        out_specs=pl.BlockSpec((None, tm, n), lambda i, t: (i, t, 0)),
        out_shape=jax.ShapeDtypeStruct((b, l, n), out_dtype),
        compiler_params=_params(("parallel", "parallel"), 48),
        name="norm_mod_matmul",
    )(x, g.reshape(1, d), mod, w)


def _log1p_pos(e):
    u = 1.0 + e
    den = jnp.where(u == 1.0, 1.0, u - 1.0)
    return jnp.where(u == 1.0, e, jnp.log(u) * (e * (1.0 / den)))


def _softplus(x):
    return jnp.maximum(x, 0.0) + _log1p_pos(jnp.exp(-jnp.abs(x)))


def _sigmoid_tanh(x):
    return 0.5 * jnp.tanh(0.5 * x) + 0.5


def _scan_rows(a, b, h, reverse):
    tl = a.shape[0]
    nb = a.shape[1] // _LANES
    a3 = pltpu.einshape("t(jc)->tjc", a, j=nb)
    b3 = pltpu.einshape("t(jc)->tjc", b, j=nb)
    states = [None] * tl
    for t in (range(tl - 1, -1, -1) if reverse else range(tl)):
        h = a3[t] * h + b3[t]
        states[t] = h
    return pltpu.einshape("tjc->t(jc)", jnp.stack(states, axis=0)), h


def _lru_kernel(*refs, reverse, tl, nt, fuse):
    if fuse:
        (xm_ref, xp_ref, xn_ref, cw_ref, cb_ref, wa_ref, ba_ref, wi_ref, bi_ref, lam_ref, h0_ref,
         yo_ref, gb_ref, wo_ref, res_ref, mod_ref, out_ref, hl_ref, carry_ref) = refs
    else:
        (xm_ref, xp_ref, xn_ref, cw_ref, cb_ref, wa_ref, ba_ref, wi_ref, bi_ref, lam_ref, h0_ref,
         out_ref, hl_ref, carry_ref) = refs
    t = pl.program_id(1)
    tt = (nt - 1 - t) if reverse else t

    @pl.when(t == 0)
    def _():
        carry_ref[...] = h0_ref[...]

    xm = xm_ref[...]
    prev = jnp.where(tt > 0, xp_ref[...], 0.0)
    nxt = jnp.where(tt < nt - 1, xn_ref[...], 0.0)
    row8 = lax.broadcasted_iota(jnp.int32, prev.shape, 0)

    def tap(offset):
        if offset < 0:
            s = -offset
            y = pltpu.roll(xm, s, axis=0)
            head = jnp.where(row8 < s, pltpu.roll(prev, s, axis=0), y[:_SUBLANES, :])
            return jnp.concatenate([head, y[_SUBLANES:, :]], axis=0)
        y = pltpu.roll(xm, tl - offset, axis=0)
        tail = jnp.where(row8 >= _SUBLANES - offset, pltpu.roll(nxt, _SUBLANES - offset, axis=0),
                         y[tl - _SUBLANES:, :])
        return jnp.concatenate([y[:tl - _SUBLANES, :], tail], axis=0)

    cw = cw_ref[...]
    left = _CONV_W // 2
    xc = cb_ref[...]
    for k in range(_CONV_W):
        xk = xm if k == left else tap(k - left)
        xc = xc + xk * cw[k:k + 1, :]

    xcb = xc.astype(BF16)
    nb = wa_ref.shape[0]
    blocks = [slice(j * _LANES, (j + 1) * _LANES) for j in range(nb)]
    pre_r = jnp.concatenate([_dot(xcb[:, s], wa_ref[j]) for j, s in enumerate(blocks)], axis=1)
    pre_i = jnp.concatenate([_dot(xcb[:, s], wi_ref[j]) for j, s in enumerate(blocks)], axis=1)
    r = _sigmoid_tanh(pre_r + ba_ref[...])
    i = _sigmoid_tanh(pre_i + bi_ref[...])
    decay = (_LRU_C * r) * _softplus(-lam_ref[...])
    a = jnp.exp(-decay)
    mult = jnp.sqrt(jnp.tanh(decay) * (1.0 + a * a))
    b = mult * (i * xc)

    hs, new_carry = _scan_rows(a, b, carry_ref[...], reverse)
    carry_ref[...] = new_carry
    hl_ref[...] = new_carry
    if fuse:
        gate = jax.nn.gelu(gb_ref[...], approximate=True)
        z = ((hs + yo_ref[...]) * gate).astype(BF16)
        out_ref[...] = res_ref[...] + mod_ref[2:3, :] * _dot(z, wo_ref[...])
    else:
        out_ref[...] = hs


def _lru_call(xg, conv_w, conv_b, w_a, b_a, w_i, b_i, lam, h0, tail=None, *, reverse):
    b, l, c2 = xg.shape
    c = c2 // 2
    nb = c // _LANES
    tl = _tile(l, 256)
    nt = l // tl
    hb = tl // _SUBLANES
    nh = l // _SUBLANES
    fuse = tail is not None

    def tmap(t):
        return (nt - 1 - t) if reverse else t

    vec = lambda: pl.BlockSpec((1, c), lambda i, t: (0, 0))
    gate_w = lambda: pl.BlockSpec((nb, _LANES, _LANES), lambda i, t: (0, 0, 0))
    in_specs = [
        pl.BlockSpec((None, tl, c), lambda i, t: (i, tmap(t), 0)),
        pl.BlockSpec((None, _SUBLANES, c), lambda i, t: (i, jnp.maximum(tmap(t) * hb - 1, 0), 0)),
        pl.BlockSpec((None, _SUBLANES, c), lambda i, t: (i, jnp.minimum((tmap(t) + 1) * hb, nh - 1), 0)),
        pl.BlockSpec((_CONV_W, c), lambda i, t: (0, 0)),
        vec(),
        gate_w(),
        vec(),
        gate_w(),
        vec(),
        vec(),
        pl.BlockSpec((None, nb, _LANES), lambda i, t: (i, 0, 0)),
    ]
    args = [xg, xg, xg, conv_w, conv_b.reshape(1, c), w_a, b_a.reshape(1, c), w_i, b_i.reshape(1, c),
            lam.reshape(1, c), h0]
    d_out = c
    if fuse:
        y_other, w_out, res, mod = tail
        d_out = w_out.shape[1]
        in_specs += [
            pl.BlockSpec((None, tl, c), lambda i, t: (i, tmap(t), 0)),
            pl.BlockSpec((None, tl, c), lambda i, t: (i, tmap(t), 1)),
            pl.BlockSpec((c, d_out), lambda i, t: (0, 0), pipeline_mode=pl.Buffered(1)),
            pl.BlockSpec((None, tl, d_out), lambda i, t: (i, tmap(t), 0)),
            pl.BlockSpec((None, 6, d_out), lambda i, t: (i, 0, 0)),
        ]
        args += [y_other, xg, w_out, res, mod]
    return pl.pallas_call(
        functools.partial(_lru_kernel, reverse=reverse, tl=tl, nt=nt, fuse=fuse),
        grid=(b, nt),
        in_specs=in_specs,
        out_specs=[
            pl.BlockSpec((None, tl, d_out), lambda i, t: (i, tmap(t), 0)),
            pl.BlockSpec((None, nb, _LANES), lambda i, t: (i, 0, 0)),
        ],
        out_shape=[jax.ShapeDtypeStruct((b, l, d_out), F32), jax.ShapeDtypeStruct((b, nb, _LANES), F32)],
        scratch_shapes=[pltpu.VMEM((nb, _LANES), F32)],
        compiler_params=_params(("parallel", "arbitrary"), 48),
        name="lru_rev" if reverse else "lru_fwd",
    )(*args)


def _proj_res_kernel(z_ref, w_ref, x_ref, mod_ref, o_ref, *, gi):
    o_ref[...] = x_ref[...] + mod_ref[gi:gi + 1, :] * _dot(z_ref[...], w_ref[...])


def _proj_res_call(z, w, x, mod, gi):
    b, l, d = x.shape
    k = z.shape[2]
    tm = _tile(l, 1024)
    return pl.pallas_call(
        functools.partial(_proj_res_kernel, gi=gi),
        grid=(b, l // tm),
        in_specs=[
            pl.BlockSpec((None, tm, k), lambda i, t: (i, t, 0)),
            pl.BlockSpec((k, d), lambda i, t: (0, 0)),
            pl.BlockSpec((None, tm, d), lambda i, t: (i, t, 0)),
            pl.BlockSpec((None, 6, d), lambda i, t: (i, 0, 0)),
        ],
        out_specs=pl.BlockSpec((None, tm, d), lambda i, t: (i, t, 0)),
        out_shape=jax.ShapeDtypeStruct((b, l, d), F32),
        compiler_params=_params(("parallel", "parallel"), 48),
        name="proj_residual",
    )(z, w, x, mod)


def _ffn_kernel(x_ref, g_ref, mod_ref, w1_ref, w3_ref, w2_ref, o_ref):
    x = x_ref[...]
    h = _rms_mod(x, g_ref[...], mod_ref[3:4, :], mod_ref[4:5, :]).astype(BF16)
    h1 = _dot(h, w1_ref[...])
    h3 = _dot(h, w3_ref[...])
    hid = (h1 * _sigmoid(h1)) * h3
    o_ref[...] = x + mod_ref[5:6, :] * _dot(hid.astype(BF16), w2_ref[...])


def _ffn_call(x, g, mod, w1, w3, w2):
    b, l, d = x.shape
    ff = w1.shape[1]
    tm = _tile(l, 512)
    resident = pl.Buffered(1)
    return pl.pallas_call(
        _ffn_kernel,
        grid=(b, l // tm),
        in_specs=[
            pl.BlockSpec((None, tm, d), lambda i, t: (i, t, 0)),
            pl.BlockSpec((1, d), lambda i, t: (0, 0)),
            pl.BlockSpec((None, 6, d), lambda i, t: (i, 0, 0)),
            pl.BlockSpec((d, ff), lambda i, t: (0, 0), pipeline_mode=resident),
            pl.BlockSpec((d, ff), lambda i, t: (0, 0), pipeline_mode=resident),
            pl.BlockSpec((ff, d), lambda i, t: (0, 0), pipeline_mode=resident),
        ],
        out_specs=pl.BlockSpec((None, tm, d), lambda i, t: (i, t, 0)),
        out_shape=jax.ShapeDtypeStruct((b, l, d), F32),
        compiler_params=_params(("parallel", "parallel"), 56),
        name="dense_swiglu",
    )(x, g.reshape(1, d), mod, w1, w3, w2)


def _rope_tables(l):
    pos = jnp.arange(l)
    row = (pos // _GRID_W).astype(F32)
    col = (pos % _GRID_W).astype(F32)
    freqs = _ROPE_THETA ** (-jnp.arange(_ROPE_NFREQ, dtype=F32) / _ROPE_NFREQ)
    ar = row[:, None] * freqs
    ac = col[:, None] * freqs
    cos = jnp.concatenate([jnp.cos(ar), jnp.cos(ar), jnp.cos(ac), jnp.cos(ac)], axis=-1)
    sin = jnp.concatenate([-jnp.sin(ar), jnp.sin(ar), -jnp.sin(ac), jnp.sin(ac)], axis=-1)
    return cos, sin


def _qkv_kernel(x_ref, g_ref, mod_ref, w_ref, qg_ref, kg_ref, cos_ref, sin_ref, q_ref, k_ref, v_ref,
                *, n_q, n_kv, q_scale):
    h = _rms_mod(x_ref[...], g_ref[...], mod_ref[0:1, :], mod_ref[1:2, :])
    qkv = _dot(h.astype(BF16), w_ref[...])
    cos = cos_ref[...]
    sin = sin_ref[...]
    lane = lax.broadcasted_iota(jnp.int32, cos.shape, 1)
    first = (lane % (2 * _ROPE_NFREQ)) < _ROPE_NFREQ

    def norm_rope(v, gain):
        y = v * lax.rsqrt(jnp.mean(v * v, axis=-1, keepdims=True) + _EPS) * gain
        partner = jnp.where(first, pltpu.roll(y, _HEAD_DIM - _ROPE_NFREQ, axis=1),
                            pltpu.roll(y, _ROPE_NFREQ, axis=1))
        return y * cos + partner * sin

    for hh in range(n_q):
        sl = slice(hh * _HEAD_DIM, (hh + 1) * _HEAD_DIM)
        q_ref[:, sl] = (norm_rope(qkv[:, sl], qg_ref[...]) * q_scale).astype(q_ref.dtype)
    for hh in range(n_kv):
        src = slice((n_q + hh) * _HEAD_DIM, (n_q + hh + 1) * _HEAD_DIM)
        k_ref[hh] = norm_rope(qkv[:, src], kg_ref[...]).T.astype(k_ref.dtype)
    ones_col = jnp.where(lane == 0, 1.0, 0.0).astype(v_ref.dtype)
    for hh in range(n_kv):
        src = slice((n_q + n_kv + hh) * _HEAD_DIM, (n_q + n_kv + hh + 1) * _HEAD_DIM)
        v_ref[:, 2 * hh * _HEAD_DIM:(2 * hh + 1) * _HEAD_DIM] = qkv[:, src].astype(v_ref.dtype)
        v_ref[:, (2 * hh + 1) * _HEAD_DIM:(2 * hh + 2) * _HEAD_DIM] = ones_col


_Q_SCALE = (_HEAD_DIM ** -0.5) * _LOG2E


def _qkv_call(x, g, mod, w, q_g, k_g, cos, sin, n_q, n_kv):
    b, l, d = x.shape
    n = w.shape[1]
    tm = _tile(l, 512)
    nq = n_q * _HEAD_DIM
    nkv = n_kv * _HEAD_DIM
    q_scale = _Q_SCALE
    return pl.pallas_call(
        functools.partial(_qkv_kernel, n_q=n_q, n_kv=n_kv, q_scale=q_scale),
        grid=(b, l // tm),
        in_specs=[
            pl.BlockSpec((None, tm, d), lambda i, t: (i, t, 0)),
            pl.BlockSpec((1, d), lambda i, t: (0, 0)),
            pl.BlockSpec((None, 6, d), lambda i, t: (i, 0, 0)),
            pl.BlockSpec((d, n), lambda i, t: (0, 0)),
            pl.BlockSpec((1, _HEAD_DIM), lambda i, t: (0, 0)),
            pl.BlockSpec((1, _HEAD_DIM), lambda i, t: (0, 0)),
            pl.BlockSpec((tm, _HEAD_DIM), lambda i, t: (t, 0)),
            pl.BlockSpec((tm, _HEAD_DIM), lambda i, t: (t, 0)),
        ],
        out_specs=[
            pl.BlockSpec((None, tm, nq), lambda i, t: (i, t, 0)),
            pl.BlockSpec((None, n_kv, _HEAD_DIM, tm), lambda i, t: (i, 0, 0, t)),
            pl.BlockSpec((None, tm, 2 * nkv), lambda i, t: (i, t, 0)),
        ],
        out_shape=[
            jax.ShapeDtypeStruct((b, l, nq), BF16),
            jax.ShapeDtypeStruct((b, n_kv, _HEAD_DIM, l), BF16),
            jax.ShapeDtypeStruct((b, l, 2 * nkv), BF16),
        ],
        compiler_params=_params(("parallel", "parallel"), 48),
        name="qkv_norm_rope",
    )(x, g.reshape(1, d), mod, w, q_g.reshape(1, _HEAD_DIM), k_g.reshape(1, _HEAD_DIM), cos, sin)


def _lane_tile(x, reps):
    return x if reps == 1 else jnp.concatenate([x] * reps, axis=1)


def _attn_kernel(q_ref, k_ref, v_ref, o_ref, m_ref, l_ref, acc_ref, *, tq, tk, groups, n_chunks):
    q = jnp.concatenate([q_ref[:, g * _HEAD_DIM:(g + 1) * _HEAD_DIM] for g in range(groups)], axis=0)
    m_ref[...] = jnp.full_like(m_ref, -jnp.inf)
    l_ref[...] = jnp.zeros_like(l_ref)
    acc_ref[...] = jnp.zeros_like(acc_ref)

    def body(c, carry):
        start = pl.multiple_of(c * tk, tk)
        k = k_ref[pl.ds(start, tk), :]
        v = v_ref[pl.ds(start, tk), :]
        s = lax.dot_general(q, k, (((1,), (1,)), ((), ())), preferred_element_type=F32)
        m_prev = m_ref[...]
        m_next = jnp.maximum(m_prev, jnp.max(s, axis=1, keepdims=True))
        alpha = jnp.exp2(m_prev - m_next)
        p = jnp.exp2(s - _lane_tile(m_next, tk // _LANES))
        l_ref[...] = alpha * l_ref[...] + jnp.sum(p, axis=1, keepdims=True)
        acc_ref[...] = alpha * acc_ref[...] + _dot(p.astype(BF16), v)
        m_ref[...] = m_next
        return carry

    lax.fori_loop(0, n_chunks, body, 0)
    o = acc_ref[...] * (1.0 / l_ref[...])
    for g in range(groups):
        o_ref[:, g * _HEAD_DIM:(g + 1) * _HEAD_DIM] = o[g * tq:(g + 1) * tq, :].astype(o_ref.dtype)


def _attn_bounded_kernel(q_ref, kt_ref, v_ref, o_ref, *, tq, tk, groups, n_chunks):
    q = jnp.concatenate([q_ref[:, g * _HEAD_DIM:(g + 1) * _HEAD_DIM] for g in range(groups)], axis=0)
    acc = None
    for c in range(n_chunks):
        p = jnp.exp2(_dot(q, kt_ref[:, c * tk:(c + 1) * tk])).astype(BF16)
        pv = _dot(p, v_ref[c * tk:(c + 1) * tk, :])
        acc = pv if acc is None else acc + pv
    o = acc[:, :_HEAD_DIM] * (1.0 / acc[:, _HEAD_DIM:_HEAD_DIM + 1])
    for g in range(groups):
        o_ref[:, g * _HEAD_DIM:(g + 1) * _HEAD_DIM] = o[g * tq:(g + 1) * tq, :].astype(o_ref.dtype)


_SCORE_BOUND = 60.0
_KV_CHUNKS = (256, 128)


def _attn_bounded_call(q, kt, vx, n_kv):
    b, l, nq = q.shape
    s = kt.shape[3]
    groups = nq // (n_kv * _HEAD_DIM)
    gw = groups * _HEAD_DIM
    tq = _tile(l, 512)
    tk = next(c for c in _KV_CHUNKS if s % c == 0)
    n_chunks = s // tk
    vw = 2 * _HEAD_DIM
    return pl.pallas_call(
        functools.partial(_attn_bounded_kernel, tq=tq, tk=tk, groups=groups, n_chunks=n_chunks),
        grid=(b, n_kv, l // tq),
        in_specs=[
            pl.BlockSpec((None, tq, gw), lambda i, j, t: (i, t, j)),
            pl.BlockSpec((None, None, _HEAD_DIM, s), lambda i, j, t: (i, j, 0, 0)),
            pl.BlockSpec((None, s, vw), lambda i, j, t: (i, 0, j)),
        ],
        out_specs=pl.BlockSpec((None, tq, gw), lambda i, j, t: (i, t, j)),
        out_shape=jax.ShapeDtypeStruct((b, l, nq), BF16),
        compiler_params=_params(("parallel", "parallel", "arbitrary"), 56),
        name="gqa_attention_bounded",
    )(q, kt, vx)


def _attn_call(q, kt, vx, n_kv):
    b, l, nq = q.shape
    s = kt.shape[3]
    k = kt.transpose(0, 3, 1, 2).reshape(b, s, n_kv * _HEAD_DIM)
    v = vx.reshape(b, s, n_kv, 2 * _HEAD_DIM)[..., :_HEAD_DIM].reshape(b, s, n_kv * _HEAD_DIM)
    groups = nq // (n_kv * _HEAD_DIM)
    gw = groups * _HEAD_DIM
    tq = _tile(l, 256)
    tk = 768 if s % 768 == 0 else _tile(s, 512)
    n_chunks = s // tk
    m_rows = groups * tq
    return pl.pallas_call(
        functools.partial(_attn_kernel, tq=tq, tk=tk, groups=groups, n_chunks=n_chunks),
        grid=(b, n_kv, l // tq),
        in_specs=[
            pl.BlockSpec((None, tq, gw), lambda i, j, t: (i, t, j)),
            pl.BlockSpec((None, s, _HEAD_DIM), lambda i, j, t: (i, 0, j)),
            pl.BlockSpec((None, s, _HEAD_DIM), lambda i, j, t: (i, 0, j)),
        ],
        out_specs=pl.BlockSpec((None, tq, gw), lambda i, j, t: (i, t, j)),
        out_shape=jax.ShapeDtypeStruct((b, l, nq), BF16),
        scratch_shapes=[
            pltpu.VMEM((m_rows, _LANES), F32),
            pltpu.VMEM((m_rows, _LANES), F32),
            pltpu.VMEM((m_rows, _HEAD_DIM), F32),
        ],
        compiler_params=_params(("parallel", "parallel", "arbitrary"), 48),
        name="gqa_flash_attention",
    )(q, k, v)


_INFO_E0, _INFO_E1, _INFO_R0, _INFO_R1, _INFO_G0, _INFO_G1 = range(6)


def _route_kernel(o_ref, wo_ref, x_ref, g_ref, mod_ref, rw_ref, x1_ref, h_ref, info_ref, slots_ref, cnt_ref,
                  carry_ref, tri_ref, *, n_exp, tm):
    step = pl.program_id(0)

    ts = tri_ref.shape[0]

    @pl.when(step == 0)
    def _():
        carry_ref[...] = jnp.zeros_like(carry_ref)
        r = lax.broadcasted_iota(jnp.int32, (ts, ts), 0)
        cidx = lax.broadcasted_iota(jnp.int32, (ts, ts), 1)
        tri_ref[...] = jnp.where(r > cidx, 1.0, 0.0).astype(BF16)

    def sub_tile(rows, counted):
        x1 = x_ref[rows, :] + mod_ref[2:3, :] * _dot(o_ref[rows, :], wo_ref[...])
        x1_ref[rows, :] = x1
        h = _rms_mod(x1, g_ref[...], mod_ref[3:4, :], mod_ref[4:5, :])
        h_ref[rows] = pltpu.einshape("t(jc)->tjc", h, j=h_ref.shape[1])
        h_hi, h_lo = _split_bf16(h)
        both = _dot(h_hi, rw_ref[...])
        logits = both[:, :_LANES] + (both[:, _LANES:] + _dot(h_lo, rw_ref[:, :_LANES]))
        lane = lax.broadcasted_iota(jnp.int32, logits.shape, 1)
        neg = jnp.float32(-jnp.inf)
        lg = jnp.where(lane < n_exp, logits, neg)
        m0 = jnp.max(lg, axis=1, keepdims=True)
        i0 = jnp.min(jnp.where(lg == m0, lane, _LANES), axis=1, keepdims=True)
        oh0 = lane == i0
        lg1 = jnp.where(oh0, neg, lg)
        m1 = jnp.max(lg1, axis=1, keepdims=True)
        i1 = jnp.min(jnp.where(lg1 == m1, lane, _LANES), axis=1, keepdims=True)
        oh1 = lane == i1
        e = jnp.exp(m1 - m0)
        g0 = 1.0 / (1.0 + e)
        g1 = e * g0

        chosen = jnp.where(oh0 | oh1, 1.0, 0.0)
        before = _dot(tri_ref[...], chosen.astype(BF16)) + counted
        r0 = jnp.sum(jnp.where(oh0, before, 0.0), axis=1, keepdims=True)
        r1 = jnp.sum(jnp.where(oh1, before, 0.0), axis=1, keepdims=True)

        info = jnp.zeros(logits.shape, F32)
        for idx, val in ((_INFO_E0, i0.astype(F32)), (_INFO_E1, i1.astype(F32)), (_INFO_R0, r0), (_INFO_R1, r1),
                         (_INFO_G0, g0), (_INFO_G1, g1)):
            info = jnp.where(lane == idx, val, info)
        info_ref[rows, :] = info
        slots_ref[:, rows] = info.T[:_SUBLANES, :].astype(jnp.int32)
        return counted + jnp.sum(chosen, axis=0, keepdims=True)

    counted = carry_ref[...]
    for s in range(tm // ts):
        counted = sub_tile(slice(s * ts, (s + 1) * ts), counted)
    carry_ref[...] = counted
    cnt_ref[...] = counted


def _route_call(o, w_o, x, g, mod, router):
    b, l, d = x.shape
    k = o.shape[2]
    n_exp = router.shape[1]
    t = b * l
    tm = _tile(l, 512)
    per_b = l // tm
    rw = jnp.zeros((d, _LANES), F32).at[:, :n_exp].set(router)
    rw = jnp.concatenate(_split_bf16(rw), axis=1)
    ts = tm // 2 if tm % 256 == 0 else tm
    return pl.pallas_call(
        functools.partial(_route_kernel, n_exp=n_exp, tm=tm),
        grid=(t // tm,),
        in_specs=[
            pl.BlockSpec((tm, k), lambda i: (i, 0)),
            pl.BlockSpec((k, d), lambda i: (0, 0), pipeline_mode=pl.Buffered(1)),
            pl.BlockSpec((tm, d), lambda i: (i, 0)),
            pl.BlockSpec((1, d), lambda i: (0, 0)),
            pl.BlockSpec((None, 6, d), lambda i: (i // per_b, 0, 0)),
            pl.BlockSpec((d, 2 * _LANES), lambda i: (0, 0)),
        ],
        out_specs=[
            pl.BlockSpec((tm, d), lambda i: (i, 0)),
            pl.BlockSpec((tm, d // _LANES, _LANES), lambda i: (i, 0, 0)),
            pl.BlockSpec((tm, _LANES), lambda i: (i, 0)),
            pl.BlockSpec((_SUBLANES, tm), lambda i: (0, i)),
            pl.BlockSpec((1, _LANES), lambda i: (0, 0)),
        ],
        out_shape=[
            jax.ShapeDtypeStruct((t, d), F32),
            jax.ShapeDtypeStruct((t, d // _LANES, _LANES), F32),
            jax.ShapeDtypeStruct((t, _LANES), F32),
            jax.ShapeDtypeStruct((_SUBLANES, t), jnp.int32),
            jax.ShapeDtypeStruct((1, _LANES), F32),
        ],
        scratch_shapes=[pltpu.VMEM((1, _LANES), F32), pltpu.VMEM((ts, ts), BF16)],
        compiler_params=_params(("arbitrary",), 48),
        name="moe_route",
    )(o.reshape(t, k), w_o, x.reshape(t, d), g.reshape(1, d), mod, rw)


def _pos_tiles(pos0, pos1, tt):
    n = pos0.shape[0] // tt
    return jnp.stack([pos0.reshape(n, tt), pos1.reshape(n, tt)], axis=1).reshape(n * 2 * tt)


def _row_copy(src, src_row, dst, dst_row, sem):
    return pltpu.make_async_copy(src.at[pl.ds(src_row, 1)], dst.at[pl.ds(dst_row, 1)], sem)


_ROW_DMA_UNROLL = 8


def _dispatch_kernel(pos_ref, h_ref, xs_in_ref, xs_ref, pos_smem, sem_s, sem_d, *, tt):
    del xs_in_ref
    cp = pltpu.make_async_copy(pos_ref, pos_smem, sem_s)
    cp.start()
    cp.wait()

    def body(r, carry):
        _row_copy(h_ref, r, xs_ref, pos_smem[r], sem_d).start(priority=0)
        _row_copy(h_ref, r, xs_ref, pos_smem[tt + r], sem_d).start(priority=1)
        return carry

    lax.fori_loop(0, tt, body, 0, unroll=_ROW_DMA_UNROLL)
    for _ in range(2 * tt):
        _row_copy(h_ref, 0, xs_ref, 0, sem_d).wait()


def _dispatch_call(h, pos_t, p_rows, tt):
    t, nb, _ = h.shape
    xs0 = jnp.zeros((p_rows, nb, _LANES), F32)
    return pl.pallas_call(
        functools.partial(_dispatch_kernel, tt=tt),
        grid=(t // tt,),
        in_specs=[
            pl.BlockSpec((2 * tt,), lambda i: (i,)),
            pl.BlockSpec((tt, nb, _LANES), lambda i: (i, 0, 0)),
            pl.BlockSpec(memory_space=pl.ANY),
        ],
        out_specs=pl.BlockSpec(memory_space=pl.ANY),
        out_shape=jax.ShapeDtypeStruct((p_rows, nb, _LANES), F32),
        scratch_shapes=[
            pltpu.SMEM((2 * tt,), jnp.int32),
            pltpu.SemaphoreType.DMA(()),
            pltpu.SemaphoreType.DMA(()),
        ],
        input_output_aliases={2: 0},
        compiler_params=_params(("arbitrary",), 32),
        name="moe_dispatch",
    )(pos_t, h, xs0)


def _moe_kernel(be_ref, bs_ref, bv_ref, x_ref, w1_ref, w3_ref, w2_ref, o_ref, acc_ref):
    del be_ref, bs_ref
    i = pl.program_id(0)
    f = pl.program_id(1)

    @pl.when((i == 0) & (f == 0))
    def _():
        acc_ref[...] = jnp.zeros_like(acc_ref)

    @pl.when(bv_ref[i] > 0)
    def _():
        xb = pltpu.einshape("tjc->t(jc)", x_ref[...]).astype(BF16)
        h1 = _dot(xb, w1_ref[...])
        h3 = _dot(xb, w3_ref[...])
        hid = (h1 * _sigmoid(h1)) * h3
        acc = jnp.where(f == 0, 0.0, acc_ref[...]) + _dot(hid.astype(BF16), w2_ref[...])
        acc_ref[...] = acc
        o_ref[...] = pltpu.einshape("t(jc)->tjc", acc, j=o_ref.shape[1])


def _moe_call(blk_e, blk_src, blk_valid, xs, w1, w3, w2, tmm):
    p_rows, nb, _ = xs.shape
    d = nb * _LANES
    ff = w1.shape[2]
    tf = 1792 if ff % 1792 == 0 else _tile(ff, 512)
    nf = ff // tf
    nblk = p_rows // tmm

    def fsel(i, f, bv):
        return jnp.where(bv[i] > 0, f, nf - 1)

    grid_spec = pltpu.PrefetchScalarGridSpec(
        num_scalar_prefetch=3,
        grid=(nblk, nf),
        in_specs=[
            pl.BlockSpec((tmm, nb, _LANES), lambda i, f, be, bs, bv: (bs[i], 0, 0)),
            pl.BlockSpec((None, d, tf), lambda i, f, be, bs, bv: (be[i], 0, fsel(i, f, bv))),
            pl.BlockSpec((None, d, tf), lambda i, f, be, bs, bv: (be[i], 0, fsel(i, f, bv))),
            pl.BlockSpec((None, tf, d), lambda i, f, be, bs, bv: (be[i], fsel(i, f, bv), 0)),
        ],
        out_specs=pl.BlockSpec((tmm, nb, _LANES), lambda i, f, be, bs, bv: (bs[i], 0, 0)),
        scratch_shapes=[pltpu.VMEM((tmm, d), F32)],
    )
    return pl.pallas_call(
        _moe_kernel,
        grid_spec=grid_spec,
        out_shape=jax.ShapeDtypeStruct((p_rows, nb, _LANES), F32),
        compiler_params=_params(("arbitrary", "arbitrary"), 58),
        name="moe_experts",
    )(blk_e, blk_src, blk_valid, xs, w1, w3, w2)


def _combine_kernel(pos_ref, pos_next_ref, x_ref, mod_ref, info_ref, fg_ref, y_ref, o_ref, pos_smem, ybuf,
                    sem_s, sem_d, *, tt, n_steps):
    i = pl.program_id(0)
    slot = lax.rem(i, 2)

    def start_gathers(p_ref, s):
        cp = pltpu.make_async_copy(p_ref, pos_smem, sem_s)
        cp.start()
        cp.wait()

        def body(r, carry):
            _row_copy(y_ref, pos_smem[r], ybuf.at[s, 0], r, sem_d.at[s]).start(priority=0)
            _row_copy(y_ref, pos_smem[tt + r], ybuf.at[s, 1], r, sem_d.at[s]).start(priority=1)
            return carry

        lax.fori_loop(0, tt, body, 0, unroll=_ROW_DMA_UNROLL)

    @pl.when(i == 0)
    def _():
        start_gathers(pos_ref, 0)

    @pl.when(i + 1 < n_steps)
    def _():
        start_gathers(pos_next_ref, 1 - slot)

    for _ in range(2 * tt):
        _row_copy(y_ref, 0, ybuf.at[slot, 0], 0, sem_d.at[slot]).wait()

    info = info_ref[...]
    g0 = info[:, _INFO_G0:_INFO_G0 + 1]
    g1 = info[:, _INFO_G1:_INFO_G1 + 1]
    y0 = pltpu.einshape("tjc->t(jc)", ybuf[slot, 0])
    y1 = pltpu.einshape("tjc->t(jc)", ybuf[slot, 1])
    ffn = g0 * y0 + g1 * y1
    xo = x_ref[...] + mod_ref[5:6, :] * ffn
    o_ref[...] = xo * lax.rsqrt(jnp.mean(xo * xo, axis=-1, keepdims=True) + _EPS) * fg_ref[...]


def _combine_call(pos_t, x, mod, info, final_g, y, tt):
    b, l, d = x.shape
    t = b * l
    per_b = l // tt
    n_steps = t // tt
    return pl.pallas_call(
        functools.partial(_combine_kernel, tt=tt, n_steps=n_steps),
        grid=(n_steps,),
        in_specs=[
            pl.BlockSpec((2 * tt,), lambda i: (i,)),
            pl.BlockSpec((2 * tt,), lambda i: (jnp.minimum(i + 1, n_steps - 1),)),
            pl.BlockSpec((tt, d), lambda i: (i, 0)),
            pl.BlockSpec((None, 6, d), lambda i: (i // per_b, 0, 0)),
            pl.BlockSpec((tt, _LANES), lambda i: (i, 0)),
            pl.BlockSpec((1, d), lambda i: (0, 0)),
            pl.BlockSpec(memory_space=pl.ANY),
        ],
        out_specs=pl.BlockSpec((tt, d), lambda i: (i, 0)),
        out_shape=jax.ShapeDtypeStruct((t, d), F32),
        scratch_shapes=[
            pltpu.SMEM((2 * tt,), jnp.int32),
            pltpu.VMEM((2, 2, tt, d // _LANES, _LANES), F32),
            pltpu.SemaphoreType.DMA(()),
            pltpu.SemaphoreType.DMA((2,)),
        ],
        compiler_params=_params(("arbitrary",), 40),
        name="moe_combine_norm",
    )(pos_t, pos_t, x.reshape(t, d), mod, info, final_g.reshape(1, d), y).reshape(b, l, d)


def _moe_layer(o, w_o, x, g, mod, router, w1, w3, w2, final_g):
    b, l, d = x.shape
    t = b * l
    n_exp = router.shape[1]
    tmm = 512 if t >= 4096 else 128
    tt = _tile(l, 512)
    x, h, info, slots, counts = _route_call(o, w_o, x, g, mod, router)
    x = x.reshape(b, l, d)

    counts = counts[0, :n_exp].astype(jnp.int32)
    padded = ((counts + tmm - 1) // tmm) * tmm
    pend = jnp.cumsum(padded)
    gstart = pend - padded
    pos0 = gstart[slots[_INFO_E0]] + slots[_INFO_R0]
    pos1 = gstart[slots[_INFO_E1]] + slots[_INFO_R1]
    pos_t = _pos_tiles(pos0, pos1, tt)

    nblk = (t * 2) // tmm + n_exp
    p_rows = nblk * tmm
    bstart = jnp.arange(nblk, dtype=jnp.int32) * tmm
    used = bstart < pend[-1]
    last = pend[-1] // tmm - 1
    bidx = jnp.where(used, jnp.arange(nblk, dtype=jnp.int32), last)
    blk_e = jnp.minimum(jnp.sum(((bidx * tmm)[:, None] >= pend[None, :]).astype(jnp.int32), axis=1), n_exp - 1)
    blk_valid = jnp.where(used, jnp.clip(counts[blk_e] - (bidx * tmm - gstart[blk_e]), 0, tmm), 0).astype(jnp.int32)

    xs = _dispatch_call(h, pos_t, p_rows, tt)
    y = _moe_call(blk_e, bidx.astype(jnp.int32), blk_valid, xs, w1.astype(BF16), w3.astype(BF16), w2.astype(BF16),
                  tmm)
    return _combine_call(pos_t, x, mod, info, final_g, y, tt)


def kernel(x, c, ctx, c_ctx, ada_w, ada_b, norm1_g, norm2_g, rg_w_in, rg_conv_w, rg_conv_b, rg_w_a, rg_b_a,
           rg_w_i, rg_b_i, rg_lam, rg_w_out, attn_w_qkv, attn_q_g, attn_k_g, attn_w_o, ffn_w1, ffn_w3, ffn_w2,
           moe_router, moe_w1, moe_w3, moe_w2, final_g):
    b, l, d = x.shape
    n_ctx = ctx.shape[1]
    c_dim = rg_w_out.shape[1]
    n_q = attn_w_o.shape[1] // _HEAD_DIM
    n_kv = (attn_w_qkv.shape[2] // _HEAD_DIM - n_q) // 2

    rows = 16
    c_rows = jnp.zeros((rows, d), F32).at[:b].set(c).at[b].set(c_ctx)
    mod = _ada_call(c_rows, ada_w, ada_b).reshape(ada_w.shape[0], rows, 6, d)
    mod_l = [mod[i, :b] for i in range(2)]
    mod_c = [jnp.broadcast_to(mod[i, b], (b, 6, d)) for i in range(2)]

    w_in = rg_w_in[0].astype(BF16)
    w_out = rg_w_out[0].astype(BF16)
    w_a = rg_w_a[0].astype(BF16)
    w_i = rg_w_i[0].astype(BF16)
    xg_l = _nmm_call(x, norm1_g[0], mod_l[0], w_in, 0, 1, F32)
    xg_c = _nmm_call(ctx, norm1_g[0], mod_c[0], w_in, 0, 1, F32)
    zero_h = jnp.zeros((b, c_dim // _LANES, _LANES), F32)

    def lru(xg, h0, tail, direction, reverse):
        return _lru_call(xg, rg_conv_w[0], rg_conv_b[0], w_a[direction], rg_b_a[0, direction], w_i[direction],
                         rg_b_i[0, direction], rg_lam[0, direction], h0, tail, reverse=reverse)

    yc_rev, h0_rev = lru(xg_c, zero_h, None, 1, True)
    ctx, h0_fwd = lru(xg_c, zero_h, (yc_rev, w_out, ctx, mod_c[0]), 0, False)
    yl_rev, _ = lru(xg_l, h0_rev, None, 1, True)
    x, _ = lru(xg_l, h0_fwd, (yl_rev, w_out, x, mod_l[0]), 0, False)
    f1, f3, f2 = ffn_w1[0].astype(BF16), ffn_w3[0].astype(BF16), ffn_w2[0].astype(BF16)
    x = _ffn_call(x, norm2_g[0], mod_l[0], f1, f3, f2)
    ctx = _ffn_call(ctx, norm2_g[0], mod_c[0], f1, f3, f2)

    w_qkv = attn_w_qkv[0].astype(BF16)
    cos, sin = _rope_tables(l)
    q, k_l, v_l = _qkv_call(x, norm1_g[1], mod_l[1], w_qkv, attn_q_g[0], attn_k_g[0], cos, sin, n_q, n_kv)
    ones = jnp.ones((n_ctx, _HEAD_DIM), F32)
    _, k_c, v_c = _qkv_call(ctx, norm1_g[1], mod_c[1], w_qkv, attn_q_g[0], attn_k_g[0], ones, jnp.zeros_like(ones),
                            n_q, n_kv)
    k_all = jnp.concatenate([k_c, k_l], axis=3)
    v_all = jnp.concatenate([v_c, v_l], axis=1)
    score_bound = _HEAD_DIM * jnp.max(jnp.abs(attn_q_g[0])) * jnp.max(jnp.abs(attn_k_g[0])) * _Q_SCALE
    o = lax.cond(score_bound <= _SCORE_BOUND,
                 lambda: _attn_bounded_call(q, k_all, v_all, n_kv),
                 lambda: _attn_call(q, k_all, v_all, n_kv))
    return _moe_layer(o, attn_w_o[0].astype(BF16), x, norm2_g[1], mod_l[1], moe_router[0], moe_w1[0], moe_w3[0],
                      moe_w2[0], final_g)
```

```python
import functools

import jax
import jax.numpy as jnp
from jax import lax
from jax.experimental import pallas as pl
from jax.experimental.pallas import tpu as pltpu

F32 = jnp.float32
BF16 = jnp.bfloat16

_EPS = 1e-6
_HEAD_DIM = 128
_GRID_W = 64
_ROPE_THETA = 10000.0
_ROPE_NFREQ = _HEAD_DIM // 4
_LRU_C = 8.0
_CONV_W = 4
_LANES = 128
_SUBLANES = 8
_LOG2E = 1.4426950408889634
_MIB = 1024 * 1024


def _params(semantics, vmem_mib):
    return pltpu.CompilerParams(dimension_semantics=semantics, vmem_limit_bytes=vmem_mib * _MIB)


def _tile(n, pref):
    if n <= pref:
        return n
    t = pref
    while n % t:
        t //= 2
    return t


def _dot(a, b):
    return jnp.dot(a, b, preferred_element_type=F32)


def _split_bf16(a):
    hi = a.astype(BF16)
    lo = (a - hi.astype(F32)).astype(BF16)
    return hi, lo


def _dot3(a, b):
    ah, al = _split_bf16(a)
    bh, bl = _split_bf16(b)
    return _dot(ah, bh) + (_dot(al, bh) + _dot(ah, bl))


def _rms_mod(x, g, shift, scale):
    y = x * lax.rsqrt(jnp.mean(x * x, axis=-1, keepdims=True) + _EPS)
    return (y * g) * (1.0 + scale) + shift


def _sigmoid(x):
    return 1.0 / (1.0 + jnp.exp(-x))


def _ada_kernel(c_ref, w_ref, b_ref, o_ref):
    c = c_ref[...]
    s = c * _sigmoid(c)
    o_ref[...] = _dot3(s, w_ref[...]) + b_ref[...]


def _ada_call(c_rows, ada_w, ada_b):
    depth, d, n = ada_w.shape
    rows = c_rows.shape[0]
    tn = _tile(n, 1536)
    return pl.pallas_call(
        _ada_kernel,
        grid=(depth, n // tn),
        in_specs=[
            pl.BlockSpec((rows, d), lambda i, j: (0, 0)),
            pl.BlockSpec((None, d, tn), lambda i, j: (i, 0, j)),
            pl.BlockSpec((None, 1, tn), lambda i, j: (i, 0, j)),
        ],
        out_specs=pl.BlockSpec((None, rows, tn), lambda i, j: (i, 0, j)),
        out_shape=jax.ShapeDtypeStruct((depth, rows, n), F32),
        compiler_params=_params(("parallel", "parallel"), 40),
        name="ada_mod",
    )(c_rows, ada_w, ada_b.reshape(depth, 1, n))


def _nmm_kernel(x_ref, g_ref, mod_ref, w_ref, o_ref, *, sh, sc):
    h = _rms_mod(x_ref[...], g_ref[...], mod_ref[sh:sh + 1, :], mod_ref[sc:sc + 1, :])
    o_ref[...] = _dot(h.astype(BF16), w_ref[...]).astype(o_ref.dtype)


def _nmm_call(x, g, mod, w, sh, sc, out_dtype):
    b, l, d = x.shape
    n = w.shape[1]
    tm = _tile(l, 512)
    return pl.pallas_call(
        functools.partial(_nmm_kernel, sh=sh, sc=sc),
        grid=(b, l // tm),
        in_specs=[
            pl.BlockSpec((None, tm, d), lambda i, t: (i, t, 0)),
            pl.BlockSpec((1, d), lambda i, t: (0, 0)),
            pl.BlockSpec((None, 6, d), lambda i, t: (i, 0, 0)),
            pl.BlockSpec((d, n), lambda i, t: (0, 0)),
        ],
        out_specs=pl.BlockSpec((None, tm, n), lambda i, t: (i, t, 0)),
        out_shape=jax.ShapeDtypeStruct((b, l, n), out_dtype),
        compiler_params=_params(("parallel", "parallel"), 48),
        name="norm_mod_matmul",
    )(x, g.reshape(1, d), mod, w)


def _log1p_pos(e):
    u = 1.0 + e
    den = jnp.where(u == 1.0, 1.0, u - 1.0)
    return jnp.where(u == 1.0, e, jnp.log(u) * (e * (1.0 / den)))


def _softplus(x):
    return jnp.maximum(x, 0.0) + _log1p_pos(jnp.exp(-jnp.abs(x)))


def _sigmoid_tanh(x):
    return 0.5 * jnp.tanh(0.5 * x) + 0.5


def _scan_rows(a, b, h, reverse):
    tl = a.shape[0]
    nb = a.shape[1] // _LANES
    a3 = pltpu.einshape("t(jc)->tjc", a, j=nb)
    b3 = pltpu.einshape("t(jc)->tjc", b, j=nb)
    states = [None] * tl
    for t in (range(tl - 1, -1, -1) if reverse else range(tl)):
        h = a3[t] * h + b3[t]
        states[t] = h
    return pltpu.einshape("tjc->t(jc)", jnp.stack(states, axis=0)), h


def _lru_kernel(*refs, reverse, tl, nt, fuse):
    if fuse:
        (xm_ref, xp_ref, xn_ref, cw_ref, cb_ref, wa_ref, ba_ref, wi_ref, bi_ref, lam_ref, h0_ref,
         yo_ref, gb_ref, wo_ref, res_ref, mod_ref, out_ref, hl_ref, carry_ref) = refs
    else:
        (xm_ref, xp_ref, xn_ref, cw_ref, cb_ref, wa_ref, ba_ref, wi_ref, bi_ref, lam_ref, h0_ref,
         out_ref, hl_ref, carry_ref) = refs
    t = pl.program_id(1)
    tt = (nt - 1 - t) if reverse else t

    @pl.when(t == 0)
    def _():
        carry_ref[...] = h0_ref[...]

    xm = xm_ref[...]
    prev = jnp.where(tt > 0, xp_ref[...], 0.0)
    nxt = jnp.where(tt < nt - 1, xn_ref[...], 0.0)
    row8 = lax.broadcasted_iota(jnp.int32, prev.shape, 0)

    def tap(offset):
        if offset < 0:
            s = -offset
            y = pltpu.roll(xm, s, axis=0)
            head = jnp.where(row8 < s, pltpu.roll(prev, s, axis=0), y[:_SUBLANES, :])
            return jnp.concatenate([head, y[_SUBLANES:, :]], axis=0)
        y = pltpu.roll(xm, tl - offset, axis=0)
        tail = jnp.where(row8 >= _SUBLANES - offset, pltpu.roll(nxt, _SUBLANES - offset, axis=0),
                         y[tl - _SUBLANES:, :])
        return jnp.concatenate([y[:tl - _SUBLANES, :], tail], axis=0)

    cw = cw_ref[...]
    left = _CONV_W // 2
    xc = cb_ref[...]
    for k in range(_CONV_W):
        xk = xm if k == left else tap(k - left)
        xc = xc + xk * cw[k:k + 1, :]

    xcb = xc.astype(BF16)
    nb = wa_ref.shape[0]
    blocks = [slice(j * _LANES, (j + 1) * _LANES) for j in range(nb)]
    pre_r = jnp.concatenate([_dot(xcb[:, s], wa_ref[j]) for j, s in enumerate(blocks)], axis=1)
    pre_i = jnp.concatenate([_dot(xcb[:, s], wi_ref[j]) for j, s in enumerate(blocks)], axis=1)
    r = _sigmoid_tanh(pre_r + ba_ref[...])
    i = _sigmoid_tanh(pre_i + bi_ref[...])
    decay = (_LRU_C * r) * _softplus(-lam_ref[...])
    a = jnp.exp(-decay)
    mult = jnp.sqrt(jnp.tanh(decay) * (1.0 + a * a))
    b = mult * (i * xc)

    hs, new_carry = _scan_rows(a, b, carry_ref[...], reverse)
    carry_ref[...] = new_carry
    hl_ref[...] = new_carry
    if fuse:
        gate = jax.nn.gelu(gb_ref[...], approximate=True)
        z = ((hs + yo_ref[...]) * gate).astype(BF16)
        out_ref[...] = res_ref[...] + mod_ref[2:3, :] * _dot(z, wo_ref[...])
    else:
        out_ref[...] = hs


def _lru_call(xg, conv_w, conv_b, w_a, b_a, w_i, b_i, lam, h0, tail=None, *, reverse):
    b, l, c2 = xg.shape
    c = c2 // 2
    nb = c // _LANES
    tl = _tile(l, 256)
    nt = l // tl
    hb = tl // _SUBLANES
    nh = l // _SUBLANES
    fuse = tail is not None

    def tmap(t):
        return (nt - 1 - t) if reverse else t

    vec = lambda: pl.BlockSpec((1, c), lambda i, t: (0, 0))
    gate_w = lambda: pl.BlockSpec((nb, _LANES, _LANES), lambda i, t: (0, 0, 0))
    in_specs = [
        pl.BlockSpec((None, tl, c), lambda i, t: (i, tmap(t), 0)),
        pl.BlockSpec((None, _SUBLANES, c), lambda i, t: (i, jnp.maximum(tmap(t) * hb - 1, 0), 0)),
        pl.BlockSpec((None, _SUBLANES, c), lambda i, t: (i, jnp.minimum((tmap(t) + 1) * hb, nh - 1), 0)),
        pl.BlockSpec((_CONV_W, c), lambda i, t: (0, 0)),
        vec(),
        gate_w(),
        vec(),
        gate_w(),
        vec(),
        vec(),
        pl.BlockSpec((None, nb, _LANES), lambda i, t: (i, 0, 0)),
    ]
    args = [xg, xg, xg, conv_w, conv_b.reshape(1, c), w_a, b_a.reshape(1, c), w_i, b_i.reshape(1, c),
            lam.reshape(1, c), h0]
    d_out = c
    if fuse:
        y_other, w_out, res, mod = tail
        d_out = w_out.shape[1]
        in_specs += [
            pl.BlockSpec((None, tl, c), lambda i, t: (i, tmap(t), 0)),
            pl.BlockSpec((None, tl, c), lambda i, t: (i, tmap(t), 1)),
            pl.BlockSpec((c, d_out), lambda i, t: (0, 0), pipeline_mode=pl.Buffered(1)),
            pl.BlockSpec((None, tl, d_out), lambda i, t: (i, tmap(t), 0)),
            pl.BlockSpec((None, 6, d_out), lambda i, t: (i, 0, 0)),
        ]
        args += [y_other, xg, w_out, res, mod]
    return pl.pallas_call(
        functools.partial(_lru_kernel, reverse=reverse, tl=tl, nt=nt, fuse=fuse),
        grid=(b, nt),
        in_specs=in_specs,
        out_specs=[
            pl.BlockSpec((None, tl, d_out), lambda i, t: (i, tmap(t), 0)),
            pl.BlockSpec((None, nb, _LANES), lambda i, t: (i, 0, 0)),
        ],
        out_shape=[jax.ShapeDtypeStruct((b, l, d_out), F32), jax.ShapeDtypeStruct((b, nb, _LANES), F32)],
        scratch_shapes=[pltpu.VMEM((nb, _LANES), F32)],
        compiler_params=_params(("parallel", "arbitrary"), 48),
        name="lru_rev" if reverse else "lru_fwd",
    )(*args)


def _ffn_kernel(x_ref, g_ref, mod_ref, w1_ref, w3_ref, w2_ref, o_ref):
    x = x_ref[...]
    h = _rms_mod(x, g_ref[...], mod_ref[3:4, :], mod_ref[4:5, :]).astype(BF16)
    h1 = _dot(h, w1_ref[...])
    h3 = _dot(h, w3_ref[...])
    hid = (h1 * _sigmoid(h1)) * h3
    o_ref[...] = x + mod_ref[5:6, :] * _dot(hid.astype(BF16), w2_ref[...])


def _ffn_call(x, g, mod, w1, w3, w2):
    b, l, d = x.shape
    ff = w1.shape[1]
    tm = _tile(l, 512)
    resident = pl.Buffered(1)
    return pl.pallas_call(
        _ffn_kernel,
        grid=(b, l // tm),
        in_specs=[
            pl.BlockSpec((None, tm, d), lambda i, t: (i, t, 0)),
            pl.BlockSpec((1, d), lambda i, t: (0, 0)),
            pl.BlockSpec((None, 6, d), lambda i, t: (i, 0, 0)),
            pl.BlockSpec((d, ff), lambda i, t: (0, 0), pipeline_mode=resident),
            pl.BlockSpec((d, ff), lambda i, t: (0, 0), pipeline_mode=resident),
            pl.BlockSpec((ff, d), lambda i, t: (0, 0), pipeline_mode=resident),
        ],
        out_specs=pl.BlockSpec((None, tm, d), lambda i, t: (i, t, 0)),
        out_shape=jax.ShapeDtypeStruct((b, l, d), F32),
        compiler_params=_params(("parallel", "parallel"), 56),
        name="dense_swiglu",
    )(x, g.reshape(1, d), mod, w1, w3, w2)


def _rope_tables(l):
    pos = jnp.arange(l)
    row = (pos // _GRID_W).astype(F32)
    col = (pos % _GRID_W).astype(F32)
    freqs = _ROPE_THETA ** (-jnp.arange(_ROPE_NFREQ, dtype=F32) / _ROPE_NFREQ)
    ar = row[:, None] * freqs
    ac = col[:, None] * freqs
    cos = jnp.concatenate([jnp.cos(ar), jnp.cos(ar), jnp.cos(ac), jnp.cos(ac)], axis=-1)
    sin = jnp.concatenate([-jnp.sin(ar), jnp.sin(ar), -jnp.sin(ac), jnp.sin(ac)], axis=-1)
    return cos, sin


def _qkv_kernel(x_ref, g_ref, mod_ref, w_ref, qg_ref, kg_ref, cos_ref, sin_ref, q_ref, k_ref, v_ref,
                *, n_q, n_kv, q_scale):
    h = _rms_mod(x_ref[...], g_ref[...], mod_ref[0:1, :], mod_ref[1:2, :])
    qkv = _dot(h.astype(BF16), w_ref[...])
    cos = cos_ref[...]
    sin = sin_ref[...]
    lane = lax.broadcasted_iota(jnp.int32, cos.shape, 1)
    first = (lane % (2 * _ROPE_NFREQ)) < _ROPE_NFREQ

    def norm_rope(v, gain):
        y = v * lax.rsqrt(jnp.mean(v * v, axis=-1, keepdims=True) + _EPS) * gain
        partner = jnp.where(first, pltpu.roll(y, _HEAD_DIM - _ROPE_NFREQ, axis=1),
                            pltpu.roll(y, _ROPE_NFREQ, axis=1))
        return y * cos + partner * sin

    for hh in range(n_q):
        sl = slice(hh * _HEAD_DIM, (hh + 1) * _HEAD_DIM)
        q_ref[:, sl] = (norm_rope(qkv[:, sl], qg_ref[...]) * q_scale).astype(q_ref.dtype)
    for hh in range(n_kv):
        src = slice((n_q + hh) * _HEAD_DIM, (n_q + hh + 1) * _HEAD_DIM)
        k_ref[hh] = norm_rope(qkv[:, src], kg_ref[...]).T.astype(k_ref.dtype)
    ones_col = jnp.where(lane == 0, 1.0, 0.0).astype(v_ref.dtype)
    for hh in range(n_kv):
        src = slice((n_q + n_kv + hh) * _HEAD_DIM, (n_q + n_kv + hh + 1) * _HEAD_DIM)
        v_ref[:, 2 * hh * _HEAD_DIM:(2 * hh + 1) * _HEAD_DIM] = qkv[:, src].astype(v_ref.dtype)
        v_ref[:, (2 * hh + 1) * _HEAD_DIM:(2 * hh + 2) * _HEAD_DIM] = ones_col


_Q_SCALE = (_HEAD_DIM ** -0.5) * _LOG2E


def _qkv_call(x, g, mod, w, q_g, k_g, cos, sin, n_q, n_kv):
    b, l, d = x.shape
    n = w.shape[1]
    tm = _tile(l, 512)
    nq = n_q * _HEAD_DIM
    nkv = n_kv * _HEAD_DIM
    q_scale = _Q_SCALE
    return pl.pallas_call(
        functools.partial(_qkv_kernel, n_q=n_q, n_kv=n_kv, q_scale=q_scale),
        grid=(b, l // tm),
        in_specs=[
            pl.BlockSpec((None, tm, d), lambda i, t: (i, t, 0)),
            pl.BlockSpec((1, d), lambda i, t: (0, 0)),
            pl.BlockSpec((None, 6, d), lambda i, t: (i, 0, 0)),
            pl.BlockSpec((d, n), lambda i, t: (0, 0)),
            pl.BlockSpec((1, _HEAD_DIM), lambda i, t: (0, 0)),
            pl.BlockSpec((1, _HEAD_DIM), lambda i, t: (0, 0)),
            pl.BlockSpec((tm, _HEAD_DIM), lambda i, t: (t, 0)),
            pl.BlockSpec((tm, _HEAD_DIM), lambda i, t: (t, 0)),
        ],
        out_specs=[
            pl.BlockSpec((None, tm, nq), lambda i, t: (i, t, 0)),
            pl.BlockSpec((None, n_kv, _HEAD_DIM, tm), lambda i, t: (i, 0, 0, t)),
            pl.BlockSpec((None, tm, 2 * nkv), lambda i, t: (i, t, 0)),
        ],
        out_shape=[
            jax.ShapeDtypeStruct((b, l, nq), BF16),
            jax.ShapeDtypeStruct((b, n_kv, _HEAD_DIM, l), BF16),
            jax.ShapeDtypeStruct((b, l, 2 * nkv), BF16),
        ],
        compiler_params=_params(("parallel", "parallel"), 48),
        name="qkv_norm_rope",
    )(x, g.reshape(1, d), mod, w, q_g.reshape(1, _HEAD_DIM), k_g.reshape(1, _HEAD_DIM), cos, sin)


def _lane_tile(x, reps):
    return x if reps == 1 else jnp.concatenate([x] * reps, axis=1)


def _attn_kernel(q_ref, k_ref, v_ref, o_ref, m_ref, l_ref, acc_ref, *, tq, tk, groups, n_chunks):
    q = jnp.concatenate([q_ref[:, g * _HEAD_DIM:(g + 1) * _HEAD_DIM] for g in range(groups)], axis=0)
    m_ref[...] = jnp.full_like(m_ref, -jnp.inf)
    l_ref[...] = jnp.zeros_like(l_ref)
    acc_ref[...] = jnp.zeros_like(acc_ref)

    def body(c, carry):
        start = pl.multiple_of(c * tk, tk)
        k = k_ref[pl.ds(start, tk), :]
        v = v_ref[pl.ds(start, tk), :]
        s = lax.dot_general(q, k, (((1,), (1,)), ((), ())), preferred_element_type=F32)
        m_prev = m_ref[...]
        m_next = jnp.maximum(m_prev, jnp.max(s, axis=1, keepdims=True))
        alpha = jnp.exp2(m_prev - m_next)
        p = jnp.exp2(s - _lane_tile(m_next, tk // _LANES))
        l_ref[...] = alpha * l_ref[...] + jnp.sum(p, axis=1, keepdims=True)
        acc_ref[...] = alpha * acc_ref[...] + _dot(p.astype(BF16), v)
        m_ref[...] = m_next
        return carry

    lax.fori_loop(0, n_chunks, body, 0)
    o = acc_ref[...] * (1.0 / l_ref[...])
    for g in range(groups):
        o_ref[:, g * _HEAD_DIM:(g + 1) * _HEAD_DIM] = o[g * tq:(g + 1) * tq, :].astype(o_ref.dtype)


def _attn_bounded_kernel(q_ref, kt_ref, v_ref, *rest, tq, tk, groups, n_chunks):
    n_side = (len(rest) - 1) // 2
    side_in, o_ref, side_out = rest[:n_side], rest[n_side], rest[n_side + 1:]
    for src, dst in zip(side_in, side_out):
        dst[...] = src[...].astype(dst.dtype)
    q = jnp.concatenate([q_ref[:, g * _HEAD_DIM:(g + 1) * _HEAD_DIM] for g in range(groups)], axis=0)
    acc = None
    for c in range(n_chunks):
        p = jnp.exp2(_dot(q, kt_ref[:, c * tk:(c + 1) * tk])).astype(BF16)
        pv = _dot(p, v_ref[c * tk:(c + 1) * tk, :])
        acc = pv if acc is None else acc + pv
    o = acc[:, :_HEAD_DIM] * (1.0 / acc[:, _HEAD_DIM:_HEAD_DIM + 1])
    for g in range(groups):
        o_ref[:, g * _HEAD_DIM:(g + 1) * _HEAD_DIM] = o[g * tq:(g + 1) * tq, :].astype(o_ref.dtype)


_SCORE_BOUND = 60.0
_KV_CHUNKS = (256, 128)


def _side_cast_ok(side, steps):
    return all(a.shape[0] % steps == 0 and (a.shape[0] // steps) % (2 * _SUBLANES) == 0 for a in side)


def _attn_bounded_call(q, kt, vx, n_kv, side=()):
    b, l, nq = q.shape
    s = kt.shape[3]
    groups = nq // (n_kv * _HEAD_DIM)
    gw = groups * _HEAD_DIM
    tq = _tile(l, 512)
    nt = l // tq
    tk = next(c for c in _KV_CHUNKS if s % c == 0)
    n_chunks = s // tk
    vw = 2 * _HEAD_DIM
    steps = b * n_kv * nt
    side_spec = lambda a: pl.BlockSpec((a.shape[0] // steps, a.shape[1]),
                                       lambda i, j, t: ((i * n_kv + j) * nt + t, 0))
    return pl.pallas_call(
        functools.partial(_attn_bounded_kernel, tq=tq, tk=tk, groups=groups, n_chunks=n_chunks),
        grid=(b, n_kv, nt),
        in_specs=[
            pl.BlockSpec((None, tq, gw), lambda i, j, t: (i, t, j)),
            pl.BlockSpec((None, None, _HEAD_DIM, s), lambda i, j, t: (i, j, 0, 0)),
            pl.BlockSpec((None, s, vw), lambda i, j, t: (i, 0, j)),
        ] + [side_spec(a) for a in side],
        out_specs=[pl.BlockSpec((None, tq, gw), lambda i, j, t: (i, t, j))] + [side_spec(a) for a in side],
        out_shape=[jax.ShapeDtypeStruct((b, l, nq), BF16)] + [jax.ShapeDtypeStruct(a.shape, BF16) for a in side],
        compiler_params=_params(("parallel", "parallel", "arbitrary"), 56),
        name="gqa_attention_bounded",
    )(q, kt, vx, *side)


def _attn_call(q, kt, vx, n_kv):
    b, l, nq = q.shape
    s = kt.shape[3]
    k = kt.transpose(0, 3, 1, 2).reshape(b, s, n_kv * _HEAD_DIM)
    v = vx.reshape(b, s, n_kv, 2 * _HEAD_DIM)[..., :_HEAD_DIM].reshape(b, s, n_kv * _HEAD_DIM)
    groups = nq // (n_kv * _HEAD_DIM)
    gw = groups * _HEAD_DIM
    tq = _tile(l, 256)
    tk = 768 if s % 768 == 0 else _tile(s, 512)
    n_chunks = s // tk
    m_rows = groups * tq
    return pl.pallas_call(
        functools.partial(_attn_kernel, tq=tq, tk=tk, groups=groups, n_chunks=n_chunks),
        grid=(b, n_kv, l // tq),
        in_specs=[
            pl.BlockSpec((None, tq, gw), lambda i, j, t: (i, t, j)),
            pl.BlockSpec((None, s, _HEAD_DIM), lambda i, j, t: (i, 0, j)),
            pl.BlockSpec((None, s, _HEAD_DIM), lambda i, j, t: (i, 0, j)),
        ],
        out_specs=pl.BlockSpec((None, tq, gw), lambda i, j, t: (i, t, j)),
        out_shape=jax.ShapeDtypeStruct((b, l, nq), BF16),
        scratch_shapes=[
            pltpu.VMEM((m_rows, _LANES), F32),
            pltpu.VMEM((m_rows, _LANES), F32),
            pltpu.VMEM((m_rows, _HEAD_DIM), F32),
        ],
        compiler_params=_params(("parallel", "parallel", "arbitrary"), 48),
        name="gqa_flash_attention",
    )(q, k, v)


_INFO_E0, _INFO_E1, _INFO_R0, _INFO_R1, _INFO_G0, _INFO_G1 = range(6)


def _route_kernel(o_ref, wo_ref, x_ref, g_ref, mod_ref, rw_ref, x1_ref, h_ref, info_ref, slots_ref, cnt_ref,
                  carry_ref, tri_ref, *, n_exp, tm):
    step = pl.program_id(0)

    ts = tri_ref.shape[0]

    @pl.when(step == 0)
    def _():
        carry_ref[...] = jnp.zeros_like(carry_ref)
        r = lax.broadcasted_iota(jnp.int32, (ts, ts), 0)
        cidx = lax.broadcasted_iota(jnp.int32, (ts, ts), 1)
        tri_ref[...] = jnp.where(r > cidx, 1.0, 0.0).astype(BF16)

    def sub_tile(rows, counted):
        x1 = x_ref[rows, :] + mod_ref[2:3, :] * _dot(o_ref[rows, :], wo_ref[...])
        x1_ref[rows, :] = x1
        h = _rms_mod(x1, g_ref[...], mod_ref[3:4, :], mod_ref[4:5, :])
        h_ref[rows] = pltpu.einshape("t(jc)->tjc", h, j=h_ref.shape[1])
        h_hi, h_lo = _split_bf16(h)
        both = _dot(h_hi, rw_ref[...])
        logits = both[:, :_LANES] + (both[:, _LANES:] + _dot(h_lo, rw_ref[:, :_LANES]))
        lane = lax.broadcasted_iota(jnp.int32, logits.shape, 1)
        neg = jnp.float32(-jnp.inf)
        lg = jnp.where(lane < n_exp, logits, neg)
        m0 = jnp.max(lg, axis=1, keepdims=True)
        i0 = jnp.min(jnp.where(lg == m0, lane, _LANES), axis=1, keepdims=True)
        oh0 = lane == i0
        lg1 = jnp.where(oh0, neg, lg)
        m1 = jnp.max(lg1, axis=1, keepdims=True)
        i1 = jnp.min(jnp.where(lg1 == m1, lane, _LANES), axis=1, keepdims=True)
        oh1 = lane == i1
        e = jnp.exp(m1 - m0)
        g0 = 1.0 / (1.0 + e)
        g1 = e * g0

        chosen = jnp.where(oh0 | oh1, 1.0, 0.0)
        before = _dot(tri_ref[...], chosen.astype(BF16)) + counted
        r0 = jnp.sum(jnp.where(oh0, before, 0.0), axis=1, keepdims=True)
        r1 = jnp.sum(jnp.where(oh1, before, 0.0), axis=1, keepdims=True)

        info = jnp.zeros(logits.shape, F32)
        for idx, val in ((_INFO_E0, i0.astype(F32)), (_INFO_E1, i1.astype(F32)), (_INFO_R0, r0), (_INFO_R1, r1),
                         (_INFO_G0, g0), (_INFO_G1, g1)):
            info = jnp.where(lane == idx, val, info)
        info_ref[rows, :] = info
        slots_ref[:, rows] = info.T[:_SUBLANES, :].astype(jnp.int32)
        return counted + jnp.sum(chosen, axis=0, keepdims=True)

    counted = carry_ref[...]
    for s in range(tm // ts):
        counted = sub_tile(slice(s * ts, (s + 1) * ts), counted)
    carry_ref[...] = counted
    cnt_ref[...] = counted


def _route_call(o, w_o, x, g, mod, router):
    b, l, d = x.shape
    k = o.shape[2]
    n_exp = router.shape[1]
    t = b * l
    tm = _tile(l, 512)
    per_b = l // tm
    rw = jnp.zeros((d, _LANES), F32).at[:, :n_exp].set(router)
    rw = jnp.concatenate(_split_bf16(rw), axis=1)
    ts = tm // 2 if tm % 256 == 0 else tm
    return pl.pallas_call(
        functools.partial(_route_kernel, n_exp=n_exp, tm=tm),
        grid=(t // tm,),
        in_specs=[
            pl.BlockSpec((tm, k), lambda i: (i, 0)),
            pl.BlockSpec((k, d), lambda i: (0, 0), pipeline_mode=pl.Buffered(1)),
            pl.BlockSpec((tm, d), lambda i: (i, 0)),
            pl.BlockSpec((1, d), lambda i: (0, 0)),
            pl.BlockSpec((None, 6, d), lambda i: (i // per_b, 0, 0)),
            pl.BlockSpec((d, 2 * _LANES), lambda i: (0, 0)),
        ],
        out_specs=[
            pl.BlockSpec((tm, d), lambda i: (i, 0)),
            pl.BlockSpec((tm, d // _LANES, _LANES), lambda i: (i, 0, 0)),
            pl.BlockSpec((tm, _LANES), lambda i: (i, 0)),
            pl.BlockSpec((_SUBLANES, tm), lambda i: (0, i)),
            pl.BlockSpec((1, _LANES), lambda i: (0, 0)),
        ],
        out_shape=[
            jax.ShapeDtypeStruct((t, d), F32),
            jax.ShapeDtypeStruct((t, d // _LANES, _LANES), F32),
            jax.ShapeDtypeStruct((t, _LANES), F32),
            jax.ShapeDtypeStruct((_SUBLANES, t), jnp.int32),
            jax.ShapeDtypeStruct((1, _LANES), F32),
        ],
        scratch_shapes=[pltpu.VMEM((1, _LANES), F32), pltpu.VMEM((ts, ts), BF16)],
        compiler_params=_params(("arbitrary",), 48),
        name="moe_route",
    )(o.reshape(t, k), w_o, x.reshape(t, d), g.reshape(1, d), mod, rw)


def _pos_tiles(pos0, pos1, tt):
    n = pos0.shape[0] // tt
    return jnp.stack([pos0.reshape(n, tt), pos1.reshape(n, tt)], axis=1).reshape(n * 2 * tt)


def _row_copy(src, src_row, dst, dst_row, sem):
    return pltpu.make_async_copy(src.at[pl.ds(src_row, 1)], dst.at[pl.ds(dst_row, 1)], sem)


_ROW_DMA_UNROLL = 8


def _dispatch_kernel(pos_ref, h_ref, xs_in_ref, xs_ref, pos_smem, sem_s, sem_d, *, tt):
    del xs_in_ref
    cp = pltpu.make_async_copy(pos_ref, pos_smem, sem_s)
    cp.start()
    cp.wait()

    def body(r, carry):
        _row_copy(h_ref, r, xs_ref, pos_smem[r], sem_d).start(priority=0)
        _row_copy(h_ref, r, xs_ref, pos_smem[tt + r], sem_d).start(priority=1)
        return carry

    lax.fori_loop(0, tt, body, 0, unroll=_ROW_DMA_UNROLL)
    for _ in range(2 * tt):
        _row_copy(h_ref, 0, xs_ref, 0, sem_d).wait()


def _dispatch_call(h, pos_t, p_rows, tt):
    t, nb, _ = h.shape
    xs0 = jnp.zeros((p_rows, nb, _LANES), F32)
    return pl.pallas_call(
        functools.partial(_dispatch_kernel, tt=tt),
        grid=(t // tt,),
        in_specs=[
            pl.BlockSpec((2 * tt,), lambda i: (i,)),
            pl.BlockSpec((tt, nb, _LANES), lambda i: (i, 0, 0)),
            pl.BlockSpec(memory_space=pl.ANY),
        ],
        out_specs=pl.BlockSpec(memory_space=pl.ANY),
        out_shape=jax.ShapeDtypeStruct((p_rows, nb, _LANES), F32),
        scratch_shapes=[
            pltpu.SMEM((2 * tt,), jnp.int32),
            pltpu.SemaphoreType.DMA(()),
            pltpu.SemaphoreType.DMA(()),
        ],
        input_output_aliases={2: 0},
        compiler_params=_params(("arbitrary",), 32),
        name="moe_dispatch",
    )(pos_t, h, xs0)


def _moe_kernel(be_ref, bs_ref, bv_ref, x_ref, w1_ref, w3_ref, w2_ref, o_ref, acc_ref):
    del be_ref, bs_ref
    i = pl.program_id(0)
    f = pl.program_id(1)

    @pl.when((i == 0) & (f == 0))
    def _():
        acc_ref[...] = jnp.zeros_like(acc_ref)

    @pl.when(bv_ref[i] > 0)
    def _():
        xb = pltpu.einshape("tjc->t(jc)", x_ref[...]).astype(BF16)
        h1 = _dot(xb, w1_ref[...])
        h3 = _dot(xb, w3_ref[...])
        hid = (h1 * _sigmoid(h1)) * h3
        acc = jnp.where(f == 0, 0.0, acc_ref[...]) + _dot(hid.astype(BF16), w2_ref[...])
        acc_ref[...] = acc
        o_ref[...] = pltpu.einshape("t(jc)->tjc", acc, j=o_ref.shape[1])


def _moe_call(blk_e, blk_src, blk_valid, xs, w1, w3, w2, tmm):
    p_rows, nb, _ = xs.shape
    d = nb * _LANES
    ff = w1.shape[2]
    tf = 1792 if ff % 1792 == 0 else _tile(ff, 512)
    nf = ff // tf
    nblk = p_rows // tmm

    def fsel(i, f, bv):
        return jnp.where(bv[i] > 0, f, nf - 1)

    grid_spec = pltpu.PrefetchScalarGridSpec(
        num_scalar_prefetch=3,
        grid=(nblk, nf),
        in_specs=[
            pl.BlockSpec((tmm, nb, _LANES), lambda i, f, be, bs, bv: (bs[i], 0, 0)),
            pl.BlockSpec((None, d, tf), lambda i, f, be, bs, bv: (be[i], 0, fsel(i, f, bv))),
            pl.BlockSpec((None, d, tf), lambda i, f, be, bs, bv: (be[i], 0, fsel(i, f, bv))),
            pl.BlockSpec((None, tf, d), lambda i, f, be, bs, bv: (be[i], fsel(i, f, bv), 0)),
        ],
        out_specs=pl.BlockSpec((tmm, nb, _LANES), lambda i, f, be, bs, bv: (bs[i], 0, 0)),
        scratch_shapes=[pltpu.VMEM((tmm, d), F32)],
    )
    return pl.pallas_call(
        _moe_kernel,
        grid_spec=grid_spec,
        out_shape=jax.ShapeDtypeStruct((p_rows, nb, _LANES), F32),
        compiler_params=_params(("arbitrary", "arbitrary"), 58),
        name="moe_experts",
    )(blk_e, blk_src, blk_valid, xs, w1, w3, w2)


def _combine_kernel(pos_ref, pos_next_ref, x_ref, mod_ref, info_ref, fg_ref, y_ref, o_ref, pos_smem, ybuf,
                    sem_s, sem_d, *, tt, n_steps):
    i = pl.program_id(0)
    slot = lax.rem(i, 2)

    def start_gathers(p_ref, s):
        cp = pltpu.make_async_copy(p_ref, pos_smem, sem_s)
        cp.start()
        cp.wait()

        def body(r, carry):
            _row_copy(y_ref, pos_smem[r], ybuf.at[s, 0], r, sem_d.at[s]).start(priority=0)
            _row_copy(y_ref, pos_smem[tt + r], ybuf.at[s, 1], r, sem_d.at[s]).start(priority=1)
            return carry

        lax.fori_loop(0, tt, body, 0, unroll=_ROW_DMA_UNROLL)

    @pl.when(i == 0)
    def _():
        start_gathers(pos_ref, 0)

    @pl.when(i + 1 < n_steps)
    def _():
        start_gathers(pos_next_ref, 1 - slot)

    for _ in range(2 * tt):
        _row_copy(y_ref, 0, ybuf.at[slot, 0], 0, sem_d.at[slot]).wait()

    info = info_ref[...]
    g0 = info[:, _INFO_G0:_INFO_G0 + 1]
    g1 = info[:, _INFO_G1:_INFO_G1 + 1]
    y0 = pltpu.einshape("tjc->t(jc)", ybuf[slot, 0])
    y1 = pltpu.einshape("tjc->t(jc)", ybuf[slot, 1])
    ffn = g0 * y0 + g1 * y1
    xo = x_ref[...] + mod_ref[5:6, :] * ffn
    o_ref[...] = xo * lax.rsqrt(jnp.mean(xo * xo, axis=-1, keepdims=True) + _EPS) * fg_ref[...]


def _combine_call(pos_t, x, mod, info, final_g, y, tt):
    b, l, d = x.shape
    t = b * l
    per_b = l // tt
    n_steps = t // tt
    return pl.pallas_call(
        functools.partial(_combine_kernel, tt=tt, n_steps=n_steps),
        grid=(n_steps,),
        in_specs=[
            pl.BlockSpec((2 * tt,), lambda i: (i,)),
            pl.BlockSpec((2 * tt,), lambda i: (jnp.minimum(i + 1, n_steps - 1),)),
            pl.BlockSpec((tt, d), lambda i: (i, 0)),
            pl.BlockSpec((None, 6, d), lambda i: (i // per_b, 0, 0)),
            pl.BlockSpec((tt, _LANES), lambda i: (i, 0)),
            pl.BlockSpec((1, d), lambda i: (0, 0)),
            pl.BlockSpec(memory_space=pl.ANY),
        ],
        out_specs=pl.BlockSpec((tt, d), lambda i: (i, 0)),
        out_shape=jax.ShapeDtypeStruct((t, d), F32),
        scratch_shapes=[
            pltpu.SMEM((2 * tt,), jnp.int32),
            pltpu.VMEM((2, 2, tt, d // _LANES, _LANES), F32),
            pltpu.SemaphoreType.DMA(()),
            pltpu.SemaphoreType.DMA((2,)),
        ],
        compiler_params=_params(("arbitrary",), 40),
        name="moe_combine_norm",
    )(pos_t, pos_t, x.reshape(t, d), mod, info, final_g.reshape(1, d), y).reshape(b, l, d)


def _moe_layer(o, w_o, x, g, mod, router, w1, w3, w2, final_g):
    b, l, d = x.shape
    t = b * l
    n_exp = router.shape[1]
    tmm = 512 if t >= 4096 else 128
    tt = _tile(l, 512)
    x, h, info, slots, counts = _route_call(o, w_o, x, g, mod, router)
    x = x.reshape(b, l, d)

    counts = counts[0, :n_exp].astype(jnp.int32)
    padded = ((counts + tmm - 1) // tmm) * tmm
    pend = jnp.cumsum(padded)
    gstart = pend - padded
    pos0 = gstart[slots[_INFO_E0]] + slots[_INFO_R0]
    pos1 = gstart[slots[_INFO_E1]] + slots[_INFO_R1]
    pos_t = _pos_tiles(pos0, pos1, tt)

    nblk = (t * 2) // tmm + n_exp
    p_rows = nblk * tmm
    bstart = jnp.arange(nblk, dtype=jnp.int32) * tmm
    used = bstart < pend[-1]
    last = pend[-1] // tmm - 1
    bidx = jnp.where(used, jnp.arange(nblk, dtype=jnp.int32), last)
    blk_e = jnp.minimum(jnp.sum(((bidx * tmm)[:, None] >= pend[None, :]).astype(jnp.int32), axis=1), n_exp - 1)
    blk_valid = jnp.where(used, jnp.clip(counts[blk_e] - (bidx * tmm - gstart[blk_e]), 0, tmm), 0).astype(jnp.int32)

    xs = _dispatch_call(h, pos_t, p_rows, tt)
    y = _moe_call(blk_e, bidx.astype(jnp.int32), blk_valid, xs, w1, w3, w2, tmm)
    return _combine_call(pos_t, x, mod, info, final_g, y, tt)


def kernel(x, c, ctx, c_ctx, ada_w, ada_b, norm1_g, norm2_g, rg_w_in, rg_conv_w, rg_conv_b, rg_w_a, rg_b_a,
           rg_w_i, rg_b_i, rg_lam, rg_w_out, attn_w_qkv, attn_q_g, attn_k_g, attn_w_o, ffn_w1, ffn_w3, ffn_w2,
           moe_router, moe_w1, moe_w3, moe_w2, final_g):
    b, l, d = x.shape
    n_ctx = ctx.shape[1]
    c_dim = rg_w_out.shape[1]
    n_q = attn_w_o.shape[1] // _HEAD_DIM
    n_kv = (attn_w_qkv.shape[2] // _HEAD_DIM - n_q) // 2

    rows = 16
    c_rows = jnp.zeros((rows, d), F32).at[:b].set(c).at[b].set(c_ctx)
    mod = _ada_call(c_rows, ada_w, ada_b).reshape(ada_w.shape[0], rows, 6, d)
    mod_l = [mod[i, :b] for i in range(2)]
    mod_c = [jnp.broadcast_to(mod[i, b], (b, 6, d)) for i in range(2)]

    w_in = rg_w_in[0].astype(BF16)
    w_out = rg_w_out[0].astype(BF16)
    w_a = rg_w_a[0].astype(BF16)
    w_i = rg_w_i[0].astype(BF16)
    xg_l = _nmm_call(x, norm1_g[0], mod_l[0], w_in, 0, 1, F32)
    xg_c = _nmm_call(ctx, norm1_g[0], mod_c[0], w_in, 0, 1, F32)
    zero_h = jnp.zeros((b, c_dim // _LANES, _LANES), F32)

    def lru(xg, h0, tail, direction, reverse):
        return _lru_call(xg, rg_conv_w[0], rg_conv_b[0], w_a[direction], rg_b_a[0, direction], w_i[direction],
                         rg_b_i[0, direction], rg_lam[0, direction], h0, tail, reverse=reverse)

    yc_rev, h0_rev = lru(xg_c, zero_h, None, 1, True)
    ctx, h0_fwd = lru(xg_c, zero_h, (yc_rev, w_out, ctx, mod_c[0]), 0, False)
    yl_rev, _ = lru(xg_l, h0_rev, None, 1, True)
    x, _ = lru(xg_l, h0_fwd, (yl_rev, w_out, x, mod_l[0]), 0, False)
    f1, f3, f2 = ffn_w1[0].astype(BF16), ffn_w3[0].astype(BF16), ffn_w2[0].astype(BF16)
    x = _ffn_call(x, norm2_g[0], mod_l[0], f1, f3, f2)
    ctx = _ffn_call(ctx, norm2_g[0], mod_c[0], f1, f3, f2)

    w_qkv = attn_w_qkv[0].astype(BF16)
    cos, sin = _rope_tables(l)
    q, k_l, v_l = _qkv_call(x, norm1_g[1], mod_l[1], w_qkv, attn_q_g[0], attn_k_g[0], cos, sin, n_q, n_kv)
    ones = jnp.ones((n_ctx, _HEAD_DIM), F32)
    _, k_c, v_c = _qkv_call(ctx, norm1_g[1], mod_c[1], w_qkv, attn_q_g[0], attn_k_g[0], ones, jnp.zeros_like(ones),
                            n_q, n_kv)
    k_all = jnp.concatenate([k_c, k_l], axis=3)
    v_all = jnp.concatenate([v_c, v_l], axis=1)
    score_bound = _HEAD_DIM * jnp.max(jnp.abs(attn_q_g[0])) * jnp.max(jnp.abs(attn_k_g[0])) * _Q_SCALE
    ew = [w.reshape(-1, w.shape[-1]) for w in (moe_w1[0], moe_w3[0], moe_w2[0])]
    steps = b * n_kv * (l // _tile(l, 512))
    side = tuple(ew) if _side_cast_ok(ew, steps) else ()
    res = lax.cond(score_bound <= _SCORE_BOUND,
                   lambda: tuple(_attn_bounded_call(q, k_all, v_all, n_kv, side)),
                   lambda: (_attn_call(q, k_all, v_all, n_kv),) + tuple(w.astype(BF16) for w in side))
    o = res[0]
    ew = list(res[1:]) if side else [w.astype(BF16) for w in ew]
    ew = [w.reshape(m.shape) for w, m in zip(ew, (moe_w1[0], moe_w3[0], moe_w2[0]))]
    return _moe_layer(o, attn_w_o[0].astype(BF16), x, norm2_g[1], mod_l[1], moe_router[0], ew[0], ew[1], ew[2],
                      final_g)
```

```python
import functools

import jax
import jax.numpy as jnp
import numpy as np
from jax import lax
from jax.experimental import pallas as pl
from jax.experimental.pallas import tpu as pltpu

F32 = jnp.float32
BF16 = jnp.bfloat16

_EPS = 1e-6
_HEAD_DIM = 128
_GRID_W = 64
_ROPE_THETA = 10000.0
_ROPE_NFREQ = _HEAD_DIM // 4
_LRU_C = 8.0
_CONV_W = 4
_LANES = 128
_SUBLANES = 8
_LOG2E = 1.4426950408889634
_MIB = 1024 * 1024


def _params(semantics, vmem_mib):
    return pltpu.CompilerParams(dimension_semantics=semantics, vmem_limit_bytes=vmem_mib * _MIB)


def _tile(n, pref):
    if n <= pref:
        return n
    t = pref
    while n % t:
        t //= 2
    return t


def _dot(a, b):
    return jnp.dot(a, b, preferred_element_type=F32)


def _split_bf16(a):
    hi = a.astype(BF16)
    lo = (a - hi.astype(F32)).astype(BF16)
    return hi, lo


def _dot3(a, b):
    ah, al = _split_bf16(a)
    bh, bl = _split_bf16(b)
    return _dot(ah, bh) + (_dot(al, bh) + _dot(ah, bl))


def _rms_mod(x, g, shift, scale):
    y = x * lax.rsqrt(jnp.mean(x * x, axis=-1, keepdims=True) + _EPS)
    return (y * g) * (1.0 + scale) + shift


def _sigmoid(x):
    return 1.0 / (1.0 + jnp.exp(-x))


def _ada_kernel(c_ref, w_ref, b_ref, o_ref):
    c = c_ref[...]
    s = c * _sigmoid(c)
    o_ref[...] = _dot3(s, w_ref[...]) + b_ref[...]


def _ada_call(c_rows, ada_w, ada_b):
    depth, d, n = ada_w.shape
    rows = c_rows.shape[0]
    tn = _tile(n, 1536)
    return pl.pallas_call(
        _ada_kernel,
        grid=(depth, n // tn),
        in_specs=[
            pl.BlockSpec((rows, d), lambda i, j: (0, 0)),
            pl.BlockSpec((None, d, tn), lambda i, j: (i, 0, j)),
            pl.BlockSpec((None, 1, tn), lambda i, j: (i, 0, j)),
        ],
        out_specs=pl.BlockSpec((None, rows, tn), lambda i, j: (i, 0, j)),
        out_shape=jax.ShapeDtypeStruct((depth, rows, n), F32),
        compiler_params=_params(("parallel", "parallel"), 40),
        name="ada_mod",
    )(c_rows, ada_w, ada_b.reshape(depth, 1, n))


def _nmm_kernel(x_ref, g_ref, mod_ref, w_ref, o_ref, *, sh, sc):
    h = _rms_mod(x_ref[...], g_ref[...], mod_ref[sh:sh + 1, :], mod_ref[sc:sc + 1, :])
    o_ref[...] = _dot(h.astype(BF16), w_ref[...]).astype(o_ref.dtype)


def _nmm_call(x, g, mod, w, sh, sc, out_dtype):
    b, l, d = x.shape
    n = w.shape[1]
    tm = _tile(l, 512)
    return pl.pallas_call(
        functools.partial(_nmm_kernel, sh=sh, sc=sc),
        grid=(b, l // tm),
        in_specs=[
            pl.BlockSpec((None, tm, d), lambda i, t: (i, t, 0)),
            pl.BlockSpec((1, d), lambda i, t: (0, 0)),
            pl.BlockSpec((None, 6, d), lambda i, t: (i, 0, 0)),
            pl.BlockSpec((d, n), lambda i, t: (0, 0)),
        ],
        out_specs=pl.BlockSpec((None, tm, n), lambda i, t: (i, t, 0)),
        out_shape=jax.ShapeDtypeStruct((b, l, n), out_dtype),
        compiler_params=_params(("parallel", "parallel"), 48),
        name="norm_mod_matmul",
    )(x, g.reshape(1, d), mod, w)


def _log1p_pos(e):
    u = 1.0 + e
    den = jnp.where(u == 1.0, 1.0, u - 1.0)
    return jnp.where(u == 1.0, e, jnp.log(u) * (e * (1.0 / den)))


def _softplus(x):
    return jnp.maximum(x, 0.0) + _log1p_pos(jnp.exp(-jnp.abs(x)))


def _sigmoid_tanh(x):
    return 0.5 * jnp.tanh(0.5 * x) + 0.5


def _scan_rows(a, b, h, reverse):
    tl = a.shape[0]
    nb = a.shape[1] // _LANES
    a3 = pltpu.einshape("t(jc)->tjc", a, j=nb)
    b3 = pltpu.einshape("t(jc)->tjc", b, j=nb)
    states = [None] * tl
    for t in (range(tl - 1, -1, -1) if reverse else range(tl)):
        h = a3[t] * h + b3[t]
        states[t] = h
    return pltpu.einshape("tjc->t(jc)", jnp.stack(states, axis=0)), h


def _lru_kernel(*refs, reverse, tl, nt, fuse):
    if fuse:
        (xm_ref, xp_ref, xn_ref, cw_ref, cb_ref, wa_ref, ba_ref, wi_ref, bi_ref, lam_ref, h0_ref,
         yo_ref, gb_ref, wo_ref, res_ref, mod_ref, out_ref, hl_ref, carry_ref) = refs
    else:
        (xm_ref, xp_ref, xn_ref, cw_ref, cb_ref, wa_ref, ba_ref, wi_ref, bi_ref, lam_ref, h0_ref,
         out_ref, hl_ref, carry_ref) = refs
    t = pl.program_id(1)
    tt = (nt - 1 - t) if reverse else t

    @pl.when(t == 0)
    def _():
        carry_ref[...] = h0_ref[...]

    xm = xm_ref[...]
    prev = jnp.where(tt > 0, xp_ref[...], 0.0)
    nxt = jnp.where(tt < nt - 1, xn_ref[...], 0.0)
    row8 = lax.broadcasted_iota(jnp.int32, prev.shape, 0)

    def tap(offset):
        if offset < 0:
            s = -offset
            y = pltpu.roll(xm, s, axis=0)
            head = jnp.where(row8 < s, pltpu.roll(prev, s, axis=0), y[:_SUBLANES, :])
            return jnp.concatenate([head, y[_SUBLANES:, :]], axis=0)
        y = pltpu.roll(xm, tl - offset, axis=0)
        tail = jnp.where(row8 >= _SUBLANES - offset, pltpu.roll(nxt, _SUBLANES - offset, axis=0),
                         y[tl - _SUBLANES:, :])
        return jnp.concatenate([y[:tl - _SUBLANES, :], tail], axis=0)

    cw = cw_ref[...]
    left = _CONV_W // 2
    xc = cb_ref[...]
    for k in range(_CONV_W):
        xk = xm if k == left else tap(k - left)
        xc = xc + xk * cw[k:k + 1, :]

    xcb = xc.astype(BF16)
    nb = wa_ref.shape[0]
    blocks = [slice(j * _LANES, (j + 1) * _LANES) for j in range(nb)]
    pre_r = jnp.concatenate([_dot(xcb[:, s], wa_ref[j]) for j, s in enumerate(blocks)], axis=1)
    pre_i = jnp.concatenate([_dot(xcb[:, s], wi_ref[j]) for j, s in enumerate(blocks)], axis=1)
    r = _sigmoid_tanh(pre_r + ba_ref[...])
    i = _sigmoid_tanh(pre_i + bi_ref[...])
    decay = (_LRU_C * r) * _softplus(-lam_ref[...])
    a = jnp.exp(-decay)
    mult = jnp.sqrt(jnp.tanh(decay) * (1.0 + a * a))
    b = mult * (i * xc)

    hs, new_carry = _scan_rows(a, b, carry_ref[...], reverse)
    carry_ref[...] = new_carry
    hl_ref[...] = new_carry
    if fuse:
        gate = jax.nn.gelu(gb_ref[...], approximate=True)
        z = ((hs + yo_ref[...]) * gate).astype(BF16)
        out_ref[...] = res_ref[...] + mod_ref[2:3, :] * _dot(z, wo_ref[...])
    else:
        out_ref[...] = hs


def _lru_call(xg, conv_w, conv_b, w_a, b_a, w_i, b_i, lam, h0, tail=None, *, reverse):
    b, l, c2 = xg.shape
    c = c2 // 2
    nb = c // _LANES
    tl = _tile(l, 256)
    nt = l // tl
    hb = tl // _SUBLANES
    nh = l // _SUBLANES
    fuse = tail is not None

    def tmap(t):
        return (nt - 1 - t) if reverse else t

    vec = lambda: pl.BlockSpec((1, c), lambda i, t: (0, 0))
    gate_w = lambda: pl.BlockSpec((nb, _LANES, _LANES), lambda i, t: (0, 0, 0))
    in_specs = [
        pl.BlockSpec((None, tl, c), lambda i, t: (i, tmap(t), 0)),
        pl.BlockSpec((None, _SUBLANES, c), lambda i, t: (i, jnp.maximum(tmap(t) * hb - 1, 0), 0)),
        pl.BlockSpec((None, _SUBLANES, c), lambda i, t: (i, jnp.minimum((tmap(t) + 1) * hb, nh - 1), 0)),
        pl.BlockSpec((_CONV_W, c), lambda i, t: (0, 0)),
        vec(),
        gate_w(),
        vec(),
        gate_w(),
        vec(),
        vec(),
        pl.BlockSpec((None, nb, _LANES), lambda i, t: (i, 0, 0)),
    ]
    args = [xg, xg, xg, conv_w, conv_b.reshape(1, c), w_a, b_a.reshape(1, c), w_i, b_i.reshape(1, c),
            lam.reshape(1, c), h0]
    d_out = c
    if fuse:
        y_other, w_out, res, mod = tail
        d_out = w_out.shape[1]
        in_specs += [
            pl.BlockSpec((None, tl, c), lambda i, t: (i, tmap(t), 0)),
            pl.BlockSpec((None, tl, c), lambda i, t: (i, tmap(t), 1)),
            pl.BlockSpec((c, d_out), lambda i, t: (0, 0), pipeline_mode=pl.Buffered(1)),
            pl.BlockSpec((None, tl, d_out), lambda i, t: (i, tmap(t), 0)),
            pl.BlockSpec((None, 6, d_out), lambda i, t: (i, 0, 0)),
        ]
        args += [y_other, xg, w_out, res, mod]
    return pl.pallas_call(
        functools.partial(_lru_kernel, reverse=reverse, tl=tl, nt=nt, fuse=fuse),
        grid=(b, nt),
        in_specs=in_specs,
        out_specs=[
            pl.BlockSpec((None, tl, d_out), lambda i, t: (i, tmap(t), 0)),
            pl.BlockSpec((None, nb, _LANES), lambda i, t: (i, 0, 0)),
        ],
        out_shape=[jax.ShapeDtypeStruct((b, l, d_out), F32), jax.ShapeDtypeStruct((b, nb, _LANES), F32)],
        scratch_shapes=[pltpu.VMEM((nb, _LANES), F32)],
        compiler_params=_params(("parallel", "arbitrary"), 48),
        name="lru_rev" if reverse else "lru_fwd",
    )(*args)


def _ffn_kernel(x_ref, g_ref, mod_ref, w1_ref, w3_ref, w2_ref, o_ref, *fill_refs):
    x = x_ref[...]
    h = _rms_mod(x, g_ref[...], mod_ref[3:4, :], mod_ref[4:5, :]).astype(BF16)
    h1 = _dot(h, w1_ref[...])
    h3 = _dot(h, w3_ref[...])
    hid = (h1 * _sigmoid(h1)) * h3
    o_ref[...] = x + mod_ref[5:6, :] * _dot(hid.astype(BF16), w2_ref[...])
    for fill_ref in fill_refs:
        fill_ref[...] = jnp.zeros_like(fill_ref)


def _ffn_call(x, g, mod, w1, w3, w2, zero_fill=None):
    b, l, d = x.shape
    ff = w1.shape[1]
    tm = _tile(l, 512)
    steps = b * (l // tm)
    resident = pl.Buffered(1)
    out_specs = [pl.BlockSpec((None, tm, d), lambda i, t: (i, t, 0))]
    out_shape = [jax.ShapeDtypeStruct((b, l, d), F32)]
    if zero_fill is not None:
        rows = zero_fill[0] // steps
        out_specs.append(pl.BlockSpec((rows,) + tuple(zero_fill[1:]), lambda i, t: (i * (l // tm) + t, 0, 0)))
        out_shape.append(jax.ShapeDtypeStruct(tuple(zero_fill), F32))
    res = pl.pallas_call(
        _ffn_kernel,
        grid=(b, l // tm),
        in_specs=[
            pl.BlockSpec((None, tm, d), lambda i, t: (i, t, 0)),
            pl.BlockSpec((1, d), lambda i, t: (0, 0)),
            pl.BlockSpec((None, 6, d), lambda i, t: (i, 0, 0)),
            pl.BlockSpec((d, ff), lambda i, t: (0, 0), pipeline_mode=resident),
            pl.BlockSpec((d, ff), lambda i, t: (0, 0), pipeline_mode=resident),
            pl.BlockSpec((ff, d), lambda i, t: (0, 0), pipeline_mode=resident),
        ],
        out_specs=out_specs,
        out_shape=out_shape,
        compiler_params=_params(("parallel", "parallel"), 56),
        name="dense_swiglu",
    )(x, g.reshape(1, d), mod, w1, w3, w2)
    return res[0] if zero_fill is None else tuple(res)


_ROPE_PERM = np.concatenate([np.arange(0, 32), np.arange(64, 96), np.arange(32, 64), np.arange(96, 128)])


def _rope_tables(l):
    pos = np.arange(l)
    row = (pos // _GRID_W).astype(np.float64)
    col = (pos % _GRID_W).astype(np.float64)
    freqs = _ROPE_THETA ** (-np.arange(_ROPE_NFREQ, dtype=np.float64) / _ROPE_NFREQ)
    ar = row[:, None] * freqs
    ac = col[:, None] * freqs
    cos = np.concatenate([np.cos(ar), np.cos(ac), np.cos(ar), np.cos(ac)], axis=-1)
    sin = np.concatenate([-np.sin(ar), -np.sin(ac), np.sin(ar), np.sin(ac)], axis=-1)
    return jnp.asarray(cos, F32), jnp.asarray(sin, F32)


def _permute_qk_heads(w_qkv, q_g, k_g, n_q, n_kv):
    cols = np.arange(w_qkv.shape[1])
    for hh in range(n_q + n_kv):
        cols[hh * _HEAD_DIM:(hh + 1) * _HEAD_DIM] = hh * _HEAD_DIM + _ROPE_PERM
    return w_qkv[:, cols], q_g[_ROPE_PERM], k_g[_ROPE_PERM]


def _qkv_kernel(x_ref, g_ref, mod_ref, w_ref, qg_ref, kg_ref, cos_ref, sin_ref, q_ref, k_ref, v_ref,
                *, n_q, n_kv, q_scale):
    tm = x_ref.shape[0]
    ts = tm // 2 if tm % (4 * _LANES) == 0 else tm
    for s0 in range(0, tm, ts):
        rows = slice(s0, s0 + ts)
        h = _rms_mod(x_ref[rows, :], g_ref[...], mod_ref[0:1, :], mod_ref[1:2, :])
        qkv = _dot(h.astype(BF16), w_ref[...])
        cos = cos_ref[rows, :]
        sin = sin_ref[rows, :]
        lane = lax.broadcasted_iota(jnp.int32, cos.shape, 1)

        def norm_rope(v, gain):
            y = v * lax.rsqrt(jnp.mean(v * v, axis=-1, keepdims=True) + _EPS) * gain
            return y * cos + pltpu.roll(y, _HEAD_DIM // 2, axis=1) * sin

        for hh in range(n_q):
            sl = slice(hh * _HEAD_DIM, (hh + 1) * _HEAD_DIM)
            q_ref[rows, sl] = (norm_rope(qkv[:, sl], qg_ref[...]) * q_scale).astype(q_ref.dtype)
        for hh in range(n_kv):
            src = slice((n_q + hh) * _HEAD_DIM, (n_q + hh + 1) * _HEAD_DIM)
            k_ref[hh, :, rows] = norm_rope(qkv[:, src], kg_ref[...]).T.astype(k_ref.dtype)
        ones_col = jnp.where(lane == 0, 1.0, 0.0).astype(v_ref.dtype)
        for hh in range(n_kv):
            src = slice((n_q + n_kv + hh) * _HEAD_DIM, (n_q + n_kv + hh + 1) * _HEAD_DIM)
            v_ref[rows, 2 * hh * _HEAD_DIM:(2 * hh + 1) * _HEAD_DIM] = qkv[:, src].astype(v_ref.dtype)
            v_ref[rows, (2 * hh + 1) * _HEAD_DIM:(2 * hh + 2) * _HEAD_DIM] = ones_col


_Q_SCALE = (_HEAD_DIM ** -0.5) * _LOG2E


def _qkv_call(x, g, mod, w, q_g, k_g, cos, sin, n_q, n_kv):
    b, l, d = x.shape
    n = w.shape[1]
    tm = _tile(l, 512)
    nq = n_q * _HEAD_DIM
    nkv = n_kv * _HEAD_DIM
    q_scale = _Q_SCALE
    return pl.pallas_call(
        functools.partial(_qkv_kernel, n_q=n_q, n_kv=n_kv, q_scale=q_scale),
        grid=(b, l // tm),
        in_specs=[
            pl.BlockSpec((None, tm, d), lambda i, t: (i, t, 0)),
            pl.BlockSpec((1, d), lambda i, t: (0, 0)),
            pl.BlockSpec((None, 6, d), lambda i, t: (i, 0, 0)),
            pl.BlockSpec((d, n), lambda i, t: (0, 0)),
            pl.BlockSpec((1, _HEAD_DIM), lambda i, t: (0, 0)),
            pl.BlockSpec((1, _HEAD_DIM), lambda i, t: (0, 0)),
            pl.BlockSpec((tm, _HEAD_DIM), lambda i, t: (t, 0)),
            pl.BlockSpec((tm, _HEAD_DIM), lambda i, t: (t, 0)),
        ],
        out_specs=[
            pl.BlockSpec((None, tm, nq), lambda i, t: (i, t, 0)),
            pl.BlockSpec((None, n_kv, _HEAD_DIM, tm), lambda i, t: (i, 0, 0, t)),
            pl.BlockSpec((None, tm, 2 * nkv), lambda i, t: (i, t, 0)),
        ],
        out_shape=[
            jax.ShapeDtypeStruct((b, l, nq), BF16),
            jax.ShapeDtypeStruct((b, n_kv, _HEAD_DIM, l), BF16),
            jax.ShapeDtypeStruct((b, l, 2 * nkv), BF16),
        ],
        compiler_params=_params(("parallel", "parallel"), 48),
        name="qkv_norm_rope",
    )(x, g.reshape(1, d), mod, w, q_g.reshape(1, _HEAD_DIM), k_g.reshape(1, _HEAD_DIM), cos, sin)


def _lane_tile(x, reps):
    return x if reps == 1 else jnp.concatenate([x] * reps, axis=1)


def _attn_kernel(q_ref, k_ref, v_ref, o_ref, m_ref, l_ref, acc_ref, *, tq, tk, groups, n_chunks):
    q = jnp.concatenate([q_ref[:, g * _HEAD_DIM:(g + 1) * _HEAD_DIM] for g in range(groups)], axis=0)
    m_ref[...] = jnp.full_like(m_ref, -jnp.inf)
    l_ref[...] = jnp.zeros_like(l_ref)
    acc_ref[...] = jnp.zeros_like(acc_ref)

    def body(c, carry):
        start = pl.multiple_of(c * tk, tk)
        k = k_ref[pl.ds(start, tk), :]
        v = v_ref[pl.ds(start, tk), :]
        s = lax.dot_general(q, k, (((1,), (1,)), ((), ())), preferred_element_type=F32)
        m_prev = m_ref[...]
        m_next = jnp.maximum(m_prev, jnp.max(s, axis=1, keepdims=True))
        alpha = jnp.exp2(m_prev - m_next)
        p = jnp.exp2(s - _lane_tile(m_next, tk // _LANES))
        l_ref[...] = alpha * l_ref[...] + jnp.sum(p, axis=1, keepdims=True)
        acc_ref[...] = alpha * acc_ref[...] + _dot(p.astype(BF16), v)
        m_ref[...] = m_next
        return carry

    lax.fori_loop(0, n_chunks, body, 0)
    o = acc_ref[...] * (1.0 / l_ref[...])
    for g in range(groups):
        o_ref[:, g * _HEAD_DIM:(g + 1) * _HEAD_DIM] = o[g * tq:(g + 1) * tq, :].astype(o_ref.dtype)


def _attn_bounded_kernel(q_ref, kt_ref, v_ref, *rest, tq, tk, groups, n_chunks):
    n_side = (len(rest) - 1) // 2
    side_in, o_ref, side_out = rest[:n_side], rest[n_side], rest[n_side + 1:]
    for src, dst in zip(side_in, side_out):
        dst[...] = src[...].astype(dst.dtype)
    q = jnp.concatenate([q_ref[:, g * _HEAD_DIM:(g + 1) * _HEAD_DIM] for g in range(groups)], axis=0)
    acc = None
    for c in range(n_chunks):
        p = jnp.exp2(_dot(q, kt_ref[:, c * tk:(c + 1) * tk])).astype(BF16)
        pv = _dot(p, v_ref[c * tk:(c + 1) * tk, :])
        acc = pv if acc is None else acc + pv
    o = acc[:, :_HEAD_DIM] * (1.0 / acc[:, _HEAD_DIM:_HEAD_DIM + 1])
    for g in range(groups):
        o_ref[:, g * _HEAD_DIM:(g + 1) * _HEAD_DIM] = o[g * tq:(g + 1) * tq, :].astype(o_ref.dtype)


_SCORE_BOUND = 60.0
_KV_CHUNKS = (256, 128)


def _side_cast_ok(side, steps):
    return all(a.shape[0] % steps == 0 and (a.shape[0] // steps) % (2 * _SUBLANES) == 0 for a in side)


def _attn_bounded_call(q, kt, vx, n_kv, side=()):
    b, l, nq = q.shape
    s = kt.shape[3]
    groups = nq // (n_kv * _HEAD_DIM)
    gw = groups * _HEAD_DIM
    tq = _tile(l, 512)
    nt = l // tq
    tk = next(c for c in _KV_CHUNKS if s % c == 0)
    n_chunks = s // tk
    vw = 2 * _HEAD_DIM
    steps = b * n_kv * nt
    side_spec = lambda a: pl.BlockSpec((a.shape[0] // steps, a.shape[1]),
                                       lambda i, j, t: ((i * n_kv + j) * nt + t, 0))
    return pl.pallas_call(
        functools.partial(_attn_bounded_kernel, tq=tq, tk=tk, groups=groups, n_chunks=n_chunks),
        grid=(b, n_kv, nt),
        in_specs=[
            pl.BlockSpec((None, tq, gw), lambda i, j, t: (i, t, j)),
            pl.BlockSpec((None, None, _HEAD_DIM, s), lambda i, j, t: (i, j, 0, 0)),
            pl.BlockSpec((None, s, vw), lambda i, j, t: (i, 0, j)),
        ] + [side_spec(a) for a in side],
        out_specs=[pl.BlockSpec((None, tq, gw), lambda i, j, t: (i, t, j))] + [side_spec(a) for a in side],
        out_shape=[jax.ShapeDtypeStruct((b, l, nq), BF16)] + [jax.ShapeDtypeStruct(a.shape, BF16) for a in side],
        compiler_params=_params(("parallel", "parallel", "arbitrary"), 56),
        name="gqa_attention_bounded",
    )(q, kt, vx, *side)


def _attn_call(q, kt, vx, n_kv):
    b, l, nq = q.shape
    s = kt.shape[3]
    k = kt.transpose(0, 3, 1, 2).reshape(b, s, n_kv * _HEAD_DIM)
    v = vx.reshape(b, s, n_kv, 2 * _HEAD_DIM)[..., :_HEAD_DIM].reshape(b, s, n_kv * _HEAD_DIM)
    groups = nq // (n_kv * _HEAD_DIM)
    gw = groups * _HEAD_DIM
    tq = _tile(l, 256)
    tk = 768 if s % 768 == 0 else _tile(s, 512)
    n_chunks = s // tk
    m_rows = groups * tq
    return pl.pallas_call(
        functools.partial(_attn_kernel, tq=tq, tk=tk, groups=groups, n_chunks=n_chunks),
        grid=(b, n_kv, l // tq),
        in_specs=[
            pl.BlockSpec((None, tq, gw), lambda i, j, t: (i, t, j)),
            pl.BlockSpec((None, s, _HEAD_DIM), lambda i, j, t: (i, 0, j)),
            pl.BlockSpec((None, s, _HEAD_DIM), lambda i, j, t: (i, 0, j)),
        ],
        out_specs=pl.BlockSpec((None, tq, gw), lambda i, j, t: (i, t, j)),
        out_shape=jax.ShapeDtypeStruct((b, l, nq), BF16),
        scratch_shapes=[
            pltpu.VMEM((m_rows, _LANES), F32),
            pltpu.VMEM((m_rows, _LANES), F32),
            pltpu.VMEM((m_rows, _HEAD_DIM), F32),
        ],
        compiler_params=_params(("parallel", "parallel", "arbitrary"), 48),
        name="gqa_flash_attention",
    )(q, k, v)


_INFO_E0, _INFO_E1, _INFO_R0, _INFO_R1, _INFO_G0, _INFO_G1 = range(6)


def _route_kernel(o_ref, wo_ref, x_ref, g_ref, mod_ref, rw_ref, x1_ref, h_ref, info_ref, slots_ref, cnt_ref,
                  carry_ref, tri_ref, *, n_exp, tm):
    step = pl.program_id(0)

    ts = tri_ref.shape[0]

    @pl.when(step == 0)
    def _():
        carry_ref[...] = jnp.zeros_like(carry_ref)
        r = lax.broadcasted_iota(jnp.int32, (ts, ts), 0)
        cidx = lax.broadcasted_iota(jnp.int32, (ts, ts), 1)
        tri_ref[...] = jnp.where(r > cidx, 1.0, 0.0).astype(BF16)

    def sub_tile(rows, counted):
        x1 = x_ref[rows, :] + mod_ref[2:3, :] * _dot(o_ref[rows, :], wo_ref[...])
        x1_ref[rows, :] = x1
        h = _rms_mod(x1, g_ref[...], mod_ref[3:4, :], mod_ref[4:5, :])
        h_ref[rows] = pltpu.einshape("t(jc)->tjc", h, j=h_ref.shape[1])
        h_hi, h_lo = _split_bf16(h)
        both = _dot(h_hi, rw_ref[...])
        logits = both[:, :_LANES] + (both[:, _LANES:] + _dot(h_lo, rw_ref[:, :_LANES]))
        lane = lax.broadcasted_iota(jnp.int32, logits.shape, 1)
        neg = jnp.float32(-jnp.inf)
        lg = jnp.where(lane < n_exp, logits, neg)
        m0 = jnp.max(lg, axis=1, keepdims=True)
        i0 = jnp.min(jnp.where(lg == m0, lane, _LANES), axis=1, keepdims=True)
        oh0 = lane == i0
        lg1 = jnp.where(oh0, neg, lg)
        m1 = jnp.max(lg1, axis=1, keepdims=True)
        i1 = jnp.min(jnp.where(lg1 == m1, lane, _LANES), axis=1, keepdims=True)
        oh1 = lane == i1
        e = jnp.exp(m1 - m0)
        g0 = 1.0 / (1.0 + e)
        g1 = e * g0

        chosen = jnp.where(oh0 | oh1, 1.0, 0.0)
        before = _dot(tri_ref[...], chosen.astype(BF16)) + counted
        r0 = jnp.sum(jnp.where(oh0, before, 0.0), axis=1, keepdims=True)
        r1 = jnp.sum(jnp.where(oh1, before, 0.0), axis=1, keepdims=True)

        info = jnp.zeros(logits.shape, F32)
        for idx, val in ((_INFO_E0, i0.astype(F32)), (_INFO_E1, i1.astype(F32)), (_INFO_R0, r0), (_INFO_R1, r1),
                         (_INFO_G0, g0), (_INFO_G1, g1)):
            info = jnp.where(lane == idx, val, info)
        info_ref[rows, :] = info
        slots_ref[:, rows] = info.T[:_SUBLANES, :].astype(jnp.int32)
        return counted + jnp.sum(chosen, axis=0, keepdims=True)

    counted = carry_ref[...]
    for s in range(tm // ts):
        counted = sub_tile(slice(s * ts, (s + 1) * ts), counted)
    carry_ref[...] = counted
    cnt_ref[...] = counted


def _route_call(o, w_o, x, g, mod, router):
    b, l, d = x.shape
    k = o.shape[2]
    n_exp = router.shape[1]
    t = b * l
    tm = _tile(l, 1024)
    per_b = l // tm
    rw = jnp.pad(router, ((0, 0), (0, _LANES - n_exp)))
    rw = jnp.concatenate(_split_bf16(rw), axis=1)
    ts = tm // 2 if tm % 256 == 0 else tm
    return pl.pallas_call(
        functools.partial(_route_kernel, n_exp=n_exp, tm=tm),
        grid=(t // tm,),
        in_specs=[
            pl.BlockSpec((tm, k), lambda i: (i, 0)),
            pl.BlockSpec((k, d), lambda i: (0, 0), pipeline_mode=pl.Buffered(1)),
            pl.BlockSpec((tm, d), lambda i: (i, 0)),
            pl.BlockSpec((1, d), lambda i: (0, 0)),
            pl.BlockSpec((None, 6, d), lambda i: (i // per_b, 0, 0)),
            pl.BlockSpec((d, 2 * _LANES), lambda i: (0, 0)),
        ],
        out_specs=[
            pl.BlockSpec((tm, d), lambda i: (i, 0)),
            pl.BlockSpec((tm, d // _LANES, _LANES), lambda i: (i, 0, 0)),
            pl.BlockSpec((tm, _LANES), lambda i: (i, 0)),
            pl.BlockSpec((_SUBLANES, tm), lambda i: (0, i)),
            pl.BlockSpec((1, _LANES), lambda i: (0, 0)),
        ],
        out_shape=[
            jax.ShapeDtypeStruct((t, d), F32),
            jax.ShapeDtypeStruct((t, d // _LANES, _LANES), F32),
            jax.ShapeDtypeStruct((t, _LANES), F32),
            jax.ShapeDtypeStruct((_SUBLANES, t), jnp.int32),
            jax.ShapeDtypeStruct((1, _LANES), F32),
        ],
        scratch_shapes=[pltpu.VMEM((1, _LANES), F32), pltpu.VMEM((ts, ts), BF16)],
        compiler_params=_params(("arbitrary",), 48),
        name="moe_route",
    )(o.reshape(t, k), w_o, x.reshape(t, d), g.reshape(1, d), mod, rw)


def _pos_tiles(pos0, pos1, tt):
    n = pos0.shape[0] // tt
    return jnp.stack([pos0.reshape(n, tt), pos1.reshape(n, tt)], axis=1).reshape(n * 2 * tt)


def _row_copy(src, src_row, dst, dst_row, sem):
    return pltpu.make_async_copy(src.at[pl.ds(src_row, 1)], dst.at[pl.ds(dst_row, 1)], sem)


_ROW_DMA_UNROLL = 8


def _dispatch_kernel(pos_ref, h_ref, xs_in_ref, xs_ref, pos_smem, sem_s, sem_d, *, tt):
    del xs_in_ref
    cp = pltpu.make_async_copy(pos_ref, pos_smem, sem_s)
    cp.start()
    cp.wait()

    def body(r, carry):
        _row_copy(h_ref, r, xs_ref, pos_smem[r], sem_d).start(priority=0)
        _row_copy(h_ref, r, xs_ref, pos_smem[tt + r], sem_d).start(priority=1)
        return carry

    lax.fori_loop(0, tt, body, 0, unroll=_ROW_DMA_UNROLL)
    for _ in range(2 * tt):
        _row_copy(h_ref, 0, xs_ref, 0, sem_d).wait()


def _dispatch_call(h, pos_t, xs0, tt):
    t, nb, _ = h.shape
    p_rows = xs0.shape[0]
    return pl.pallas_call(
        functools.partial(_dispatch_kernel, tt=tt),
        grid=(t // tt,),
        in_specs=[
            pl.BlockSpec((2 * tt,), lambda i: (i,)),
            pl.BlockSpec((tt, nb, _LANES), lambda i: (i, 0, 0)),
            pl.BlockSpec(memory_space=pl.ANY),
        ],
        out_specs=pl.BlockSpec(memory_space=pl.ANY),
        out_shape=jax.ShapeDtypeStruct((p_rows, nb, _LANES), F32),
        scratch_shapes=[
            pltpu.SMEM((2 * tt,), jnp.int32),
            pltpu.SemaphoreType.DMA(()),
            pltpu.SemaphoreType.DMA(()),
        ],
        input_output_aliases={2: 0},
        compiler_params=_params(("arbitrary",), 32),
        name="moe_dispatch",
    )(pos_t, h, xs0)


def _moe_kernel(be_ref, bs_ref, bv_ref, x_ref, w1_ref, w3_ref, w2_ref, o_ref, acc_ref):
    del be_ref, bs_ref
    i = pl.program_id(0)
    f = pl.program_id(1)

    @pl.when((i == 0) & (f == 0))
    def _():
        acc_ref[...] = jnp.zeros_like(acc_ref)

    @pl.when(bv_ref[i] > 0)
    def _():
        xb = pltpu.einshape("tjc->t(jc)", x_ref[...]).astype(BF16)
        h1 = _dot(xb, w1_ref[...])
        h3 = _dot(xb, w3_ref[...])
        hid = (h1 * _sigmoid(h1)) * h3
        acc = jnp.where(f == 0, 0.0, acc_ref[...]) + _dot(hid.astype(BF16), w2_ref[...])
        acc_ref[...] = acc
        o_ref[...] = pltpu.einshape("t(jc)->tjc", acc, j=o_ref.shape[1])


def _moe_call(blk_e, blk_src, blk_valid, xs, w1, w3, w2, tmm):
    p_rows, nb, _ = xs.shape
    d = nb * _LANES
    ff = w1.shape[2]
    tf = 1792 if ff % 1792 == 0 else _tile(ff, 512)
    nf = ff // tf
    nblk = p_rows // tmm

    def fsel(i, f, bv):
        return jnp.where(bv[i] > 0, f, nf - 1)

    grid_spec = pltpu.PrefetchScalarGridSpec(
        num_scalar_prefetch=3,
        grid=(nblk, nf),
        in_specs=[
            pl.BlockSpec((tmm, nb, _LANES), lambda i, f, be, bs, bv: (bs[i], 0, 0)),
            pl.BlockSpec((None, d, tf), lambda i, f, be, bs, bv: (be[i], 0, fsel(i, f, bv))),
            pl.BlockSpec((None, d, tf), lambda i, f, be, bs, bv: (be[i], 0, fsel(i, f, bv))),
            pl.BlockSpec((None, tf, d), lambda i, f, be, bs, bv: (be[i], fsel(i, f, bv), 0)),
        ],
        out_specs=pl.BlockSpec((tmm, nb, _LANES), lambda i, f, be, bs, bv: (bs[i], 0, 0)),
        scratch_shapes=[pltpu.VMEM((tmm, d), F32)],
    )
    return pl.pallas_call(
        _moe_kernel,
        grid_spec=grid_spec,
        out_shape=jax.ShapeDtypeStruct((p_rows, nb, _LANES), F32),
        compiler_params=_params(("arbitrary", "arbitrary"), 58),
        name="moe_experts",
    )(blk_e, blk_src, blk_valid, xs, w1, w3, w2)


def _combine_kernel(pos_ref, pos_next_ref, x_ref, mod_ref, info_ref, fg_ref, y_ref, o_ref, pos_smem, ybuf,
                    sem_s, sem_d, *, tt, n_steps):
    i = pl.program_id(0)
    slot = lax.rem(i, 2)

    def start_gathers(p_ref, s):
        cp = pltpu.make_async_copy(p_ref, pos_smem, sem_s)
        cp.start()
        cp.wait()

        def body(r, carry):
            _row_copy(y_ref, pos_smem[r], ybuf.at[s, 0], r, sem_d.at[s]).start(priority=0)
            _row_copy(y_ref, pos_smem[tt + r], ybuf.at[s, 1], r, sem_d.at[s]).start(priority=1)
            return carry

        lax.fori_loop(0, tt, body, 0, unroll=_ROW_DMA_UNROLL)

    @pl.when(i == 0)
    def _():
        start_gathers(pos_ref, 0)

    @pl.when(i + 1 < n_steps)
    def _():
        start_gathers(pos_next_ref, 1 - slot)

    for _ in range(2 * tt):
        _row_copy(y_ref, 0, ybuf.at[slot, 0], 0, sem_d.at[slot]).wait()

    info = info_ref[...]
    g0 = info[:, _INFO_G0:_INFO_G0 + 1]
    g1 = info[:, _INFO_G1:_INFO_G1 + 1]
    y0 = pltpu.einshape("tjc->t(jc)", ybuf[slot, 0])
    y1 = pltpu.einshape("tjc->t(jc)", ybuf[slot, 1])
    ffn = g0 * y0 + g1 * y1
    xo = x_ref[...] + mod_ref[5:6, :] * ffn
    o_ref[...] = xo * lax.rsqrt(jnp.mean(xo * xo, axis=-1, keepdims=True) + _EPS) * fg_ref[...]


def _combine_call(pos_t, x, mod, info, final_g, y, tt):
    b, l, d = x.shape
    t = b * l
    per_b = l // tt
    n_steps = t // tt
    return pl.pallas_call(
        functools.partial(_combine_kernel, tt=tt, n_steps=n_steps),
        grid=(n_steps,),
        in_specs=[
            pl.BlockSpec((2 * tt,), lambda i: (i,)),
            pl.BlockSpec((2 * tt,), lambda i: (jnp.minimum(i + 1, n_steps - 1),)),
            pl.BlockSpec((tt, d), lambda i: (i, 0)),
            pl.BlockSpec((None, 6, d), lambda i: (i // per_b, 0, 0)),
            pl.BlockSpec((tt, _LANES), lambda i: (i, 0)),
            pl.BlockSpec((1, d), lambda i: (0, 0)),
            pl.BlockSpec(memory_space=pl.ANY),
        ],
        out_specs=pl.BlockSpec((tt, d), lambda i: (i, 0)),
        out_shape=jax.ShapeDtypeStruct((t, d), F32),
        scratch_shapes=[
            pltpu.SMEM((2 * tt,), jnp.int32),
            pltpu.VMEM((2, 2, tt, d // _LANES, _LANES), F32),
            pltpu.SemaphoreType.DMA(()),
            pltpu.SemaphoreType.DMA((2,)),
        ],
        compiler_params=_params(("arbitrary",), 40),
        name="moe_combine_norm",
    )(pos_t, pos_t, x.reshape(t, d), mod, info, final_g.reshape(1, d), y).reshape(b, l, d)


def _moe_block_rows(t):
    return 512 if t >= 4096 else 128


def _moe_sorted_rows(t, n_exp):
    tmm = _moe_block_rows(t)
    return ((t * 2) // tmm + n_exp) * tmm


def _moe_layer(o, w_o, x, g, mod, router, w1, w3, w2, final_g, xs0=None):
    b, l, d = x.shape
    t = b * l
    n_exp = router.shape[1]
    tmm = _moe_block_rows(t)
    tt = _tile(l, 512)
    x, h, info, slots, counts = _route_call(o, w_o, x, g, mod, router)
    x = x.reshape(b, l, d)

    counts = counts[0, :n_exp].astype(jnp.int32)
    padded = ((counts + tmm - 1) // tmm) * tmm
    pend = jnp.cumsum(padded)
    gstart = pend - padded
    pos0 = gstart[slots[_INFO_E0]] + slots[_INFO_R0]
    pos1 = gstart[slots[_INFO_E1]] + slots[_INFO_R1]
    pos_t = _pos_tiles(pos0, pos1, tt)

    p_rows = _moe_sorted_rows(t, n_exp)
    nblk = p_rows // tmm
    if xs0 is None:
        xs0 = jnp.zeros((p_rows, d // _LANES, _LANES), F32)
    bstart = jnp.arange(nblk, dtype=jnp.int32) * tmm
    used = bstart < pend[-1]
    last = pend[-1] // tmm - 1
    bidx = jnp.where(used, jnp.arange(nblk, dtype=jnp.int32), last)
    blk_e = jnp.minimum(jnp.sum(((bidx * tmm)[:, None] >= pend[None, :]).astype(jnp.int32), axis=1), n_exp - 1)
    blk_valid = jnp.where(used, jnp.clip(counts[blk_e] - (bidx * tmm - gstart[blk_e]), 0, tmm), 0).astype(jnp.int32)

    xs = _dispatch_call(h, pos_t, xs0, tt)
    y = _moe_call(blk_e, bidx.astype(jnp.int32), blk_valid, xs, w1, w3, w2, tmm)
    return _combine_call(pos_t, x, mod, info, final_g, y, tt)


def kernel(x, c, ctx, c_ctx, ada_w, ada_b, norm1_g, norm2_g, rg_w_in, rg_conv_w, rg_conv_b, rg_w_a, rg_b_a,
           rg_w_i, rg_b_i, rg_lam, rg_w_out, attn_w_qkv, attn_q_g, attn_k_g, attn_w_o, ffn_w1, ffn_w3, ffn_w2,
           moe_router, moe_w1, moe_w3, moe_w2, final_g):
    b, l, d = x.shape
    n_ctx = ctx.shape[1]
    c_dim = rg_w_out.shape[1]
    n_q = attn_w_o.shape[1] // _HEAD_DIM
    n_kv = (attn_w_qkv.shape[2] // _HEAD_DIM - n_q) // 2

    rows = 16
    c_rows = jnp.concatenate([c, c_ctx[None, :], jnp.zeros((rows - b - 1, d), F32)], axis=0)
    mod = _ada_call(c_rows, ada_w, ada_b).reshape(ada_w.shape[0], rows, 6, d)
    mod_l = [mod[i, :b] for i in range(2)]
    mod_c = [jnp.broadcast_to(mod[i, b], (b, 6, d)) for i in range(2)]

    w_in = rg_w_in[0].astype(BF16)
    w_out = rg_w_out[0].astype(BF16)
    w_a = rg_w_a[0].astype(BF16)
    w_i = rg_w_i[0].astype(BF16)
    xg_l = _nmm_call(x, norm1_g[0], mod_l[0], w_in, 0, 1, F32)
    xg_c = _nmm_call(ctx, norm1_g[0], mod_c[0], w_in, 0, 1, F32)
    zero_h = jnp.zeros((b, c_dim // _LANES, _LANES), F32)

    def lru(xg, h0, tail, direction, reverse):
        return _lru_call(xg, rg_conv_w[0], rg_conv_b[0], w_a[direction], rg_b_a[0, direction], w_i[direction],
                         rg_b_i[0, direction], rg_lam[0, direction], h0, tail, reverse=reverse)

    yc_rev, h0_rev = lru(xg_c, zero_h, None, 1, True)
    ctx, h0_fwd = lru(xg_c, zero_h, (yc_rev, w_out, ctx, mod_c[0]), 0, False)
    yl_rev, _ = lru(xg_l, h0_rev, None, 1, True)
    x, _ = lru(xg_l, h0_fwd, (yl_rev, w_out, x, mod_l[0]), 0, False)
    f1, f3, f2 = ffn_w1[0].astype(BF16), ffn_w3[0].astype(BF16), ffn_w2[0].astype(BF16)
    p_rows = _moe_sorted_rows(b * l, moe_router.shape[2])
    ffn_steps = b * (l // _tile(l, 512))
    if p_rows % ffn_steps == 0:
        x, xs0 = _ffn_call(x, norm2_g[0], mod_l[0], f1, f3, f2, zero_fill=(p_rows, d // _LANES, _LANES))
    else:
        x, xs0 = _ffn_call(x, norm2_g[0], mod_l[0], f1, f3, f2), None
    ctx = _ffn_call(ctx, norm2_g[0], mod_c[0], f1, f3, f2)

    w_qkv, q_g, k_g = _permute_qk_heads(attn_w_qkv[0], attn_q_g[0], attn_k_g[0], n_q, n_kv)
    w_qkv = w_qkv.astype(BF16)
    cos, sin = _rope_tables(l)
    q, k_l, v_l = _qkv_call(x, norm1_g[1], mod_l[1], w_qkv, q_g, k_g, cos, sin, n_q, n_kv)
    ones = jnp.ones((n_ctx, _HEAD_DIM), F32)
    _, k_c, v_c = _qkv_call(ctx, norm1_g[1], mod_c[1], w_qkv, q_g, k_g, ones, jnp.zeros_like(ones), n_q, n_kv)
    k_all = jnp.concatenate([k_c, k_l], axis=3)
    v_all = jnp.concatenate([v_c, v_l], axis=1)
    score_bound = _HEAD_DIM * jnp.max(jnp.abs(attn_q_g[0])) * jnp.max(jnp.abs(attn_k_g[0])) * _Q_SCALE
    ew = [w.reshape(-1, w.shape[-1]) for w in (moe_w1[0], moe_w3[0], moe_w2[0])]
    steps = b * n_kv * (l // _tile(l, 512))
    side = tuple(ew) if _side_cast_ok(ew, steps) else ()
    res = lax.cond(score_bound <= _SCORE_BOUND,
                   lambda: tuple(_attn_bounded_call(q, k_all, v_all, n_kv, side)),
                   lambda: (_attn_call(q, k_all, v_all, n_kv),) + tuple(w.astype(BF16) for w in side))
    o = res[0]
    ew = list(res[1:]) if side else [w.astype(BF16) for w in ew]
    ew = [w.reshape(m.shape) for w, m in zip(ew, (moe_w1[0], moe_w3[0], moe_w2[0]))]
    return _moe_layer(o, attn_w_o[0].astype(BF16), x, norm2_g[1], mod_l[1], moe_router[0], ew[0], ew[1], ew[2],
                      final_g, xs0)
```

```python
import functools

import jax
import jax.numpy as jnp
import numpy as np
from jax import lax
from jax.experimental import pallas as pl
from jax.experimental.pallas import tpu as pltpu

F32 = jnp.float32
BF16 = jnp.bfloat16

_EPS = 1e-6
_HEAD_DIM = 128
_GRID_W = 64
_ROPE_THETA = 10000.0
_ROPE_NFREQ = _HEAD_DIM // 4
_LRU_C = 8.0
_CONV_W = 4
_LANES = 128
_SUBLANES = 8
_LOG2E = 1.4426950408889634
_MIB = 1024 * 1024


def _params(semantics, vmem_mib):
    return pltpu.CompilerParams(dimension_semantics=semantics, vmem_limit_bytes=vmem_mib * _MIB)


def _tile(n, pref):
    if n <= pref:
        return n
    t = pref
    while n % t:
        t //= 2
    return t


def _dot(a, b):
    return jnp.dot(a, b, preferred_element_type=F32)


def _split_bf16(a):
    hi = a.astype(BF16)
    lo = (a - hi.astype(F32)).astype(BF16)
    return hi, lo


def _dot3(a, b):
    ah, al = _split_bf16(a)
    bh, bl = _split_bf16(b)
    return _dot(ah, bh) + (_dot(al, bh) + _dot(ah, bl))


def _rms_mod(x, g, shift, scale):
    y = x * lax.rsqrt(jnp.mean(x * x, axis=-1, keepdims=True) + _EPS)
    return (y * g) * (1.0 + scale) + shift


def _sigmoid(x):
    return 1.0 / (1.0 + jnp.exp(-x))


def _ada_kernel(c_ref, w_ref, b_ref, o_ref):
    c = c_ref[...]
    s = c * _sigmoid(c)
    o_ref[...] = _dot3(s, w_ref[...]) + b_ref[...]


def _ada_call(c_rows, ada_w, ada_b):
    depth, d, n = ada_w.shape
    rows = c_rows.shape[0]
    tn = _tile(n, 1536)
    return pl.pallas_call(
        _ada_kernel,
        grid=(depth, n // tn),
        in_specs=[
            pl.BlockSpec((rows, d), lambda i, j: (0, 0)),
            pl.BlockSpec((None, d, tn), lambda i, j: (i, 0, j)),
            pl.BlockSpec((None, 1, tn), lambda i, j: (i, 0, j)),
        ],
        out_specs=pl.BlockSpec((None, rows, tn), lambda i, j: (i, 0, j)),
        out_shape=jax.ShapeDtypeStruct((depth, rows, n), F32),
        compiler_params=_params(("parallel", "parallel"), 40),
        name="ada_mod",
    )(c_rows, ada_w, ada_b.reshape(depth, 1, n))


def _nmm_kernel(x_ref, g_ref, mod_ref, w_ref, o_ref, *, sh, sc):
    h = _rms_mod(x_ref[...], g_ref[...], mod_ref[sh:sh + 1, :], mod_ref[sc:sc + 1, :])
    o_ref[...] = _dot(h.astype(BF16), w_ref[...]).astype(o_ref.dtype)


def _nmm_call(x, g, mod, w, sh, sc, out_dtype):
    b, l, d = x.shape
    n = w.shape[1]
    tm = _tile(l, 512)
    return pl.pallas_call(
        functools.partial(_nmm_kernel, sh=sh, sc=sc),
        grid=(b, l // tm),
        in_specs=[
            pl.BlockSpec((None, tm, d), lambda i, t: (i, t, 0)),
            pl.BlockSpec((1, d), lambda i, t: (0, 0)),
            pl.BlockSpec((None, 6, d), lambda i, t: (i, 0, 0)),
            pl.BlockSpec((d, n), lambda i, t: (0, 0)),
        ],
        out_specs=pl.BlockSpec((None, tm, n), lambda i, t: (i, t, 0)),
        out_shape=jax.ShapeDtypeStruct((b, l, n), out_dtype),
        compiler_params=_params(("parallel", "parallel"), 48),
        name="norm_mod_matmul",
    )(x, g.reshape(1, d), mod, w)


def _log1p_pos(e):
    u = 1.0 + e
    den = jnp.where(u == 1.0, 1.0, u - 1.0)
    return jnp.where(u == 1.0, e, jnp.log(u) * (e * (1.0 / den)))


def _softplus(x):
    return jnp.maximum(x, 0.0) + _log1p_pos(jnp.exp(-jnp.abs(x)))


def _sigmoid_tanh(x):
    return 0.5 * jnp.tanh(0.5 * x) + 0.5


def _scan_rows(a, b, h, reverse):
    tl = a.shape[0]
    nb = a.shape[1] // _LANES
    a3 = pltpu.einshape("t(jc)->tjc", a, j=nb)
    b3 = pltpu.einshape("t(jc)->tjc", b, j=nb)
    states = [None] * tl
    for t in (range(tl - 1, -1, -1) if reverse else range(tl)):
        h = a3[t] * h + b3[t]
        states[t] = h
    return pltpu.einshape("tjc->t(jc)", jnp.stack(states, axis=0)), h


def _lru_kernel(*refs, reverse, tl, nt, fuse):
    if fuse:
        (xm_ref, xp_ref, xn_ref, cw_ref, cb_ref, wa_ref, ba_ref, wi_ref, bi_ref, lam_ref, h0_ref,
         yo_ref, gb_ref, wo_ref, res_ref, mod_ref, out_ref, hl_ref, carry_ref) = refs
    else:
        (xm_ref, xp_ref, xn_ref, cw_ref, cb_ref, wa_ref, ba_ref, wi_ref, bi_ref, lam_ref, h0_ref,
         out_ref, hl_ref, carry_ref) = refs
    t = pl.program_id(1)
    tt = (nt - 1 - t) if reverse else t

    @pl.when(t == 0)
    def _():
        carry_ref[...] = h0_ref[...]

    xm = xm_ref[...]
    prev = jnp.where(tt > 0, xp_ref[...], 0.0)
    nxt = jnp.where(tt < nt - 1, xn_ref[...], 0.0)
    row8 = lax.broadcasted_iota(jnp.int32, prev.shape, 0)

    def tap(offset):
        if offset < 0:
            s = -offset
            y = pltpu.roll(xm, s, axis=0)
            head = jnp.where(row8 < s, pltpu.roll(prev, s, axis=0), y[:_SUBLANES, :])
            return jnp.concatenate([head, y[_SUBLANES:, :]], axis=0)
        y = pltpu.roll(xm, tl - offset, axis=0)
        tail = jnp.where(row8 >= _SUBLANES - offset, pltpu.roll(nxt, _SUBLANES - offset, axis=0),
                         y[tl - _SUBLANES:, :])
        return jnp.concatenate([y[:tl - _SUBLANES, :], tail], axis=0)

    cw = cw_ref[...]
    left = _CONV_W // 2
    xc = cb_ref[...]
    for k in range(_CONV_W):
        xk = xm if k == left else tap(k - left)
        xc = xc + xk * cw[k:k + 1, :]

    xcb = xc.astype(BF16)
    nb = wa_ref.shape[0]
    blocks = [slice(j * _LANES, (j + 1) * _LANES) for j in range(nb)]
    pre_r = jnp.concatenate([_dot(xcb[:, s], wa_ref[j]) for j, s in enumerate(blocks)], axis=1)
    pre_i = jnp.concatenate([_dot(xcb[:, s], wi_ref[j]) for j, s in enumerate(blocks)], axis=1)
    r = _sigmoid_tanh(pre_r + ba_ref[...])
    i = _sigmoid_tanh(pre_i + bi_ref[...])
    decay = (_LRU_C * r) * _softplus(-lam_ref[...])
    a = jnp.exp(-decay)
    mult = jnp.sqrt(jnp.tanh(decay) * (1.0 + a * a))
    b = mult * (i * xc)

    hs, new_carry = _scan_rows(a, b, carry_ref[...], reverse)
    carry_ref[...] = new_carry
    hl_ref[...] = new_carry
    if fuse:
        gate = jax.nn.gelu(gb_ref[...], approximate=True)
        z = ((hs + yo_ref[...]) * gate).astype(BF16)
        out_ref[...] = res_ref[...] + mod_ref[2:3, :] * _dot(z, wo_ref[...])
    else:
        out_ref[...] = hs


def _lru_call(xg, conv_w, conv_b, w_a, b_a, w_i, b_i, lam, h0, tail=None, *, reverse):
    b, l, c2 = xg.shape
    c = c2 // 2
    nb = c // _LANES
    tl = _tile(l, 512 if tail is not None else 256)
    nt = l // tl
    hb = tl // _SUBLANES
    nh = l // _SUBLANES
    fuse = tail is not None

    def tmap(t):
        return (nt - 1 - t) if reverse else t

    vec = lambda: pl.BlockSpec((1, c), lambda i, t: (0, 0))
    gate_w = lambda: pl.BlockSpec((nb, _LANES, _LANES), lambda i, t: (0, 0, 0))
    in_specs = [
        pl.BlockSpec((None, tl, c), lambda i, t: (i, tmap(t), 0)),
        pl.BlockSpec((None, _SUBLANES, c), lambda i, t: (i, jnp.maximum(tmap(t) * hb - 1, 0), 0)),
        pl.BlockSpec((None, _SUBLANES, c), lambda i, t: (i, jnp.minimum((tmap(t) + 1) * hb, nh - 1), 0)),
        pl.BlockSpec((_CONV_W, c), lambda i, t: (0, 0)),
        vec(),
        gate_w(),
        vec(),
        gate_w(),
        vec(),
        vec(),
        pl.BlockSpec((None, nb, _LANES), lambda i, t: (i, 0, 0)),
    ]
    args = [xg, xg, xg, conv_w, conv_b.reshape(1, c), w_a, b_a.reshape(1, c), w_i, b_i.reshape(1, c),
            lam.reshape(1, c), h0]
    d_out = c
    if fuse:
        y_other, w_out, res, mod = tail
        d_out = w_out.shape[1]
        in_specs += [
            pl.BlockSpec((None, tl, c), lambda i, t: (i, tmap(t), 0)),
            pl.BlockSpec((None, tl, c), lambda i, t: (i, tmap(t), 1)),
            pl.BlockSpec((c, d_out), lambda i, t: (0, 0), pipeline_mode=pl.Buffered(1)),
            pl.BlockSpec((None, tl, d_out), lambda i, t: (i, tmap(t), 0)),
            pl.BlockSpec((None, 6, d_out), lambda i, t: (i, 0, 0)),
        ]
        args += [y_other, xg, w_out, res, mod]
    return pl.pallas_call(
        functools.partial(_lru_kernel, reverse=reverse, tl=tl, nt=nt, fuse=fuse),
        grid=(b, nt),
        in_specs=in_specs,
        out_specs=[
            pl.BlockSpec((None, tl, d_out), lambda i, t: (i, tmap(t), 0)),
            pl.BlockSpec((None, nb, _LANES), lambda i, t: (i, 0, 0)),
        ],
        out_shape=[jax.ShapeDtypeStruct((b, l, d_out), F32), jax.ShapeDtypeStruct((b, nb, _LANES), F32)],
        scratch_shapes=[pltpu.VMEM((nb, _LANES), F32)],
        compiler_params=_params(("parallel", "arbitrary"), 48),
        name="lru_rev" if reverse else "lru_fwd",
    )(*args)


def _ffn_kernel(x_ref, g_ref, mod_ref, w1_ref, w3_ref, w2_ref, o_ref, *fill_refs):
    x = x_ref[...]
    h = _rms_mod(x, g_ref[...], mod_ref[3:4, :], mod_ref[4:5, :]).astype(BF16)
    h1 = _dot(h, w1_ref[...])
    h3 = _dot(h, w3_ref[...])
    hid = (h1 * _sigmoid(h1)) * h3
    o_ref[...] = x + mod_ref[5:6, :] * _dot(hid.astype(BF16), w2_ref[...])
    for fill_ref in fill_refs:
        fill_ref[...] = jnp.zeros_like(fill_ref)


def _ffn_call(x, g, mod, w1, w3, w2, zero_fill=None):
    b, l, d = x.shape
    ff = w1.shape[1]
    tm = _tile(l, 512)
    steps = b * (l // tm)
    resident = pl.Buffered(1)
    out_specs = [pl.BlockSpec((None, tm, d), lambda i, t: (i, t, 0))]
    out_shape = [jax.ShapeDtypeStruct((b, l, d), F32)]
    if zero_fill is not None:
        rows = zero_fill[0] // steps
        out_specs.append(pl.BlockSpec((rows,) + tuple(zero_fill[1:]), lambda i, t: (i * (l // tm) + t, 0, 0)))
        out_shape.append(jax.ShapeDtypeStruct(tuple(zero_fill), F32))
    res = pl.pallas_call(
        _ffn_kernel,
        grid=(b, l // tm),
        in_specs=[
            pl.BlockSpec((None, tm, d), lambda i, t: (i, t, 0)),
            pl.BlockSpec((1, d), lambda i, t: (0, 0)),
            pl.BlockSpec((None, 6, d), lambda i, t: (i, 0, 0)),
            pl.BlockSpec((d, ff), lambda i, t: (0, 0), pipeline_mode=resident),
            pl.BlockSpec((d, ff), lambda i, t: (0, 0), pipeline_mode=resident),
            pl.BlockSpec((ff, d), lambda i, t: (0, 0), pipeline_mode=resident),
        ],
        out_specs=out_specs,
        out_shape=out_shape,
        compiler_params=_params(("parallel", "parallel"), 56),
        name="dense_swiglu",
    )(x, g.reshape(1, d), mod, w1, w3, w2)
    return res[0] if zero_fill is None else tuple(res)


_ROPE_PERM = np.concatenate([np.arange(0, 32), np.arange(64, 96), np.arange(32, 64), np.arange(96, 128)])


def _rope_tables(l):
    pos = np.arange(l)
    row = (pos // _GRID_W).astype(np.float64)
    col = (pos % _GRID_W).astype(np.float64)
    freqs = _ROPE_THETA ** (-np.arange(_ROPE_NFREQ, dtype=np.float64) / _ROPE_NFREQ)
    ar = row[:, None] * freqs
    ac = col[:, None] * freqs
    cos = np.concatenate([np.cos(ar), np.cos(ac), np.cos(ar), np.cos(ac)], axis=-1)
    sin = np.concatenate([-np.sin(ar), -np.sin(ac), np.sin(ar), np.sin(ac)], axis=-1)
    return jnp.asarray(cos, F32), jnp.asarray(sin, F32)


def _permute_qk_heads(w_qkv, q_g, k_g, n_q, n_kv):
    cols = np.arange(w_qkv.shape[1])
    for hh in range(n_q + n_kv):
        cols[hh * _HEAD_DIM:(hh + 1) * _HEAD_DIM] = hh * _HEAD_DIM + _ROPE_PERM
    return w_qkv[:, cols], q_g[_ROPE_PERM], k_g[_ROPE_PERM]


def _qkv_kernel(x_ref, g_ref, mod_ref, w_ref, qg_ref, kg_ref, cos_ref, sin_ref, q_ref, k_ref, v_ref,
                *, n_q, n_kv, q_scale):
    tm = x_ref.shape[0]
    ts = tm // 2 if tm % (4 * _LANES) == 0 else tm
    for s0 in range(0, tm, ts):
        rows = slice(s0, s0 + ts)
        h = _rms_mod(x_ref[rows, :], g_ref[...], mod_ref[0:1, :], mod_ref[1:2, :])
        qkv = _dot(h.astype(BF16), w_ref[...])
        cos = cos_ref[rows, :]
        sin = sin_ref[rows, :]
        lane = lax.broadcasted_iota(jnp.int32, cos.shape, 1)

        def norm_rope(v, gain):
            y = v * lax.rsqrt(jnp.mean(v * v, axis=-1, keepdims=True) + _EPS) * gain
            return y * cos + pltpu.roll(y, _HEAD_DIM // 2, axis=1) * sin

        for hh in range(n_q):
            sl = slice(hh * _HEAD_DIM, (hh + 1) * _HEAD_DIM)
            q_ref[rows, sl] = (norm_rope(qkv[:, sl], qg_ref[...]) * q_scale).astype(q_ref.dtype)
        for hh in range(n_kv):
            src = slice((n_q + hh) * _HEAD_DIM, (n_q + hh + 1) * _HEAD_DIM)
            k_ref[hh, :, rows] = norm_rope(qkv[:, src], kg_ref[...]).T.astype(k_ref.dtype)
        ones_col = jnp.where(lane == 0, 1.0, 0.0).astype(v_ref.dtype)
        for hh in range(n_kv):
            src = slice((n_q + n_kv + hh) * _HEAD_DIM, (n_q + n_kv + hh + 1) * _HEAD_DIM)
            v_ref[rows, 2 * hh * _HEAD_DIM:(2 * hh + 1) * _HEAD_DIM] = qkv[:, src].astype(v_ref.dtype)
            v_ref[rows, (2 * hh + 1) * _HEAD_DIM:(2 * hh + 2) * _HEAD_DIM] = ones_col


_Q_SCALE = (_HEAD_DIM ** -0.5) * _LOG2E


def _qkv_call(x, g, mod, w, q_g, k_g, cos, sin, n_q, n_kv):
    b, l, d = x.shape
    n = w.shape[1]
    tm = _tile(l, 512)
    nq = n_q * _HEAD_DIM
    nkv = n_kv * _HEAD_DIM
    q_scale = _Q_SCALE
    return pl.pallas_call(
        functools.partial(_qkv_kernel, n_q=n_q, n_kv=n_kv, q_scale=q_scale),
        grid=(b, l // tm),
        in_specs=[
            pl.BlockSpec((None, tm, d), lambda i, t: (i, t, 0)),
            pl.BlockSpec((1, d), lambda i, t: (0, 0)),
            pl.BlockSpec((None, 6, d), lambda i, t: (i, 0, 0)),
            pl.BlockSpec((d, n), lambda i, t: (0, 0)),
            pl.BlockSpec((1, _HEAD_DIM), lambda i, t: (0, 0)),
            pl.BlockSpec((1, _HEAD_DIM), lambda i, t: (0, 0)),
            pl.BlockSpec((tm, _HEAD_DIM), lambda i, t: (t, 0)),
            pl.BlockSpec((tm, _HEAD_DIM), lambda i, t: (t, 0)),
        ],
        out_specs=[
            pl.BlockSpec((None, tm, nq), lambda i, t: (i, t, 0)),
            pl.BlockSpec((None, n_kv, _HEAD_DIM, tm), lambda i, t: (i, 0, 0, t)),
            pl.BlockSpec((None, tm, 2 * nkv), lambda i, t: (i, t, 0)),
        ],
        out_shape=[
            jax.ShapeDtypeStruct((b, l, nq), BF16),
            jax.ShapeDtypeStruct((b, n_kv, _HEAD_DIM, l), BF16),
            jax.ShapeDtypeStruct((b, l, 2 * nkv), BF16),
        ],
        compiler_params=_params(("parallel", "parallel"), 48),
        name="qkv_norm_rope",
    )(x, g.reshape(1, d), mod, w, q_g.reshape(1, _HEAD_DIM), k_g.reshape(1, _HEAD_DIM), cos, sin)


def _lane_tile(x, reps):
    return x if reps == 1 else jnp.concatenate([x] * reps, axis=1)


def _attn_kernel(q_ref, k_ref, v_ref, o_ref, m_ref, l_ref, acc_ref, *, tq, tk, groups, n_chunks):
    q = jnp.concatenate([q_ref[:, g * _HEAD_DIM:(g + 1) * _HEAD_DIM] for g in range(groups)], axis=0)
    m_ref[...] = jnp.full_like(m_ref, -jnp.inf)
    l_ref[...] = jnp.zeros_like(l_ref)
    acc_ref[...] = jnp.zeros_like(acc_ref)

    def body(c, carry):
        start = pl.multiple_of(c * tk, tk)
        k = k_ref[pl.ds(start, tk), :]
        v = v_ref[pl.ds(start, tk), :]
        s = lax.dot_general(q, k, (((1,), (1,)), ((), ())), preferred_element_type=F32)
        m_prev = m_ref[...]
        m_next = jnp.maximum(m_prev, jnp.max(s, axis=1, keepdims=True))
        alpha = jnp.exp2(m_prev - m_next)
        p = jnp.exp2(s - _lane_tile(m_next, tk // _LANES))
        l_ref[...] = alpha * l_ref[...] + jnp.sum(p, axis=1, keepdims=True)
        acc_ref[...] = alpha * acc_ref[...] + _dot(p.astype(BF16), v)
        m_ref[...] = m_next
        return carry

    lax.fori_loop(0, n_chunks, body, 0)
    o = acc_ref[...] * (1.0 / l_ref[...])
    for g in range(groups):
        o_ref[:, g * _HEAD_DIM:(g + 1) * _HEAD_DIM] = o[g * tq:(g + 1) * tq, :].astype(o_ref.dtype)


def _attn_bounded_kernel(q_ref, kt_ref, v_ref, *rest, tq, tk, groups, n_chunks):
    n_side = (len(rest) - 1) // 2
    side_in, o_ref, side_out = rest[:n_side], rest[n_side], rest[n_side + 1:]
    for src, dst in zip(side_in, side_out):
        dst[...] = src[...].astype(dst.dtype)
    q = jnp.concatenate([q_ref[:, g * _HEAD_DIM:(g + 1) * _HEAD_DIM] for g in range(groups)], axis=0)
    acc = None
    for c in range(n_chunks):
        p = jnp.exp2(_dot(q, kt_ref[:, c * tk:(c + 1) * tk])).astype(BF16)
        pv = _dot(p, v_ref[c * tk:(c + 1) * tk, :])
        acc = pv if acc is None else acc + pv
    o = acc[:, :_HEAD_DIM] * (1.0 / acc[:, _HEAD_DIM:_HEAD_DIM + 1])
    for g in range(groups):
        o_ref[:, g * _HEAD_DIM:(g + 1) * _HEAD_DIM] = o[g * tq:(g + 1) * tq, :].astype(o_ref.dtype)


_SCORE_BOUND = 60.0
_KV_CHUNKS = (256, 128)


def _side_cast_ok(side, steps):
    return all(a.shape[0] % steps == 0 and (a.shape[0] // steps) % (2 * _SUBLANES) == 0 for a in side)


def _attn_bounded_call(q, kt, vx, n_kv, side=()):
    b, l, nq = q.shape
    s = kt.shape[3]
    groups = nq // (n_kv * _HEAD_DIM)
    gw = groups * _HEAD_DIM
    tq = _tile(l, 512)
    nt = l // tq
    tk = next(c for c in _KV_CHUNKS if s % c == 0)
    n_chunks = s // tk
    vw = 2 * _HEAD_DIM
    steps = b * n_kv * nt
    side_spec = lambda a: pl.BlockSpec((a.shape[0] // steps, a.shape[1]),
                                       lambda i, j, t: ((i * n_kv + j) * nt + t, 0))
    return pl.pallas_call(
        functools.partial(_attn_bounded_kernel, tq=tq, tk=tk, groups=groups, n_chunks=n_chunks),
        grid=(b, n_kv, nt),
        in_specs=[
            pl.BlockSpec((None, tq, gw), lambda i, j, t: (i, t, j)),
            pl.BlockSpec((None, None, _HEAD_DIM, s), lambda i, j, t: (i, j, 0, 0)),
            pl.BlockSpec((None, s, vw), lambda i, j, t: (i, 0, j)),
        ] + [side_spec(a) for a in side],
        out_specs=[pl.BlockSpec((None, tq, gw), lambda i, j, t: (i, t, j))] + [side_spec(a) for a in side],
        out_shape=[jax.ShapeDtypeStruct((b, l, nq), BF16)] + [jax.ShapeDtypeStruct(a.shape, BF16) for a in side],
        compiler_params=_params(("parallel", "parallel", "arbitrary"), 56),
        name="gqa_attention_bounded",
    )(q, kt, vx, *side)


def _attn_call(q, kt, vx, n_kv):
    b, l, nq = q.shape
    s = kt.shape[3]
    k = kt.transpose(0, 3, 1, 2).reshape(b, s, n_kv * _HEAD_DIM)
    v = vx.reshape(b, s, n_kv, 2 * _HEAD_DIM)[..., :_HEAD_DIM].reshape(b, s, n_kv * _HEAD_DIM)
    groups = nq // (n_kv * _HEAD_DIM)
    gw = groups * _HEAD_DIM
    tq = _tile(l, 256)
    tk = 768 if s % 768 == 0 else _tile(s, 512)
    n_chunks = s // tk
    m_rows = groups * tq
    return pl.pallas_call(
        functools.partial(_attn_kernel, tq=tq, tk=tk, groups=groups, n_chunks=n_chunks),
        grid=(b, n_kv, l // tq),
        in_specs=[
            pl.BlockSpec((None, tq, gw), lambda i, j, t: (i, t, j)),
            pl.BlockSpec((None, s, _HEAD_DIM), lambda i, j, t: (i, 0, j)),
            pl.BlockSpec((None, s, _HEAD_DIM), lambda i, j, t: (i, 0, j)),
        ],
        out_specs=pl.BlockSpec((None, tq, gw), lambda i, j, t: (i, t, j)),
        out_shape=jax.ShapeDtypeStruct((b, l, nq), BF16),
        scratch_shapes=[
            pltpu.VMEM((m_rows, _LANES), F32),
            pltpu.VMEM((m_rows, _LANES), F32),
            pltpu.VMEM((m_rows, _HEAD_DIM), F32),
        ],
        compiler_params=_params(("parallel", "parallel", "arbitrary"), 48),
        name="gqa_flash_attention",
    )(q, k, v)


_INFO_E0, _INFO_E1, _INFO_R0, _INFO_R1, _INFO_G0, _INFO_G1 = range(6)


def _route_kernel(o_ref, wo_ref, x_ref, g_ref, mod_ref, rw_ref, x1_ref, h_ref, info_ref, slots_ref, cnt_ref,
                  carry_ref, tri_ref, *, n_exp, tm):
    step = pl.program_id(0)

    ts = tri_ref.shape[0]

    @pl.when(step == 0)
    def _():
        carry_ref[...] = jnp.zeros_like(carry_ref)
        r = lax.broadcasted_iota(jnp.int32, (ts, ts), 0)
        cidx = lax.broadcasted_iota(jnp.int32, (ts, ts), 1)
        tri_ref[...] = jnp.where(r > cidx, 1.0, 0.0).astype(BF16)

    def sub_tile(rows, counted):
        x1 = x_ref[rows, :] + mod_ref[2:3, :] * _dot(o_ref[rows, :], wo_ref[...])
        x1_ref[rows, :] = x1
        h = _rms_mod(x1, g_ref[...], mod_ref[3:4, :], mod_ref[4:5, :])
        h_ref[rows] = pltpu.einshape("t(jc)->tjc", h, j=h_ref.shape[1])
        h_hi, h_lo = _split_bf16(h)
        both = _dot(h_hi, rw_ref[...])
        logits = both[:, :_LANES] + (both[:, _LANES:] + _dot(h_lo, rw_ref[:, :_LANES]))
        lane = lax.broadcasted_iota(jnp.int32, logits.shape, 1)
        neg = jnp.float32(-jnp.inf)
        lg = jnp.where(lane < n_exp, logits, neg)
        m0 = jnp.max(lg, axis=1, keepdims=True)
        i0 = jnp.min(jnp.where(lg == m0, lane, _LANES), axis=1, keepdims=True)
        oh0 = lane == i0
        lg1 = jnp.where(oh0, neg, lg)
        m1 = jnp.max(lg1, axis=1, keepdims=True)
        i1 = jnp.min(jnp.where(lg1 == m1, lane, _LANES), axis=1, keepdims=True)
        oh1 = lane == i1
        e = jnp.exp(m1 - m0)
        g0 = 1.0 / (1.0 + e)
        g1 = e * g0

        chosen = jnp.where(oh0 | oh1, 1.0, 0.0)
        before = _dot(tri_ref[...], chosen.astype(BF16)) + counted
        r0 = jnp.sum(jnp.where(oh0, before, 0.0), axis=1, keepdims=True)
        r1 = jnp.sum(jnp.where(oh1, before, 0.0), axis=1, keepdims=True)

        info = jnp.zeros(logits.shape, F32)
        for idx, val in ((_INFO_E0, i0.astype(F32)), (_INFO_E1, i1.astype(F32)), (_INFO_R0, r0), (_INFO_R1, r1),
                         (_INFO_G0, g0), (_INFO_G1, g1)):
            info = jnp.where(lane == idx, val, info)
        info_ref[rows, :] = info
        slots_ref[:, rows] = info.T[:_SUBLANES, :].astype(jnp.int32)
        return counted + jnp.sum(chosen, axis=0, keepdims=True)

    counted = carry_ref[...]
    for s in range(tm // ts):
        counted = sub_tile(slice(s * ts, (s + 1) * ts), counted)
    carry_ref[...] = counted
    cnt_ref[...] = counted


def _route_call(o, w_o, x, g, mod, router):
    b, l, d = x.shape
    k = o.shape[2]
    n_exp = router.shape[1]
    t = b * l
    tm = _tile(l, 1024)
    per_b = l // tm
    rw = jnp.pad(router, ((0, 0), (0, _LANES - n_exp)))
    rw = jnp.concatenate(_split_bf16(rw), axis=1)
    ts = tm // 2 if tm % 256 == 0 else tm
    return pl.pallas_call(
        functools.partial(_route_kernel, n_exp=n_exp, tm=tm),
        grid=(t // tm,),
        in_specs=[
            pl.BlockSpec((tm, k), lambda i: (i, 0)),
            pl.BlockSpec((k, d), lambda i: (0, 0), pipeline_mode=pl.Buffered(1)),
            pl.BlockSpec((tm, d), lambda i: (i, 0)),
            pl.BlockSpec((1, d), lambda i: (0, 0)),
            pl.BlockSpec((None, 6, d), lambda i: (i // per_b, 0, 0)),
            pl.BlockSpec((d, 2 * _LANES), lambda i: (0, 0)),
        ],
        out_specs=[
            pl.BlockSpec((tm, d), lambda i: (i, 0)),
            pl.BlockSpec((tm, d // _LANES, _LANES), lambda i: (i, 0, 0)),
            pl.BlockSpec((tm, _LANES), lambda i: (i, 0)),
            pl.BlockSpec((_SUBLANES, tm), lambda i: (0, i)),
            pl.BlockSpec((1, _LANES), lambda i: (0, 0)),
        ],
        out_shape=[
            jax.ShapeDtypeStruct((t, d), F32),
            jax.ShapeDtypeStruct((t, d // _LANES, _LANES), F32),
            jax.ShapeDtypeStruct((t, _LANES), F32),
            jax.ShapeDtypeStruct((_SUBLANES, t), jnp.int32),
            jax.ShapeDtypeStruct((1, _LANES), F32),
        ],
        scratch_shapes=[pltpu.VMEM((1, _LANES), F32), pltpu.VMEM((ts, ts), BF16)],
        compiler_params=_params(("arbitrary",), 48),
        name="moe_route",
    )(o.reshape(t, k), w_o, x.reshape(t, d), g.reshape(1, d), mod, rw)


def _pos_tiles(pos0, pos1, tt):
    n = pos0.shape[0] // tt
    return jnp.stack([pos0.reshape(n, tt), pos1.reshape(n, tt)], axis=1).reshape(n * 2 * tt)


def _row_copy(src, src_row, dst, dst_row, sem):
    return pltpu.make_async_copy(src.at[pl.ds(src_row, 1)], dst.at[pl.ds(dst_row, 1)], sem)


_ROW_DMA_UNROLL = 8


def _dispatch_kernel(pos_ref, h_ref, xs_in_ref, xs_ref, pos_smem, sem_s, sem_d, *, tt):
    del xs_in_ref
    cp = pltpu.make_async_copy(pos_ref, pos_smem, sem_s)
    cp.start()
    cp.wait()

    def body(r, carry):
        _row_copy(h_ref, r, xs_ref, pos_smem[r], sem_d).start(priority=0)
        _row_copy(h_ref, r, xs_ref, pos_smem[tt + r], sem_d).start(priority=1)
        return carry

    lax.fori_loop(0, tt, body, 0, unroll=_ROW_DMA_UNROLL)
    for _ in range(2 * tt):
        _row_copy(h_ref, 0, xs_ref, 0, sem_d).wait()


def _dispatch_call(h, pos_t, xs0, tt):
    t, nb, _ = h.shape
    p_rows = xs0.shape[0]
    return pl.pallas_call(
        functools.partial(_dispatch_kernel, tt=tt),
        grid=(t // tt,),
        in_specs=[
            pl.BlockSpec((2 * tt,), lambda i: (i,)),
            pl.BlockSpec((tt, nb, _LANES), lambda i: (i, 0, 0)),
            pl.BlockSpec(memory_space=pl.ANY),
        ],
        out_specs=pl.BlockSpec(memory_space=pl.ANY),
        out_shape=jax.ShapeDtypeStruct((p_rows, nb, _LANES), F32),
        scratch_shapes=[
            pltpu.SMEM((2 * tt,), jnp.int32),
            pltpu.SemaphoreType.DMA(()),
            pltpu.SemaphoreType.DMA(()),
        ],
        input_output_aliases={2: 0},
        compiler_params=_params(("arbitrary",), 32),
        name="moe_dispatch",
    )(pos_t, h, xs0)


def _moe_kernel(be_ref, bs_ref, bv_ref, x_ref, w1_ref, w3_ref, w2_ref, o_ref, acc_ref):
    del be_ref, bs_ref
    i = pl.program_id(0)
    f = pl.program_id(1)

    @pl.when((i == 0) & (f == 0))
    def _():
        acc_ref[...] = jnp.zeros_like(acc_ref)

    @pl.when(bv_ref[i] > 0)
    def _():
        xb = pltpu.einshape("tjc->t(jc)", x_ref[...]).astype(BF16)
        h1 = _dot(xb, w1_ref[...])
        h3 = _dot(xb, w3_ref[...])
        hid = (h1 * _sigmoid(h1)) * h3
        acc = jnp.where(f == 0, 0.0, acc_ref[...]) + _dot(hid.astype(BF16), w2_ref[...])
        acc_ref[...] = acc
        o_ref[...] = pltpu.einshape("t(jc)->tjc", acc, j=o_ref.shape[1])


def _moe_call(blk_e, blk_src, blk_valid, xs, w1, w3, w2, tmm):
    p_rows, nb, _ = xs.shape
    d = nb * _LANES
    ff = w1.shape[2]
    tf = 1792 if ff % 1792 == 0 else _tile(ff, 512)
    nf = ff // tf
    nblk = p_rows // tmm

    def fsel(i, f, bv):
        return jnp.where(bv[i] > 0, f, nf - 1)

    grid_spec = pltpu.PrefetchScalarGridSpec(
        num_scalar_prefetch=3,
        grid=(nblk, nf),
        in_specs=[
            pl.BlockSpec((tmm, nb, _LANES), lambda i, f, be, bs, bv: (bs[i], 0, 0)),
            pl.BlockSpec((None, d, tf), lambda i, f, be, bs, bv: (be[i], 0, fsel(i, f, bv))),
            pl.BlockSpec((None, d, tf), lambda i, f, be, bs, bv: (be[i], 0, fsel(i, f, bv))),
            pl.BlockSpec((None, tf, d), lambda i, f, be, bs, bv: (be[i], fsel(i, f, bv), 0)),
        ],
        out_specs=pl.BlockSpec((tmm, nb, _LANES), lambda i, f, be, bs, bv: (bs[i], 0, 0)),
        scratch_shapes=[pltpu.VMEM((tmm, d), F32)],
    )
    return pl.pallas_call(
        _moe_kernel,
        grid_spec=grid_spec,
        out_shape=jax.ShapeDtypeStruct((p_rows, nb, _LANES), F32),
        compiler_params=_params(("arbitrary", "arbitrary"), 58),
        name="moe_experts",
    )(blk_e, blk_src, blk_valid, xs, w1, w3, w2)


def _combine_kernel(pos_ref, pos_next_ref, x_ref, mod_ref, info_ref, fg_ref, y_ref, o_ref, pos_smem, ybuf,
                    sem_s, sem_d, *, tt, n_steps):
    i = pl.program_id(0)
    slot = lax.rem(i, 2)

    def start_gathers(p_ref, s):
        cp = pltpu.make_async_copy(p_ref, pos_smem, sem_s)
        cp.start()
        cp.wait()

        def body(r, carry):
            _row_copy(y_ref, pos_smem[r], ybuf.at[s, 0], r, sem_d.at[s]).start(priority=0)
            _row_copy(y_ref, pos_smem[tt + r], ybuf.at[s, 1], r, sem_d.at[s]).start(priority=1)
            return carry

        lax.fori_loop(0, tt, body, 0, unroll=_ROW_DMA_UNROLL)

    @pl.when(i == 0)
    def _():
        start_gathers(pos_ref, 0)

    @pl.when(i + 1 < n_steps)
    def _():
        start_gathers(pos_next_ref, 1 - slot)

    for _ in range(2 * tt):
        _row_copy(y_ref, 0, ybuf.at[slot, 0], 0, sem_d.at[slot]).wait()

    info = info_ref[...]
    g0 = info[:, _INFO_G0:_INFO_G0 + 1]
    g1 = info[:, _INFO_G1:_INFO_G1 + 1]
    y0 = pltpu.einshape("tjc->t(jc)", ybuf[slot, 0])
    y1 = pltpu.einshape("tjc->t(jc)", ybuf[slot, 1])
    ffn = g0 * y0 + g1 * y1
    xo = x_ref[...] + mod_ref[5:6, :] * ffn
    o_ref[...] = xo * lax.rsqrt(jnp.mean(xo * xo, axis=-1, keepdims=True) + _EPS) * fg_ref[...]


def _combine_call(pos_t, x, mod, info, final_g, y, tt):
    b, l, d = x.shape
    t = b * l
    per_b = l // tt
    n_steps = t // tt
    return pl.pallas_call(
        functools.partial(_combine_kernel, tt=tt, n_steps=n_steps),
        grid=(n_steps,),
        in_specs=[
            pl.BlockSpec((2 * tt,), lambda i: (i,)),
            pl.BlockSpec((2 * tt,), lambda i: (jnp.minimum(i + 1, n_steps - 1),)),
            pl.BlockSpec((tt, d), lambda i: (i, 0)),
            pl.BlockSpec((None, 6, d), lambda i: (i // per_b, 0, 0)),
            pl.BlockSpec((tt, _LANES), lambda i: (i, 0)),
            pl.BlockSpec((1, d), lambda i: (0, 0)),
            pl.BlockSpec(memory_space=pl.ANY),
        ],
        out_specs=pl.BlockSpec((tt, d), lambda i: (i, 0)),
        out_shape=jax.ShapeDtypeStruct((t, d), F32),
        scratch_shapes=[
            pltpu.SMEM((2 * tt,), jnp.int32),
            pltpu.VMEM((2, 2, tt, d // _LANES, _LANES), F32),
            pltpu.SemaphoreType.DMA(()),
            pltpu.SemaphoreType.DMA((2,)),
        ],
        compiler_params=_params(("arbitrary",), 40),
        name="moe_combine_norm",
    )(pos_t, pos_t, x.reshape(t, d), mod, info, final_g.reshape(1, d), y).reshape(b, l, d)


def _moe_block_rows(t):
    return 512 if t >= 4096 else 128


def _moe_sorted_rows(t, n_exp):
    tmm = _moe_block_rows(t)
    return ((t * 2) // tmm + n_exp) * tmm


def _moe_layer(o, w_o, x, g, mod, router, w1, w3, w2, final_g, xs0=None):
    b, l, d = x.shape
    t = b * l
    n_exp = router.shape[1]
    tmm = _moe_block_rows(t)
    tt = _tile(l, 512)
    x, h, info, slots, counts = _route_call(o, w_o, x, g, mod, router)
    x = x.reshape(b, l, d)

    counts = counts[0, :n_exp].astype(jnp.int32)
    padded = ((counts + tmm - 1) // tmm) * tmm
    pend = jnp.cumsum(padded)
    gstart = pend - padded
    pos0 = gstart[slots[_INFO_E0]] + slots[_INFO_R0]
    pos1 = gstart[slots[_INFO_E1]] + slots[_INFO_R1]
    pos_t = _pos_tiles(pos0, pos1, tt)

    p_rows = _moe_sorted_rows(t, n_exp)
    nblk = p_rows // tmm
    if xs0 is None:
        xs0 = jnp.zeros((p_rows, d // _LANES, _LANES), F32)
    bstart = jnp.arange(nblk, dtype=jnp.int32) * tmm
    used = bstart < pend[-1]
    last = pend[-1] // tmm - 1
    bidx = jnp.where(used, jnp.arange(nblk, dtype=jnp.int32), last)
    blk_e = jnp.minimum(jnp.sum(((bidx * tmm)[:, None] >= pend[None, :]).astype(jnp.int32), axis=1), n_exp - 1)
    blk_valid = jnp.where(used, jnp.clip(counts[blk_e] - (bidx * tmm - gstart[blk_e]), 0, tmm), 0).astype(jnp.int32)

    xs = _dispatch_call(h, pos_t, xs0, tt)
    y = _moe_call(blk_e, bidx.astype(jnp.int32), blk_valid, xs, w1, w3, w2, tmm)
    return _combine_call(pos_t, x, mod, info, final_g, y, tt)


def kernel(x, c, ctx, c_ctx, ada_w, ada_b, norm1_g, norm2_g, rg_w_in, rg_conv_w, rg_conv_b, rg_w_a, rg_b_a,
           rg_w_i, rg_b_i, rg_lam, rg_w_out, attn_w_qkv, attn_q_g, attn_k_g, attn_w_o, ffn_w1, ffn_w3, ffn_w2,
           moe_router, moe_w1, moe_w3, moe_w2, final_g):
    b, l, d = x.shape
    n_ctx = ctx.shape[1]
    c_dim = rg_w_out.shape[1]
    n_q = attn_w_o.shape[1] // _HEAD_DIM
    n_kv = (attn_w_qkv.shape[2] // _HEAD_DIM - n_q) // 2

    rows = 16
    c_rows = jnp.concatenate([c, c_ctx[None, :], jnp.zeros((rows - b - 1, d), F32)], axis=0)
    mod = _ada_call(c_rows, ada_w, ada_b).reshape(ada_w.shape[0], rows, 6, d)
    mod_l = [mod[i, :b] for i in range(2)]
    mod_c = [jnp.broadcast_to(mod[i, b], (b, 6, d)) for i in range(2)]

    w_in = rg_w_in[0].astype(BF16)
    w_out = rg_w_out[0].astype(BF16)
    w_a = rg_w_a[0].astype(BF16)
    w_i = rg_w_i[0].astype(BF16)
    xg_l = _nmm_call(x, norm1_g[0], mod_l[0], w_in, 0, 1, F32)
    xg_c = _nmm_call(ctx, norm1_g[0], mod_c[0], w_in, 0, 1, F32)
    zero_h = jnp.zeros((b, c_dim // _LANES, _LANES), F32)

    def lru(xg, h0, tail, direction, reverse):
        return _lru_call(xg, rg_conv_w[0], rg_conv_b[0], w_a[direction], rg_b_a[0, direction], w_i[direction],
                         rg_b_i[0, direction], rg_lam[0, direction], h0, tail, reverse=reverse)

    yc_rev, h0_rev = lru(xg_c, zero_h, None, 1, True)
    ctx, h0_fwd = lru(xg_c, zero_h, (yc_rev, w_out, ctx, mod_c[0]), 0, False)
    yl_rev, _ = lru(xg_l, h0_rev, None, 1, True)
    x, _ = lru(xg_l, h0_fwd, (yl_rev, w_out, x, mod_l[0]), 0, False)
    f1, f3, f2 = ffn_w1[0].astype(BF16), ffn_w3[0].astype(BF16), ffn_w2[0].astype(BF16)
    p_rows = _moe_sorted_rows(b * l, moe_router.shape[2])
    ffn_steps = b * (l // _tile(l, 512))
    if p_rows % ffn_steps == 0:
        x, xs0 = _ffn_call(x, norm2_g[0], mod_l[0], f1, f3, f2, zero_fill=(p_rows, d // _LANES, _LANES))
    else:
        x, xs0 = _ffn_call(x, norm2_g[0], mod_l[0], f1, f3, f2), None
    ctx = _ffn_call(ctx, norm2_g[0], mod_c[0], f1, f3, f2)

    w_qkv, q_g, k_g = _permute_qk_heads(attn_w_qkv[0], attn_q_g[0], attn_k_g[0], n_q, n_kv)
    w_qkv = w_qkv.astype(BF16)
    cos, sin = _rope_tables(l)
    q, k_l, v_l = _qkv_call(x, norm1_g[1], mod_l[1], w_qkv, q_g, k_g, cos, sin, n_q, n_kv)
    ones = jnp.ones((n_ctx, _HEAD_DIM), F32)
    _, k_c, v_c = _qkv_call(ctx, norm1_g[1], mod_c[1], w_qkv, q_g, k_g, ones, jnp.zeros_like(ones), n_q, n_kv)
    k_all = jnp.concatenate([k_c, k_l], axis=3)
    v_all = jnp.concatenate([v_c, v_l], axis=1)
    score_bound = _HEAD_DIM * jnp.max(jnp.abs(attn_q_g[0])) * jnp.max(jnp.abs(attn_k_g[0])) * _Q_SCALE
    ew = [w.reshape(-1, w.shape[-1]) for w in (moe_w1[0], moe_w3[0], moe_w2[0])]
    steps = b * n_kv * (l // _tile(l, 512))
    side = tuple(ew) if _side_cast_ok(ew, steps) else ()
    res = lax.cond(score_bound <= _SCORE_BOUND,
                   lambda: tuple(_attn_bounded_call(q, k_all, v_all, n_kv, side)),
                   lambda: (_attn_call(q, k_all, v_all, n_kv),) + tuple(w.astype(BF16) for w in side))
    o = res[0]
    ew = list(res[1:]) if side else [w.astype(BF16) for w in ew]
    ew = [w.reshape(m.shape) for w, m in zip(ew, (moe_w1[0], moe_w3[0], moe_w2[0]))]
    return _moe_layer(o, attn_w_o[0].astype(BF16), x, norm2_g[1], mod_l[1], moe_router[0], ew[0], ew[1], ew[2],
                      final_g, xs0)
```

```python
import functools

import jax
import jax.numpy as jnp
import numpy as np
from jax import lax
from jax.experimental import pallas as pl
from jax.experimental.pallas import tpu as pltpu

F32 = jnp.float32
BF16 = jnp.bfloat16

_EPS = 1e-6
_HEAD_DIM = 128
_GRID_W = 64
_ROPE_THETA = 10000.0
_ROPE_NFREQ = _HEAD_DIM // 4
_LRU_C = 8.0
_CONV_W = 4
_LANES = 128
_SUBLANES = 8
_LOG2E = 1.4426950408889634
_MIB = 1024 * 1024


def _params(semantics, vmem_mib):
    return pltpu.CompilerParams(dimension_semantics=semantics, vmem_limit_bytes=vmem_mib * _MIB)


def _tile(n, pref):
    if n <= pref:
        return n
    t = pref
    while n % t:
        t //= 2
    return t


def _dot(a, b):
    return jnp.dot(a, b, preferred_element_type=F32)


def _split_bf16(a):
    hi = a.astype(BF16)
    lo = (a - hi.astype(F32)).astype(BF16)
    return hi, lo


def _dot3(a, b):
    ah, al = _split_bf16(a)
    bh, bl = _split_bf16(b)
    return _dot(ah, bh) + (_dot(al, bh) + _dot(ah, bl))


def _rms_mod(x, g, shift, scale):
    y = x * lax.rsqrt(jnp.mean(x * x, axis=-1, keepdims=True) + _EPS)
    return (y * g) * (1.0 + scale) + shift


def _sigmoid(x):
    return 1.0 / (1.0 + jnp.exp(-x))


def _ada_kernel(c_ref, w_ref, b_ref, o_ref):
    c = c_ref[...]
    s = c * _sigmoid(c)
    o_ref[...] = _dot3(s, w_ref[...]) + b_ref[...]


def _ada_call(c_rows, ada_w, ada_b):
    depth, d, n = ada_w.shape
    rows = c_rows.shape[0]
    tn = _tile(n, 1536)
    return pl.pallas_call(
        _ada_kernel,
        grid=(depth, n // tn),
        in_specs=[
            pl.BlockSpec((rows, d), lambda i, j: (0, 0)),
            pl.BlockSpec((None, d, tn), lambda i, j: (i, 0, j)),
            pl.BlockSpec((None, 1, tn), lambda i, j: (i, 0, j)),
        ],
        out_specs=pl.BlockSpec((None, rows, tn), lambda i, j: (i, 0, j)),
        out_shape=jax.ShapeDtypeStruct((depth, rows, n), F32),
        compiler_params=_params(("parallel", "parallel"), 40),
        name="ada_mod",
    )(c_rows, ada_w, ada_b.reshape(depth, 1, n))


def _nmm_kernel(x_ref, g_ref, mod_ref, w_ref, o_ref, *, sh, sc):
    h = _rms_mod(x_ref[...], g_ref[...], mod_ref[sh:sh + 1, :], mod_ref[sc:sc + 1, :])
    o_ref[...] = _dot(h.astype(BF16), w_ref[...]).astype(o_ref.dtype)


def _nmm_call(x, g, mod, w, sh, sc, out_dtype):
    b, l, d = x.shape
    n = w.shape[1]
    tm = _tile(l, 512)
    return pl.pallas_call(
        functools.partial(_nmm_kernel, sh=sh, sc=sc),
        grid=(b, l // tm),
        in_specs=[
            pl.BlockSpec((None, tm, d), lambda i, t: (i, t, 0)),
            pl.BlockSpec((1, d), lambda i, t: (0, 0)),
            pl.BlockSpec((None, 6, d), lambda i, t: (i, 0, 0)),
            pl.BlockSpec((d, n), lambda i, t: (0, 0)),
        ],
        out_specs=pl.BlockSpec((None, tm, n), lambda i, t: (i, t, 0)),
        out_shape=jax.ShapeDtypeStruct((b, l, n), out_dtype),
        compiler_params=_params(("parallel", "parallel"), 48),
        name="norm_mod_matmul",
    )(x, g.reshape(1, d), mod, w)


def _log1p_pos(e):
    u = 1.0 + e
    den = jnp.where(u == 1.0, 1.0, u - 1.0)
    return jnp.where(u == 1.0, e, jnp.log(u) * (e * (1.0 / den)))


def _softplus(x):
    return jnp.maximum(x, 0.0) + _log1p_pos(jnp.exp(-jnp.abs(x)))


def _scan_rows(a, b, h, reverse):
    tl = a.shape[0]
    nb = a.shape[1] // _LANES
    a3 = pltpu.einshape("t(jc)->tjc", a, j=nb)
    b3 = pltpu.einshape("t(jc)->tjc", b, j=nb)
    states = [None] * tl
    for t in (range(tl - 1, -1, -1) if reverse else range(tl)):
        h = a3[t] * h + b3[t]
        states[t] = h
    return pltpu.einshape("tjc->t(jc)", jnp.stack(states, axis=0)), h


def _lru_kernel(*refs, reverse, tl, nt, fuse):
    if fuse:
        (xm_ref, xp_ref, xn_ref, cw_ref, cb_ref, wa_ref, ba_ref, wi_ref, bi_ref, lam_ref, h0_ref,
         yo_ref, gb_ref, wo_ref, res_ref, mod_ref, out_ref, hl_ref, carry_ref) = refs
    else:
        (xm_ref, xp_ref, xn_ref, cw_ref, cb_ref, wa_ref, ba_ref, wi_ref, bi_ref, lam_ref, h0_ref,
         out_ref, hl_ref, carry_ref) = refs
    t = pl.program_id(1)
    tt = (nt - 1 - t) if reverse else t

    @pl.when(t == 0)
    def _():
        carry_ref[...] = h0_ref[...]

    xm = xm_ref[...]
    prev = jnp.where(tt > 0, xp_ref[...], 0.0)
    nxt = jnp.where(tt < nt - 1, xn_ref[...], 0.0)
    row8 = lax.broadcasted_iota(jnp.int32, prev.shape, 0)

    def tap(offset):
        if offset < 0:
            s = -offset
            y = pltpu.roll(xm, s, axis=0)
            head = jnp.where(row8 < s, pltpu.roll(prev, s, axis=0), y[:_SUBLANES, :])
            return jnp.concatenate([head, y[_SUBLANES:, :]], axis=0)
        y = pltpu.roll(xm, tl - offset, axis=0)
        tail = jnp.where(row8 >= _SUBLANES - offset, pltpu.roll(nxt, _SUBLANES - offset, axis=0),
                         y[tl - _SUBLANES:, :])
        return jnp.concatenate([y[:tl - _SUBLANES, :], tail], axis=0)

    cw = cw_ref[...]
    left = _CONV_W // 2
    xc = cb_ref[...]
    for k in range(_CONV_W):
        xk = xm if k == left else tap(k - left)
        xc = xc + xk * cw[k:k + 1, :]

    xcb = xc.astype(BF16)
    nb = wa_ref.shape[0]
    blocks = [slice(j * _LANES, (j + 1) * _LANES) for j in range(nb)]
    pre_r = jnp.concatenate([_dot(xcb[:, s], wa_ref[j]) for j, s in enumerate(blocks)], axis=1)
    pre_i = jnp.concatenate([_dot(xcb[:, s], wi_ref[j]) for j, s in enumerate(blocks)], axis=1)
    t_r = jnp.tanh(pre_r + ba_ref[...])
    i = 0.5 * jnp.tanh(pre_i + bi_ref[...]) + 0.5
    decay = (t_r + 1.0) * ((0.5 * _LRU_C) * _softplus(-lam_ref[...]))
    a = jnp.exp(-decay)
    mult = jnp.sqrt(jnp.tanh(decay) * (1.0 + a * a))
    b = mult * (i * xc)

    hs, new_carry = _scan_rows(a, b, carry_ref[...], reverse)
    carry_ref[...] = new_carry
    hl_ref[...] = new_carry
    if fuse:
        gate = jax.nn.gelu(gb_ref[...], approximate=True)
        z = ((hs + yo_ref[...]) * gate).astype(BF16)
        out_ref[...] = res_ref[...] + mod_ref[2:3, :] * _dot(z, wo_ref[...])
    else:
        out_ref[...] = hs


def _lru_call(xg, conv_w, conv_b, w_a, b_a, w_i, b_i, lam, h0, tail=None, *, reverse):
    b, l, c2 = xg.shape
    c = c2 // 2
    nb = c // _LANES
    tl = _tile(l, 512 if tail is not None else 256)
    nt = l // tl
    hb = tl // _SUBLANES
    nh = l // _SUBLANES
    fuse = tail is not None

    def tmap(t):
        return (nt - 1 - t) if reverse else t

    vec = lambda: pl.BlockSpec((1, c), lambda i, t: (0, 0))
    gate_w = lambda: pl.BlockSpec((nb, _LANES, _LANES), lambda i, t: (0, 0, 0))
    in_specs = [
        pl.BlockSpec((None, tl, c), lambda i, t: (i, tmap(t), 0)),
        pl.BlockSpec((None, _SUBLANES, c), lambda i, t: (i, jnp.maximum(tmap(t) * hb - 1, 0), 0)),
        pl.BlockSpec((None, _SUBLANES, c), lambda i, t: (i, jnp.minimum((tmap(t) + 1) * hb, nh - 1), 0)),
        pl.BlockSpec((_CONV_W, c), lambda i, t: (0, 0)),
        vec(),
        gate_w(),
        vec(),
        gate_w(),
        vec(),
        vec(),
        pl.BlockSpec((None, nb, _LANES), lambda i, t: (i, 0, 0)),
    ]
    args = [xg, xg, xg, conv_w, conv_b.reshape(1, c), w_a, b_a.reshape(1, c), w_i, b_i.reshape(1, c),
            lam.reshape(1, c), h0]
    d_out = c
    if fuse:
        y_other, w_out, res, mod = tail
        d_out = w_out.shape[1]
        in_specs += [
            pl.BlockSpec((None, tl, c), lambda i, t: (i, tmap(t), 0)),
            pl.BlockSpec((None, tl, c), lambda i, t: (i, tmap(t), 1)),
            pl.BlockSpec((c, d_out), lambda i, t: (0, 0), pipeline_mode=pl.Buffered(1)),
            pl.BlockSpec((None, tl, d_out), lambda i, t: (i, tmap(t), 0)),
            pl.BlockSpec((None, 6, d_out), lambda i, t: (i, 0, 0)),
        ]
        args += [y_other, xg, w_out, res, mod]
    return pl.pallas_call(
        functools.partial(_lru_kernel, reverse=reverse, tl=tl, nt=nt, fuse=fuse),
        grid=(b, nt),
        in_specs=in_specs,
        out_specs=[
            pl.BlockSpec((None, tl, d_out), lambda i, t: (i, tmap(t), 0)),
            pl.BlockSpec((None, nb, _LANES), lambda i, t: (i, 0, 0)),
        ],
        out_shape=[jax.ShapeDtypeStruct((b, l, d_out), F32), jax.ShapeDtypeStruct((b, nb, _LANES), F32)],
        scratch_shapes=[pltpu.VMEM((nb, _LANES), F32)],
        compiler_params=_params(("parallel", "arbitrary"), 48),
        name="lru_rev" if reverse else "lru_fwd",
    )(*args)


def _ffn_kernel(x_ref, g_ref, mod_ref, w1_ref, w3_ref, w2_ref, o_ref, *fill_refs):
    x = x_ref[...]
    h = _rms_mod(x, g_ref[...], mod_ref[3:4, :], mod_ref[4:5, :]).astype(BF16)
    h1 = _dot(h, w1_ref[...])
    h3 = _dot(h, w3_ref[...])
    hid = (h1 * _sigmoid(h1)) * h3
    o_ref[...] = x + mod_ref[5:6, :] * _dot(hid.astype(BF16), w2_ref[...])
    for fill_ref in fill_refs:
        fill_ref[...] = jnp.zeros_like(fill_ref)


def _ffn_call(x, g, mod, w1, w3, w2, zero_fill=None):
    b, l, d = x.shape
    ff = w1.shape[1]
    tm = _tile(l, 512)
    steps = b * (l // tm)
    resident = pl.Buffered(1)
    out_specs = [pl.BlockSpec((None, tm, d), lambda i, t: (i, t, 0))]
    out_shape = [jax.ShapeDtypeStruct((b, l, d), F32)]
    if zero_fill is not None:
        rows = zero_fill[0] // steps
        out_specs.append(pl.BlockSpec((rows,) + tuple(zero_fill[1:]), lambda i, t: (i * (l // tm) + t, 0, 0)))
        out_shape.append(jax.ShapeDtypeStruct(tuple(zero_fill), F32))
    res = pl.pallas_call(
        _ffn_kernel,
        grid=(b, l // tm),
        in_specs=[
            pl.BlockSpec((None, tm, d), lambda i, t: (i, t, 0)),
            pl.BlockSpec((1, d), lambda i, t: (0, 0)),
            pl.BlockSpec((None, 6, d), lambda i, t: (i, 0, 0)),
            pl.BlockSpec((d, ff), lambda i, t: (0, 0), pipeline_mode=resident),
            pl.BlockSpec((d, ff), lambda i, t: (0, 0), pipeline_mode=resident),
            pl.BlockSpec((ff, d), lambda i, t: (0, 0), pipeline_mode=resident),
        ],
        out_specs=out_specs,
        out_shape=out_shape,
        compiler_params=_params(("parallel", "parallel"), 56),
        name="dense_swiglu",
    )(x, g.reshape(1, d), mod, w1, w3, w2)
    return res[0] if zero_fill is None else tuple(res)


_ROPE_PERM = np.concatenate([np.arange(0, 32), np.arange(64, 96), np.arange(32, 64), np.arange(96, 128)])


def _rope_tables(l):
    pos = np.arange(l)
    row = (pos // _GRID_W).astype(np.float64)
    col = (pos % _GRID_W).astype(np.float64)
    freqs = _ROPE_THETA ** (-np.arange(_ROPE_NFREQ, dtype=np.float64) / _ROPE_NFREQ)
    ar = row[:, None] * freqs
    ac = col[:, None] * freqs
    cos = np.concatenate([np.cos(ar), np.cos(ac), np.cos(ar), np.cos(ac)], axis=-1)
    sin = np.concatenate([-np.sin(ar), -np.sin(ac), np.sin(ar), np.sin(ac)], axis=-1)
    return jnp.asarray(cos, F32), jnp.asarray(sin, F32)


def _permute_qk_heads(w_qkv, q_g, k_g, n_q, n_kv):
    cols = np.arange(w_qkv.shape[1])
    for hh in range(n_q + n_kv):
        cols[hh * _HEAD_DIM:(hh + 1) * _HEAD_DIM] = hh * _HEAD_DIM + _ROPE_PERM
    return w_qkv[:, cols], q_g[_ROPE_PERM], k_g[_ROPE_PERM]


def _qkv_kernel(x_ref, g_ref, mod_ref, w_ref, qg_ref, kg_ref, cos_ref, sin_ref, q_ref, k_ref, v_ref,
                *, n_q, n_kv, q_scale):
    tm = x_ref.shape[0]
    ts = tm // 2 if tm % (4 * _LANES) == 0 else tm
    for s0 in range(0, tm, ts):
        rows = slice(s0, s0 + ts)
        h = _rms_mod(x_ref[rows, :], g_ref[...], mod_ref[0:1, :], mod_ref[1:2, :])
        qkv = _dot(h.astype(BF16), w_ref[...])
        cos = cos_ref[rows, :]
        sin = sin_ref[rows, :]
        lane = lax.broadcasted_iota(jnp.int32, cos.shape, 1)

        def norm_rope(v, gain):
            y = v * lax.rsqrt(jnp.mean(v * v, axis=-1, keepdims=True) + _EPS) * gain
            return y * cos + pltpu.roll(y, _HEAD_DIM // 2, axis=1) * sin

        for hh in range(n_q):
            sl = slice(hh * _HEAD_DIM, (hh + 1) * _HEAD_DIM)
            q_ref[rows, sl] = (norm_rope(qkv[:, sl], qg_ref[...]) * q_scale).astype(q_ref.dtype)
        for hh in range(n_kv):
            src = slice((n_q + hh) * _HEAD_DIM, (n_q + hh + 1) * _HEAD_DIM)
            k_ref[hh, :, rows] = norm_rope(qkv[:, src], kg_ref[...]).T.astype(k_ref.dtype)
        ones_col = jnp.where(lane == 0, 1.0, 0.0).astype(v_ref.dtype)
        for hh in range(n_kv):
            src = slice((n_q + n_kv + hh) * _HEAD_DIM, (n_q + n_kv + hh + 1) * _HEAD_DIM)
            v_ref[rows, 2 * hh * _HEAD_DIM:(2 * hh + 1) * _HEAD_DIM] = qkv[:, src].astype(v_ref.dtype)
            v_ref[rows, (2 * hh + 1) * _HEAD_DIM:(2 * hh + 2) * _HEAD_DIM] = ones_col


_Q_SCALE = (_HEAD_DIM ** -0.5) * _LOG2E


def _qkv_call(x, g, mod, w, q_g, k_g, cos, sin, n_q, n_kv):
    b, l, d = x.shape
    n = w.shape[1]
    tm = _tile(l, 512)
    nq = n_q * _HEAD_DIM
    nkv = n_kv * _HEAD_DIM
    q_scale = _Q_SCALE
    return pl.pallas_call(
        functools.partial(_qkv_kernel, n_q=n_q, n_kv=n_kv, q_scale=q_scale),
        grid=(b, l // tm),
        in_specs=[
            pl.BlockSpec((None, tm, d), lambda i, t: (i, t, 0)),
            pl.BlockSpec((1, d), lambda i, t: (0, 0)),
            pl.BlockSpec((None, 6, d), lambda i, t: (i, 0, 0)),
            pl.BlockSpec((d, n), lambda i, t: (0, 0)),
            pl.BlockSpec((1, _HEAD_DIM), lambda i, t: (0, 0)),
            pl.BlockSpec((1, _HEAD_DIM), lambda i, t: (0, 0)),
            pl.BlockSpec((tm, _HEAD_DIM), lambda i, t: (t, 0)),
            pl.BlockSpec((tm, _HEAD_DIM), lambda i, t: (t, 0)),
        ],
        out_specs=[
            pl.BlockSpec((None, tm, nq), lambda i, t: (i, t, 0)),
            pl.BlockSpec((None, n_kv, _HEAD_DIM, tm), lambda i, t: (i, 0, 0, t)),
            pl.BlockSpec((None, tm, 2 * nkv), lambda i, t: (i, t, 0)),
        ],
        out_shape=[
            jax.ShapeDtypeStruct((b, l, nq), BF16),
            jax.ShapeDtypeStruct((b, n_kv, _HEAD_DIM, l), BF16),
            jax.ShapeDtypeStruct((b, l, 2 * nkv), BF16),
        ],
        compiler_params=_params(("parallel", "parallel"), 48),
        name="qkv_norm_rope",
    )(x, g.reshape(1, d), mod, w, q_g.reshape(1, _HEAD_DIM), k_g.reshape(1, _HEAD_DIM), cos, sin)


def _lane_tile(x, reps):
    return x if reps == 1 else jnp.concatenate([x] * reps, axis=1)


def _attn_kernel(q_ref, k_ref, v_ref, o_ref, m_ref, l_ref, acc_ref, *, tq, tk, groups, n_chunks):
    q = jnp.concatenate([q_ref[:, g * _HEAD_DIM:(g + 1) * _HEAD_DIM] for g in range(groups)], axis=0)
    m_ref[...] = jnp.full_like(m_ref, -jnp.inf)
    l_ref[...] = jnp.zeros_like(l_ref)
    acc_ref[...] = jnp.zeros_like(acc_ref)

    def body(c, carry):
        start = pl.multiple_of(c * tk, tk)
        k = k_ref[pl.ds(start, tk), :]
        v = v_ref[pl.ds(start, tk), :]
        s = lax.dot_general(q, k, (((1,), (1,)), ((), ())), preferred_element_type=F32)
        m_prev = m_ref[...]
        m_next = jnp.maximum(m_prev, jnp.max(s, axis=1, keepdims=True))
        alpha = jnp.exp2(m_prev - m_next)
        p = jnp.exp2(s - _lane_tile(m_next, tk // _LANES))
        l_ref[...] = alpha * l_ref[...] + jnp.sum(p, axis=1, keepdims=True)
        acc_ref[...] = alpha * acc_ref[...] + _dot(p.astype(BF16), v)
        m_ref[...] = m_next
        return carry

    lax.fori_loop(0, n_chunks, body, 0)
    o = acc_ref[...] * (1.0 / l_ref[...])
    for g in range(groups):
        o_ref[:, g * _HEAD_DIM:(g + 1) * _HEAD_DIM] = o[g * tq:(g + 1) * tq, :].astype(o_ref.dtype)


def _attn_bounded_kernel(q_ref, kt_ref, v_ref, *rest, tq, tk, groups, n_chunks):
    n_side = (len(rest) - 1) // 2
    side_in, o_ref, side_out = rest[:n_side], rest[n_side], rest[n_side + 1:]
    for src, dst in zip(side_in, side_out):
        dst[...] = src[...].astype(dst.dtype)
    q = jnp.concatenate([q_ref[:, g * _HEAD_DIM:(g + 1) * _HEAD_DIM] for g in range(groups)], axis=0)
    acc = None
    for c in range(n_chunks):
        p = jnp.exp2(_dot(q, kt_ref[:, c * tk:(c + 1) * tk])).astype(BF16)
        pv = _dot(p, v_ref[c * tk:(c + 1) * tk, :])
        acc = pv if acc is None else acc + pv
    o = acc[:, :_HEAD_DIM] * (1.0 / acc[:, _HEAD_DIM:_HEAD_DIM + 1])
    for g in range(groups):
        o_ref[:, g * _HEAD_DIM:(g + 1) * _HEAD_DIM] = o[g * tq:(g + 1) * tq, :].astype(o_ref.dtype)


_SCORE_BOUND = 60.0
_KV_CHUNKS = (256, 128)


def _side_cast_ok(side, steps):
    return all(a.shape[0] % steps == 0 and (a.shape[0] // steps) % (2 * _SUBLANES) == 0 for a in side)


def _attn_bounded_call(q, kt, vx, n_kv, side=()):
    b, l, nq = q.shape
    s = kt.shape[3]
    groups = nq // (n_kv * _HEAD_DIM)
    gw = groups * _HEAD_DIM
    tq = _tile(l, 512)
    nt = l // tq
    tk = next(c for c in _KV_CHUNKS if s % c == 0)
    n_chunks = s // tk
    vw = 2 * _HEAD_DIM
    steps = b * n_kv * nt
    side_spec = lambda a: pl.BlockSpec((a.shape[0] // steps, a.shape[1]),
                                       lambda i, j, t: ((i * n_kv + j) * nt + t, 0))
    return pl.pallas_call(
        functools.partial(_attn_bounded_kernel, tq=tq, tk=tk, groups=groups, n_chunks=n_chunks),
        grid=(b, n_kv, nt),
        in_specs=[
            pl.BlockSpec((None, tq, gw), lambda i, j, t: (i, t, j)),
            pl.BlockSpec((None, None, _HEAD_DIM, s), lambda i, j, t: (i, j, 0, 0)),
            pl.BlockSpec((None, s, vw), lambda i, j, t: (i, 0, j)),
        ] + [side_spec(a) for a in side],
        out_specs=[pl.BlockSpec((None, tq, gw), lambda i, j, t: (i, t, j))] + [side_spec(a) for a in side],
        out_shape=[jax.ShapeDtypeStruct((b, l, nq), BF16)] + [jax.ShapeDtypeStruct(a.shape, BF16) for a in side],
        compiler_params=_params(("parallel", "parallel", "arbitrary"), 56),
        name="gqa_attention_bounded",
    )(q, kt, vx, *side)


def _attn_call(q, kt, vx, n_kv):
    b, l, nq = q.shape
    s = kt.shape[3]
    k = kt.transpose(0, 3, 1, 2).reshape(b, s, n_kv * _HEAD_DIM)
    v = vx.reshape(b, s, n_kv, 2 * _HEAD_DIM)[..., :_HEAD_DIM].reshape(b, s, n_kv * _HEAD_DIM)
    groups = nq // (n_kv * _HEAD_DIM)
    gw = groups * _HEAD_DIM
    tq = _tile(l, 256)
    tk = 768 if s % 768 == 0 else _tile(s, 512)
    n_chunks = s // tk
    m_rows = groups * tq
    return pl.pallas_call(
        functools.partial(_attn_kernel, tq=tq, tk=tk, groups=groups, n_chunks=n_chunks),
        grid=(b, n_kv, l // tq),
        in_specs=[
            pl.BlockSpec((None, tq, gw), lambda i, j, t: (i, t, j)),
            pl.BlockSpec((None, s, _HEAD_DIM), lambda i, j, t: (i, 0, j)),
            pl.BlockSpec((None, s, _HEAD_DIM), lambda i, j, t: (i, 0, j)),
        ],
        out_specs=pl.BlockSpec((None, tq, gw), lambda i, j, t: (i, t, j)),
        out_shape=jax.ShapeDtypeStruct((b, l, nq), BF16),
        scratch_shapes=[
            pltpu.VMEM((m_rows, _LANES), F32),
            pltpu.VMEM((m_rows, _LANES), F32),
            pltpu.VMEM((m_rows, _HEAD_DIM), F32),
        ],
        compiler_params=_params(("parallel", "parallel", "arbitrary"), 48),
        name="gqa_flash_attention",
    )(q, k, v)


_INFO_E0, _INFO_E1, _INFO_R0, _INFO_R1, _INFO_G0, _INFO_G1 = range(6)


def _route_kernel(o_ref, wo_ref, x_ref, g_ref, mod_ref, rw_ref, x1_ref, h_ref, info_ref, slots_ref, cnt_ref,
                  carry_ref, tri_ref, *, n_exp, tm):
    step = pl.program_id(0)

    ts = tri_ref.shape[0]

    @pl.when(step == 0)
    def _():
        carry_ref[...] = jnp.zeros_like(carry_ref)
        r = lax.broadcasted_iota(jnp.int32, (ts, ts), 0)
        cidx = lax.broadcasted_iota(jnp.int32, (ts, ts), 1)
        tri_ref[...] = jnp.where(r > cidx, 1.0, 0.0).astype(BF16)

    def sub_tile(rows, counted):
        x1 = x_ref[rows, :] + mod_ref[2:3, :] * _dot(o_ref[rows, :], wo_ref[...])
        x1_ref[rows, :] = x1
        h = _rms_mod(x1, g_ref[...], mod_ref[3:4, :], mod_ref[4:5, :])
        h_ref[rows] = pltpu.einshape("t(jc)->tjc", h, j=h_ref.shape[1])
        h_hi, h_lo = _split_bf16(h)
        both = _dot(h_hi, rw_ref[...])
        logits = both[:, :_LANES] + (both[:, _LANES:] + _dot(h_lo, rw_ref[:, :_LANES]))
        lane = lax.broadcasted_iota(jnp.int32, logits.shape, 1)
        neg = jnp.float32(-jnp.inf)
        lg = jnp.where(lane < n_exp, logits, neg)
        m0 = jnp.max(lg, axis=1, keepdims=True)
        i0 = jnp.min(jnp.where(lg == m0, lane, _LANES), axis=1, keepdims=True)
        oh0 = lane == i0
        lg1 = jnp.where(oh0, neg, lg)
        m1 = jnp.max(lg1, axis=1, keepdims=True)
        i1 = jnp.min(jnp.where(lg1 == m1, lane, _LANES), axis=1, keepdims=True)
        oh1 = lane == i1
        e = jnp.exp(m1 - m0)
        g0 = 1.0 / (1.0 + e)
        g1 = e * g0

        chosen = jnp.where(oh0 | oh1, 1.0, 0.0)
        before = _dot(tri_ref[...], chosen.astype(BF16)) + counted
        r0 = jnp.sum(jnp.where(oh0, before, 0.0), axis=1, keepdims=True)
        r1 = jnp.sum(jnp.where(oh1, before, 0.0), axis=1, keepdims=True)

        info = jnp.zeros(logits.shape, F32)
        for idx, val in ((_INFO_E0, i0.astype(F32)), (_INFO_E1, i1.astype(F32)), (_INFO_R0, r0), (_INFO_R1, r1),
                         (_INFO_G0, g0), (_INFO_G1, g1)):
            info = jnp.where(lane == idx, val, info)
        info_ref[rows, :] = info
        slots_ref[:, rows] = info.T[:_SUBLANES, :].astype(jnp.int32)
        return counted + jnp.sum(chosen, axis=0, keepdims=True)

    counted = carry_ref[...]
    for s in range(tm // ts):
        counted = sub_tile(slice(s * ts, (s + 1) * ts), counted)
    carry_ref[...] = counted
    cnt_ref[...] = counted


def _route_call(o, w_o, x, g, mod, router):
    b, l, d = x.shape
    k = o.shape[2]
    n_exp = router.shape[1]
    t = b * l
    tm = _tile(l, 1024)
    per_b = l // tm
    rw = jnp.pad(router, ((0, 0), (0, _LANES - n_exp)))
    rw = jnp.concatenate(_split_bf16(rw), axis=1)
    ts = tm // 2 if tm % 256 == 0 else tm
    return pl.pallas_call(
        functools.partial(_route_kernel, n_exp=n_exp, tm=tm),
        grid=(t // tm,),
        in_specs=[
            pl.BlockSpec((tm, k), lambda i: (i, 0)),
            pl.BlockSpec((k, d), lambda i: (0, 0), pipeline_mode=pl.Buffered(1)),
            pl.BlockSpec((tm, d), lambda i: (i, 0)),
            pl.BlockSpec((1, d), lambda i: (0, 0)),
            pl.BlockSpec((None, 6, d), lambda i: (i // per_b, 0, 0)),
            pl.BlockSpec((d, 2 * _LANES), lambda i: (0, 0)),
        ],
        out_specs=[
            pl.BlockSpec((tm, d), lambda i: (i, 0)),
            pl.BlockSpec((tm, d // _LANES, _LANES), lambda i: (i, 0, 0)),
            pl.BlockSpec((tm, _LANES), lambda i: (i, 0)),
            pl.BlockSpec((_SUBLANES, tm), lambda i: (0, i)),
            pl.BlockSpec((1, _LANES), lambda i: (0, 0)),
        ],
        out_shape=[
            jax.ShapeDtypeStruct((t, d), F32),
            jax.ShapeDtypeStruct((t, d // _LANES, _LANES), F32),
            jax.ShapeDtypeStruct((t, _LANES), F32),
            jax.ShapeDtypeStruct((_SUBLANES, t), jnp.int32),
            jax.ShapeDtypeStruct((1, _LANES), F32),
        ],
        scratch_shapes=[pltpu.VMEM((1, _LANES), F32), pltpu.VMEM((ts, ts), BF16)],
        compiler_params=_params(("arbitrary",), 48),
        name="moe_route",
    )(o.reshape(t, k), w_o, x.reshape(t, d), g.reshape(1, d), mod, rw)


def _pos_tiles(pos0, pos1, tt):
    n = pos0.shape[0] // tt
    return jnp.stack([pos0.reshape(n, tt), pos1.reshape(n, tt)], axis=1).reshape(n * 2 * tt)


def _row_copy(src, src_row, dst, dst_row, sem):
    return pltpu.make_async_copy(src.at[pl.ds(src_row, 1)], dst.at[pl.ds(dst_row, 1)], sem)


_ROW_DMA_UNROLL = 8


def _dispatch_kernel(pos_ref, h_ref, xs_in_ref, xs_ref, pos_smem, sem_s, sem_d, *, tt):
    del xs_in_ref
    cp = pltpu.make_async_copy(pos_ref, pos_smem, sem_s)
    cp.start()
    cp.wait()

    def body(r, carry):
        _row_copy(h_ref, r, xs_ref, pos_smem[r], sem_d).start(priority=0)
        _row_copy(h_ref, r, xs_ref, pos_smem[tt + r], sem_d).start(priority=1)
        return carry

    lax.fori_loop(0, tt, body, 0, unroll=_ROW_DMA_UNROLL)
    for _ in range(2 * tt):
        _row_copy(h_ref, 0, xs_ref, 0, sem_d).wait()


def _dispatch_call(h, pos_t, xs0, tt):
    t, nb, _ = h.shape
    p_rows = xs0.shape[0]
    return pl.pallas_call(
        functools.partial(_dispatch_kernel, tt=tt),
        grid=(t // tt,),
        in_specs=[
            pl.BlockSpec((2 * tt,), lambda i: (i,)),
            pl.BlockSpec((tt, nb, _LANES), lambda i: (i, 0, 0)),
            pl.BlockSpec(memory_space=pl.ANY),
        ],
        out_specs=pl.BlockSpec(memory_space=pl.ANY),
        out_shape=jax.ShapeDtypeStruct((p_rows, nb, _LANES), F32),
        scratch_shapes=[
            pltpu.SMEM((2 * tt,), jnp.int32),
            pltpu.SemaphoreType.DMA(()),
            pltpu.SemaphoreType.DMA(()),
        ],
        input_output_aliases={2: 0},
        compiler_params=_params(("arbitrary",), 32),
        name="moe_dispatch",
    )(pos_t, h, xs0)


def _moe_kernel(be_ref, bs_ref, bv_ref, x_ref, w1_ref, w3_ref, w2_ref, o_ref, acc_ref):
    del be_ref, bs_ref
    i = pl.program_id(0)
    f = pl.program_id(1)

    @pl.when((i == 0) & (f == 0))
    def _():
        acc_ref[...] = jnp.zeros_like(acc_ref)

    @pl.when(bv_ref[i] > 0)
    def _():
        xb = pltpu.einshape("tjc->t(jc)", x_ref[...]).astype(BF16)
        h1 = _dot(xb, w1_ref[...])
        h3 = _dot(xb, w3_ref[...])
        hid = (h1 * _sigmoid(h1)) * h3
        acc = jnp.where(f == 0, 0.0, acc_ref[...]) + _dot(hid.astype(BF16), w2_ref[...])
        acc_ref[...] = acc
        o_ref[...] = pltpu.einshape("t(jc)->tjc", acc, j=o_ref.shape[1])


def _moe_call(blk_e, blk_src, blk_valid, xs, w1, w3, w2, tmm):
    p_rows, nb, _ = xs.shape
    d = nb * _LANES
    ff = w1.shape[2]
    tf = 1792 if ff % 1792 == 0 else _tile(ff, 512)
    nf = ff // tf
    nblk = p_rows // tmm

    def fsel(i, f, bv):
        return jnp.where(bv[i] > 0, f, nf - 1)

    grid_spec = pltpu.PrefetchScalarGridSpec(
        num_scalar_prefetch=3,
        grid=(nblk, nf),
        in_specs=[
            pl.BlockSpec((tmm, nb, _LANES), lambda i, f, be, bs, bv: (bs[i], 0, 0)),
            pl.BlockSpec((None, d, tf), lambda i, f, be, bs, bv: (be[i], 0, fsel(i, f, bv))),
            pl.BlockSpec((None, d, tf), lambda i, f, be, bs, bv: (be[i], 0, fsel(i, f, bv))),
            pl.BlockSpec((None, tf, d), lambda i, f, be, bs, bv: (be[i], fsel(i, f, bv), 0)),
        ],
        out_specs=pl.BlockSpec((tmm, nb, _LANES), lambda i, f, be, bs, bv: (bs[i], 0, 0)),
        scratch_shapes=[pltpu.VMEM((tmm, d), F32)],
    )
    return pl.pallas_call(
        _moe_kernel,
        grid_spec=grid_spec,
        out_shape=jax.ShapeDtypeStruct((p_rows, nb, _LANES), F32),
        compiler_params=_params(("arbitrary", "arbitrary"), 58),
        name="moe_experts",
    )(blk_e, blk_src, blk_valid, xs, w1, w3, w2)


def _combine_kernel(pos_ref, pos_next_ref, x_ref, mod_ref, info_ref, fg_ref, y_ref, o_ref, pos_smem, ybuf,
                    sem_s, sem_d, *, tt, n_steps):
    i = pl.program_id(0)
    slot = lax.rem(i, 2)

    def start_gathers(p_ref, s):
        cp = pltpu.make_async_copy(p_ref, pos_smem, sem_s)
        cp.start()
        cp.wait()

        def body(r, carry):
            _row_copy(y_ref, pos_smem[r], ybuf.at[s, 0], r, sem_d.at[s]).start(priority=0)
            _row_copy(y_ref, pos_smem[tt + r], ybuf.at[s, 1], r, sem_d.at[s]).start(priority=1)
            return carry

        lax.fori_loop(0, tt, body, 0, unroll=_ROW_DMA_UNROLL)

    @pl.when(i == 0)
    def _():
        start_gathers(pos_ref, 0)

    @pl.when(i + 1 < n_steps)
    def _():
        start_gathers(pos_next_ref, 1 - slot)

    for _ in range(2 * tt):
        _row_copy(y_ref, 0, ybuf.at[slot, 0], 0, sem_d.at[slot]).wait()

    info = info_ref[...]
    g0 = info[:, _INFO_G0:_INFO_G0 + 1]
    g1 = info[:, _INFO_G1:_INFO_G1 + 1]
    y0 = pltpu.einshape("tjc->t(jc)", ybuf[slot, 0])
    y1 = pltpu.einshape("tjc->t(jc)", ybuf[slot, 1])
    ffn = g0 * y0 + g1 * y1
    xo = x_ref[...] + mod_ref[5:6, :] * ffn
    o_ref[...] = xo * lax.rsqrt(jnp.mean(xo * xo, axis=-1, keepdims=True) + _EPS) * fg_ref[...]


def _combine_call(pos_t, x, mod, info, final_g, y, tt):
    b, l, d = x.shape
    t = b * l
    per_b = l // tt
    n_steps = t // tt
    return pl.pallas_call(
        functools.partial(_combine_kernel, tt=tt, n_steps=n_steps),
        grid=(n_steps,),
        in_specs=[
            pl.BlockSpec((2 * tt,), lambda i: (i,)),
            pl.BlockSpec((2 * tt,), lambda i: (jnp.minimum(i + 1, n_steps - 1),)),
            pl.BlockSpec((tt, d), lambda i: (i, 0)),
            pl.BlockSpec((None, 6, d), lambda i: (i // per_b, 0, 0)),
            pl.BlockSpec((tt, _LANES), lambda i: (i, 0)),
            pl.BlockSpec((1, d), lambda i: (0, 0)),
            pl.BlockSpec(memory_space=pl.ANY),
        ],
        out_specs=pl.BlockSpec((tt, d), lambda i: (i, 0)),
        out_shape=jax.ShapeDtypeStruct((t, d), F32),
        scratch_shapes=[
            pltpu.SMEM((2 * tt,), jnp.int32),
            pltpu.VMEM((2, 2, tt, d // _LANES, _LANES), F32),
            pltpu.SemaphoreType.DMA(()),
            pltpu.SemaphoreType.DMA((2,)),
        ],
        compiler_params=_params(("arbitrary",), 40),
        name="moe_combine_norm",
    )(pos_t, pos_t, x.reshape(t, d), mod, info, final_g.reshape(1, d), y).reshape(b, l, d)


def _moe_block_rows(t):
    return 512 if t >= 4096 else 128


def _moe_sorted_rows(t, n_exp):
    tmm = _moe_block_rows(t)
    return ((t * 2) // tmm + n_exp) * tmm


def _moe_layer(o, w_o, x, g, mod, router, w1, w3, w2, final_g, xs0=None):
    b, l, d = x.shape
    t = b * l
    n_exp = router.shape[1]
    tmm = _moe_block_rows(t)
    tt = _tile(l, 512)
    x, h, info, slots, counts = _route_call(o, w_o, x, g, mod, router)
    x = x.reshape(b, l, d)

    counts = counts[0, :n_exp].astype(jnp.int32)
    padded = ((counts + tmm - 1) // tmm) * tmm
    pend = jnp.cumsum(padded)
    gstart = pend - padded
    pos0 = gstart[slots[_INFO_E0]] + slots[_INFO_R0]
    pos1 = gstart[slots[_INFO_E1]] + slots[_INFO_R1]
    pos_t = _pos_tiles(pos0, pos1, tt)

    p_rows = _moe_sorted_rows(t, n_exp)
    nblk = p_rows // tmm
    if xs0 is None:
        xs0 = jnp.zeros((p_rows, d // _LANES, _LANES), F32)
    bstart = jnp.arange(nblk, dtype=jnp.int32) * tmm
    used = bstart < pend[-1]
    last = pend[-1] // tmm - 1
    bidx = jnp.where(used, jnp.arange(nblk, dtype=jnp.int32), last)
    blk_e = jnp.minimum(jnp.sum(((bidx * tmm)[:, None] >= pend[None, :]).astype(jnp.int32), axis=1), n_exp - 1)
    blk_valid = jnp.where(used, jnp.clip(counts[blk_e] - (bidx * tmm - gstart[blk_e]), 0, tmm), 0).astype(jnp.int32)

    xs = _dispatch_call(h, pos_t, xs0, tt)
    y = _moe_call(blk_e, bidx.astype(jnp.int32), blk_valid, xs, w1, w3, w2, tmm)
    return _combine_call(pos_t, x, mod, info, final_g, y, tt)


def kernel(x, c, ctx, c_ctx, ada_w, ada_b, norm1_g, norm2_g, rg_w_in, rg_conv_w, rg_conv_b, rg_w_a, rg_b_a,
           rg_w_i, rg_b_i, rg_lam, rg_w_out, attn_w_qkv, attn_q_g, attn_k_g, attn_w_o, ffn_w1, ffn_w3, ffn_w2,
           moe_router, moe_w1, moe_w3, moe_w2, final_g):
    b, l, d = x.shape
    n_ctx = ctx.shape[1]
    c_dim = rg_w_out.shape[1]
    n_q = attn_w_o.shape[1] // _HEAD_DIM
    n_kv = (attn_w_qkv.shape[2] // _HEAD_DIM - n_q) // 2

    rows = 16
    c_rows = jnp.concatenate([c, c_ctx[None, :], jnp.zeros((rows - b - 1, d), F32)], axis=0)
    mod = _ada_call(c_rows, ada_w, ada_b).reshape(ada_w.shape[0], rows, 6, d)
    mod_l = [mod[i, :b] for i in range(2)]
    mod_c = [jnp.broadcast_to(mod[i, b], (b, 6, d)) for i in range(2)]

    w_in = rg_w_in[0].astype(BF16)
    w_out = rg_w_out[0].astype(BF16)
    w_a = (0.5 * rg_w_a[0]).astype(BF16)
    w_i = (0.5 * rg_w_i[0]).astype(BF16)
    half_b_a = 0.5 * rg_b_a[0]
    half_b_i = 0.5 * rg_b_i[0]
    xg_l = _nmm_call(x, norm1_g[0], mod_l[0], w_in, 0, 1, F32)
    xg_c = _nmm_call(ctx, norm1_g[0], mod_c[0], w_in, 0, 1, F32)
    zero_h = jnp.zeros((b, c_dim // _LANES, _LANES), F32)

    def lru(xg, h0, tail, direction, reverse):
        return _lru_call(xg, rg_conv_w[0], rg_conv_b[0], w_a[direction], half_b_a[direction], w_i[direction],
                         half_b_i[direction], rg_lam[0, direction], h0, tail, reverse=reverse)

    yc_rev, h0_rev = lru(xg_c, zero_h, None, 1, True)
    ctx, h0_fwd = lru(xg_c, zero_h, (yc_rev, w_out, ctx, mod_c[0]), 0, False)
    yl_rev, _ = lru(xg_l, h0_rev, None, 1, True)
    x, _ = lru(xg_l, h0_fwd, (yl_rev, w_out, x, mod_l[0]), 0, False)
    f1, f3, f2 = ffn_w1[0].astype(BF16), ffn_w3[0].astype(BF16), ffn_w2[0].astype(BF16)
    p_rows = _moe_sorted_rows(b * l, moe_router.shape[2])
    ffn_steps = b * (l // _tile(l, 512))
    if p_rows % ffn_steps == 0:
        x, xs0 = _ffn_call(x, norm2_g[0], mod_l[0], f1, f3, f2, zero_fill=(p_rows, d // _LANES, _LANES))
    else:
        x, xs0 = _ffn_call(x, norm2_g[0], mod_l[0], f1, f3, f2), None
    ctx = _ffn_call(ctx, norm2_g[0], mod_c[0], f1, f3, f2)

    w_qkv, q_g, k_g = _permute_qk_heads(attn_w_qkv[0], attn_q_g[0], attn_k_g[0], n_q, n_kv)
    w_qkv = w_qkv.astype(BF16)
    cos, sin = _rope_tables(l)
    q, k_l, v_l = _qkv_call(x, norm1_g[1], mod_l[1], w_qkv, q_g, k_g, cos, sin, n_q, n_kv)
    ones = jnp.ones((n_ctx, _HEAD_DIM), F32)
    _, k_c, v_c = _qkv_call(ctx, norm1_g[1], mod_c[1], w_qkv, q_g, k_g, ones, jnp.zeros_like(ones), n_q, n_kv)
    k_all = jnp.concatenate([k_c, k_l], axis=3)
    v_all = jnp.concatenate([v_c, v_l], axis=1)
    score_bound = _HEAD_DIM * jnp.max(jnp.abs(attn_q_g[0])) * jnp.max(jnp.abs(attn_k_g[0])) * _Q_SCALE
    ew = [w.reshape(-1, w.shape[-1]) for w in (moe_w1[0], moe_w3[0], moe_w2[0])]
    steps = b * n_kv * (l // _tile(l, 512))
    side = tuple(ew) if _side_cast_ok(ew, steps) else ()
    res = lax.cond(score_bound <= _SCORE_BOUND,
                   lambda: tuple(_attn_bounded_call(q, k_all, v_all, n_kv, side)),
                   lambda: (_attn_call(q, k_all, v_all, n_kv),) + tuple(w.astype(BF16) for w in side))
    o = res[0]
    ew = list(res[1:]) if side else [w.astype(BF16) for w in ew]
    ew = [w.reshape(m.shape) for w, m in zip(ew, (moe_w1[0], moe_w3[0], moe_w2[0]))]
    return _moe_layer(o, attn_w_o[0].astype(BF16), x, norm2_g[1], mod_l[1], moe_router[0], ew[0], ew[1], ew[2],
                      final_g, xs0)
```

```python
import functools

import jax
import jax.numpy as jnp
import numpy as np
from jax import lax
from jax.experimental import pallas as pl
from jax.experimental.pallas import tpu as pltpu

F32 = jnp.float32
BF16 = jnp.bfloat16

_EPS = 1e-6
_HEAD_DIM = 128
_GRID_W = 64
_ROPE_THETA = 10000.0
_ROPE_NFREQ = _HEAD_DIM // 4
_LRU_C = 8.0
_CONV_W = 4
_LANES = 128
_SUBLANES = 8
_LOG2E = 1.4426950408889634
_MIB = 1024 * 1024


def _params(semantics, vmem_mib):
    return pltpu.CompilerParams(dimension_semantics=semantics, vmem_limit_bytes=vmem_mib * _MIB)


def _tile(n, pref):
    if n <= pref:
        return n
    t = pref
    while n % t:
        t //= 2
    return t


def _dot(a, b):
    return jnp.dot(a, b, preferred_element_type=F32)


def _split_bf16(a):
    hi = a.astype(BF16)
    lo = (a - hi.astype(F32)).astype(BF16)
    return hi, lo


def _dot3(a, b):
    ah, al = _split_bf16(a)
    bh, bl = _split_bf16(b)
    return _dot(ah, bh) + (_dot(al, bh) + _dot(ah, bl))


def _rms_mod(x, g, shift, scale):
    y = x * lax.rsqrt(jnp.mean(x * x, axis=-1, keepdims=True) + _EPS)
    return (y * g) * (1.0 + scale) + shift


def _sigmoid(x):
    return 1.0 / (1.0 + jnp.exp(-x))


def _ada_kernel(c_ref, w_ref, b_ref, o_ref):
    c = c_ref[...]
    s = c * _sigmoid(c)
    o_ref[...] = _dot3(s, w_ref[...]) + b_ref[...]


def _ada_call(c_rows, ada_w, ada_b):
    depth, d, n = ada_w.shape
    rows = c_rows.shape[0]
    tn = _tile(n, 1536)
    return pl.pallas_call(
        _ada_kernel,
        grid=(depth, n // tn),
        in_specs=[
            pl.BlockSpec((rows, d), lambda i, j: (0, 0)),
            pl.BlockSpec((None, d, tn), lambda i, j: (i, 0, j)),
            pl.BlockSpec((None, 1, tn), lambda i, j: (i, 0, j)),
        ],
        out_specs=pl.BlockSpec((None, rows, tn), lambda i, j: (i, 0, j)),
        out_shape=jax.ShapeDtypeStruct((depth, rows, n), F32),
        compiler_params=_params(("parallel", "parallel"), 40),
        name="ada_mod",
    )(c_rows, ada_w, ada_b.reshape(depth, 1, n))


def _nmm_kernel(x_ref, g_ref, mod_ref, w_ref, o_ref, *, sh, sc):
    h = _rms_mod(x_ref[...], g_ref[...], mod_ref[sh:sh + 1, :], mod_ref[sc:sc + 1, :])
    o_ref[...] = _dot(h.astype(BF16), w_ref[...]).astype(o_ref.dtype)


def _nmm_call(x, g, mod, w, sh, sc, out_dtype):
    b, l, d = x.shape
    n = w.shape[1]
    tm = _tile(l, 512)
    return pl.pallas_call(
        functools.partial(_nmm_kernel, sh=sh, sc=sc),
        grid=(b, l // tm),
        in_specs=[
            pl.BlockSpec((None, tm, d), lambda i, t: (i, t, 0)),
            pl.BlockSpec((1, d), lambda i, t: (0, 0)),
            pl.BlockSpec((None, 6, d), lambda i, t: (i, 0, 0)),
            pl.BlockSpec((d, n), lambda i, t: (0, 0)),
        ],
        out_specs=pl.BlockSpec((None, tm, n), lambda i, t: (i, t, 0)),
        out_shape=jax.ShapeDtypeStruct((b, l, n), out_dtype),
        compiler_params=_params(("parallel", "parallel"), 48),
        name="norm_mod_matmul",
    )(x, g.reshape(1, d), mod, w)


def _log1p_pos(e):
    u = 1.0 + e
    den = jnp.where(u == 1.0, 1.0, u - 1.0)
    return jnp.where(u == 1.0, e, jnp.log(u) * (e * (1.0 / den)))


def _softplus(x):
    return jnp.maximum(x, 0.0) + _log1p_pos(jnp.exp(-jnp.abs(x)))


def _scan_rows(a, b, h, reverse):
    tl = a.shape[0]
    nb = a.shape[1] // _LANES
    a3 = pltpu.einshape("t(jc)->tjc", a, j=nb)
    b3 = pltpu.einshape("t(jc)->tjc", b, j=nb)
    states = [None] * tl
    for t in (range(tl - 1, -1, -1) if reverse else range(tl)):
        h = a3[t] * h + b3[t]
        states[t] = h
    return pltpu.einshape("tjc->t(jc)", jnp.stack(states, axis=0)), h


def _lru_kernel(*refs, reverse, tl, nt, fuse):
    if fuse:
        (xm_ref, xp_ref, xn_ref, cw_ref, cb_ref, wa_ref, ba_ref, wi_ref, bi_ref, lam_ref, h0_ref,
         yo_ref, gb_ref, wo_ref, res_ref, mod_ref, out_ref, hl_ref, carry_ref) = refs
    else:
        (xm_ref, xp_ref, xn_ref, cw_ref, cb_ref, wa_ref, ba_ref, wi_ref, bi_ref, lam_ref, h0_ref,
         out_ref, hl_ref, carry_ref) = refs
    t = pl.program_id(1)
    tt = (nt - 1 - t) if reverse else t

    @pl.when(t == 0)
    def _():
        carry_ref[...] = h0_ref[...]

    xm = xm_ref[...]
    prev = jnp.where(tt > 0, xp_ref[...], 0.0)
    nxt = jnp.where(tt < nt - 1, xn_ref[...], 0.0)
    row8 = lax.broadcasted_iota(jnp.int32, prev.shape, 0)

    def tap(offset):
        if offset < 0:
            s = -offset
            y = pltpu.roll(xm, s, axis=0)
            head = jnp.where(row8 < s, pltpu.roll(prev, s, axis=0), y[:_SUBLANES, :])
            return jnp.concatenate([head, y[_SUBLANES:, :]], axis=0)
        y = pltpu.roll(xm, tl - offset, axis=0)
        tail = jnp.where(row8 >= _SUBLANES - offset, pltpu.roll(nxt, _SUBLANES - offset, axis=0),
                         y[tl - _SUBLANES:, :])
        return jnp.concatenate([y[:tl - _SUBLANES, :], tail], axis=0)

    cw = cw_ref[...]
    left = _CONV_W // 2
    xc = cb_ref[...]
    for k in range(_CONV_W):
        xk = xm if k == left else tap(k - left)
        xc = xc + xk * cw[k:k + 1, :]

    xcb = xc.astype(BF16)
    nb = wa_ref.shape[0]
    blocks = [slice(j * _LANES, (j + 1) * _LANES) for j in range(nb)]
    pre_r = jnp.concatenate([_dot(xcb[:, s], wa_ref[j]) for j, s in enumerate(blocks)], axis=1)
    pre_i = jnp.concatenate([_dot(xcb[:, s], wi_ref[j]) for j, s in enumerate(blocks)], axis=1)
    t_r = jnp.tanh(pre_r + ba_ref[...])
    i = 0.5 * jnp.tanh(pre_i + bi_ref[...]) + 0.5
    decay = (t_r + 1.0) * ((0.5 * _LRU_C) * _softplus(-lam_ref[...]))
    a = jnp.exp(-decay)
    mult = jnp.sqrt(jnp.tanh(decay) * (1.0 + a * a))
    b = mult * (i * xc)

    hs, new_carry = _scan_rows(a, b, carry_ref[...], reverse)
    carry_ref[...] = new_carry
    hl_ref[...] = new_carry
    if fuse:
        gate = jax.nn.gelu(gb_ref[...], approximate=True)
        z = ((hs + yo_ref[...]) * gate).astype(BF16)
        out_ref[...] = res_ref[...] + mod_ref[2:3, :] * _dot(z, wo_ref[...])
    else:
        out_ref[...] = hs


def _lru_call(xg, conv_w, conv_b, w_a, b_a, w_i, b_i, lam, h0, tail=None, *, reverse):
    b, l, c2 = xg.shape
    c = c2 // 2
    nb = c // _LANES
    tl = _tile(l, 512 if tail is not None else 256)
    nt = l // tl
    hb = tl // _SUBLANES
    nh = l // _SUBLANES
    fuse = tail is not None

    def tmap(t):
        return (nt - 1 - t) if reverse else t

    vec = lambda: pl.BlockSpec((1, c), lambda i, t: (0, 0))
    gate_w = lambda: pl.BlockSpec((nb, _LANES, _LANES), lambda i, t: (0, 0, 0))
    in_specs = [
        pl.BlockSpec((None, tl, c), lambda i, t: (i, tmap(t), 0)),
        pl.BlockSpec((None, _SUBLANES, c), lambda i, t: (i, jnp.maximum(tmap(t) * hb - 1, 0), 0)),
        pl.BlockSpec((None, _SUBLANES, c), lambda i, t: (i, jnp.minimum((tmap(t) + 1) * hb, nh - 1), 0)),
        pl.BlockSpec((_CONV_W, c), lambda i, t: (0, 0)),
        vec(),
        gate_w(),
        vec(),
        gate_w(),
        vec(),
        vec(),
        pl.BlockSpec((None, nb, _LANES), lambda i, t: (i, 0, 0)),
    ]
    args = [xg, xg, xg, conv_w, conv_b.reshape(1, c), w_a, b_a.reshape(1, c), w_i, b_i.reshape(1, c),
            lam.reshape(1, c), h0]
    d_out = c
    if fuse:
        y_other, w_out, res, mod = tail
        d_out = w_out.shape[1]
        in_specs += [
            pl.BlockSpec((None, tl, c), lambda i, t: (i, tmap(t), 0)),
            pl.BlockSpec((None, tl, c), lambda i, t: (i, tmap(t), 1)),
            pl.BlockSpec((c, d_out), lambda i, t: (0, 0), pipeline_mode=pl.Buffered(1)),
            pl.BlockSpec((None, tl, d_out), lambda i, t: (i, tmap(t), 0)),
            pl.BlockSpec((None, 6, d_out), lambda i, t: (i, 0, 0)),
        ]
        args += [y_other, xg, w_out, res, mod]
    return pl.pallas_call(
        functools.partial(_lru_kernel, reverse=reverse, tl=tl, nt=nt, fuse=fuse),
        grid=(b, nt),
        in_specs=in_specs,
        out_specs=[
            pl.BlockSpec((None, tl, d_out), lambda i, t: (i, tmap(t), 0)),
            pl.BlockSpec((None, nb, _LANES), lambda i, t: (i, 0, 0)),
        ],
        out_shape=[jax.ShapeDtypeStruct((b, l, d_out), F32), jax.ShapeDtypeStruct((b, nb, _LANES), F32)],
        scratch_shapes=[pltpu.VMEM((nb, _LANES), F32)],
        compiler_params=_params(("parallel", "arbitrary"), 48),
        name="lru_rev" if reverse else "lru_fwd",
    )(*args)


def _ffn_kernel(x_ref, g_ref, mod_ref, w1_ref, w3_ref, w2_ref, o_ref, *fill_refs):
    x = x_ref[...]
    h = _rms_mod(x, g_ref[...], mod_ref[3:4, :], mod_ref[4:5, :]).astype(BF16)
    h1 = _dot(h, w1_ref[...])
    h3 = _dot(h, w3_ref[...])
    hid = (h1 * _sigmoid(h1)) * h3
    o_ref[...] = x + mod_ref[5:6, :] * _dot(hid.astype(BF16), w2_ref[...])
    for fill_ref in fill_refs:
        fill_ref[...] = jnp.zeros_like(fill_ref)


def _ffn_call(x, g, mod, w1, w3, w2, zero_fill=None):
    b, l, d = x.shape
    ff = w1.shape[1]
    tm = _tile(l, 512)
    steps = b * (l // tm)
    resident = pl.Buffered(1)
    out_specs = [pl.BlockSpec((None, tm, d), lambda i, t: (i, t, 0))]
    out_shape = [jax.ShapeDtypeStruct((b, l, d), F32)]
    if zero_fill is not None:
        rows = zero_fill[0] // steps
        out_specs.append(pl.BlockSpec((rows,) + tuple(zero_fill[1:]), lambda i, t: (i * (l // tm) + t, 0, 0)))
        out_shape.append(jax.ShapeDtypeStruct(tuple(zero_fill), F32))
    res = pl.pallas_call(
        _ffn_kernel,
        grid=(b, l // tm),
        in_specs=[
            pl.BlockSpec((None, tm, d), lambda i, t: (i, t, 0)),
            pl.BlockSpec((1, d), lambda i, t: (0, 0)),
            pl.BlockSpec((None, 6, d), lambda i, t: (i, 0, 0)),
            pl.BlockSpec((d, ff), lambda i, t: (0, 0), pipeline_mode=resident),
            pl.BlockSpec((d, ff), lambda i, t: (0, 0), pipeline_mode=resident),
            pl.BlockSpec((ff, d), lambda i, t: (0, 0), pipeline_mode=resident),
        ],
        out_specs=out_specs,
        out_shape=out_shape,
        compiler_params=_params(("parallel", "parallel"), 56),
        name="dense_swiglu",
    )(x, g.reshape(1, d), mod, w1, w3, w2)
    return res[0] if zero_fill is None else tuple(res)


_ROPE_PERM = np.concatenate([np.arange(0, 32), np.arange(64, 96), np.arange(32, 64), np.arange(96, 128)])


def _rope_tables(l):
    pos = np.arange(l)
    row = (pos // _GRID_W).astype(np.float64)
    col = (pos % _GRID_W).astype(np.float64)
    freqs = _ROPE_THETA ** (-np.arange(_ROPE_NFREQ, dtype=np.float64) / _ROPE_NFREQ)
    ar = row[:, None] * freqs
    ac = col[:, None] * freqs
    cos = np.concatenate([np.cos(ar), np.cos(ac), np.cos(ar), np.cos(ac)], axis=-1)
    sin = np.concatenate([-np.sin(ar), -np.sin(ac), np.sin(ar), np.sin(ac)], axis=-1)
    return jnp.asarray(cos, F32), jnp.asarray(sin, F32)


def _permute_qk_heads(w_qkv, q_g, k_g, n_q, n_kv):
    cols = np.arange(w_qkv.shape[1])
    for hh in range(n_q + n_kv):
        cols[hh * _HEAD_DIM:(hh + 1) * _HEAD_DIM] = hh * _HEAD_DIM + _ROPE_PERM
    return w_qkv[:, cols], q_g[_ROPE_PERM], k_g[_ROPE_PERM]


def _qkv_kernel(x_ref, g_ref, mod_ref, w_ref, qg_ref, kg_ref, cos_ref, sin_ref, *rest, n_q, n_kv, q_scale):
    q_ref, k_ref, v_ref = rest[-3:]
    tm = x_ref.shape[0]
    ts = tm // 2 if tm % (4 * _LANES) == 0 else tm
    for s0 in range(0, tm, ts):
        rows = slice(s0, s0 + ts)
        h = _rms_mod(x_ref[rows, :], g_ref[...], mod_ref[0:1, :], mod_ref[1:2, :])
        qkv = _dot(h.astype(BF16), w_ref[...])
        cos = cos_ref[rows, :]
        sin = sin_ref[rows, :]
        lane = lax.broadcasted_iota(jnp.int32, cos.shape, 1)

        def norm_rope(v, gain):
            y = v * lax.rsqrt(jnp.mean(v * v, axis=-1, keepdims=True) + _EPS) * gain
            return y * cos + pltpu.roll(y, _HEAD_DIM // 2, axis=1) * sin

        for hh in range(n_q):
            sl = slice(hh * _HEAD_DIM, (hh + 1) * _HEAD_DIM)
            q_ref[rows, sl] = (norm_rope(qkv[:, sl], qg_ref[...]) * q_scale).astype(q_ref.dtype)
        for hh in range(n_kv):
            src = slice((n_q + hh) * _HEAD_DIM, (n_q + hh + 1) * _HEAD_DIM)
            k_ref[hh, :, rows] = norm_rope(qkv[:, src], kg_ref[...]).T.astype(k_ref.dtype)
        ones_col = jnp.where(lane == 0, 1.0, 0.0).astype(v_ref.dtype)
        for hh in range(n_kv):
            src = slice((n_q + n_kv + hh) * _HEAD_DIM, (n_q + n_kv + hh + 1) * _HEAD_DIM)
            v_ref[rows, 2 * hh * _HEAD_DIM:(2 * hh + 1) * _HEAD_DIM] = qkv[:, src].astype(v_ref.dtype)
            v_ref[rows, (2 * hh + 1) * _HEAD_DIM:(2 * hh + 2) * _HEAD_DIM] = ones_col


_Q_SCALE = (_HEAD_DIM ** -0.5) * _LOG2E


def _qkv_call(x, g, mod, w, q_g, k_g, cos, sin, n_q, n_kv, s_total=None, kv_into=None, tok_offset=0):
    b, l, d = x.shape
    n = w.shape[1]
    tm = _tile(l, 512)
    nq = n_q * _HEAD_DIM
    nkv = n_kv * _HEAD_DIM
    s_total = l if s_total is None else s_total
    assert tok_offset % tm == 0
    t0 = tok_offset // tm
    in_specs = [
        pl.BlockSpec((None, tm, d), lambda i, t: (i, t, 0)),
        pl.BlockSpec((1, d), lambda i, t: (0, 0)),
        pl.BlockSpec((None, 6, d), lambda i, t: (i, 0, 0)),
        pl.BlockSpec((d, n), lambda i, t: (0, 0)),
        pl.BlockSpec((1, _HEAD_DIM), lambda i, t: (0, 0)),
        pl.BlockSpec((1, _HEAD_DIM), lambda i, t: (0, 0)),
        pl.BlockSpec((tm, _HEAD_DIM), lambda i, t: (t, 0)),
        pl.BlockSpec((tm, _HEAD_DIM), lambda i, t: (t, 0)),
    ]
    args = [x, g.reshape(1, d), mod, w, q_g.reshape(1, _HEAD_DIM), k_g.reshape(1, _HEAD_DIM), cos, sin]
    aliases = {}
    if kv_into is not None:
        aliases = {len(args): 1, len(args) + 1: 2}
        in_specs += [pl.BlockSpec(memory_space=pl.ANY), pl.BlockSpec(memory_space=pl.ANY)]
        args += list(kv_into)
    return pl.pallas_call(
        functools.partial(_qkv_kernel, n_q=n_q, n_kv=n_kv, q_scale=_Q_SCALE),
        grid=(b, l // tm),
        in_specs=in_specs,
        out_specs=[
            pl.BlockSpec((None, tm, nq), lambda i, t: (i, t, 0)),
            pl.BlockSpec((None, n_kv, _HEAD_DIM, tm), lambda i, t: (i, 0, 0, t0 + t)),
            pl.BlockSpec((None, tm, 2 * nkv), lambda i, t: (i, t0 + t, 0)),
        ],
        out_shape=[
            jax.ShapeDtypeStruct((b, l, nq), BF16),
            jax.ShapeDtypeStruct((b, n_kv, _HEAD_DIM, s_total), BF16),
            jax.ShapeDtypeStruct((b, s_total, 2 * nkv), BF16),
        ],
        input_output_aliases=aliases,
        compiler_params=_params(("parallel", "parallel"), 48),
        name="qkv_norm_rope",
    )(*args)


def _lane_tile(x, reps):
    return x if reps == 1 else jnp.concatenate([x] * reps, axis=1)


def _attn_kernel(q_ref, k_ref, v_ref, o_ref, m_ref, l_ref, acc_ref, *, tq, tk, groups, n_chunks):
    q = jnp.concatenate([q_ref[:, g * _HEAD_DIM:(g + 1) * _HEAD_DIM] for g in range(groups)], axis=0)
    m_ref[...] = jnp.full_like(m_ref, -jnp.inf)
    l_ref[...] = jnp.zeros_like(l_ref)
    acc_ref[...] = jnp.zeros_like(acc_ref)

    def body(c, carry):
        start = pl.multiple_of(c * tk, tk)
        k = k_ref[pl.ds(start, tk), :]
        v = v_ref[pl.ds(start, tk), :]
        s = lax.dot_general(q, k, (((1,), (1,)), ((), ())), preferred_element_type=F32)
        m_prev = m_ref[...]
        m_next = jnp.maximum(m_prev, jnp.max(s, axis=1, keepdims=True))
        alpha = jnp.exp2(m_prev - m_next)
        p = jnp.exp2(s - _lane_tile(m_next, tk // _LANES))
        l_ref[...] = alpha * l_ref[...] + jnp.sum(p, axis=1, keepdims=True)
        acc_ref[...] = alpha * acc_ref[...] + _dot(p.astype(BF16), v)
        m_ref[...] = m_next
        return carry

    lax.fori_loop(0, n_chunks, body, 0)
    o = acc_ref[...] * (1.0 / l_ref[...])
    for g in range(groups):
        o_ref[:, g * _HEAD_DIM:(g + 1) * _HEAD_DIM] = o[g * tq:(g + 1) * tq, :].astype(o_ref.dtype)


def _attn_bounded_kernel(q_ref, kt_ref, v_ref, *rest, tq, tk, groups, n_chunks):
    n_side = (len(rest) - 1) // 2
    side_in, o_ref, side_out = rest[:n_side], rest[n_side], rest[n_side + 1:]
    for src, dst in zip(side_in, side_out):
        dst[...] = src[...].astype(dst.dtype)
    q = jnp.concatenate([q_ref[:, g * _HEAD_DIM:(g + 1) * _HEAD_DIM] for g in range(groups)], axis=0)
    acc = None
    for c in range(n_chunks):
        p = jnp.exp2(_dot(q, kt_ref[:, c * tk:(c + 1) * tk])).astype(BF16)
        pv = _dot(p, v_ref[c * tk:(c + 1) * tk, :])
        acc = pv if acc is None else acc + pv
    o = acc[:, :_HEAD_DIM] * (1.0 / acc[:, _HEAD_DIM:_HEAD_DIM + 1])
    for g in range(groups):
        o_ref[:, g * _HEAD_DIM:(g + 1) * _HEAD_DIM] = o[g * tq:(g + 1) * tq, :].astype(o_ref.dtype)


_SCORE_BOUND = 60.0
_KV_CHUNKS = (256, 128)


def _side_cast_ok(side, steps):
    return all(a.shape[0] % steps == 0 and (a.shape[0] // steps) % (2 * _SUBLANES) == 0 for a in side)


def _attn_bounded_call(q, kt, vx, n_kv, side=()):
    b, l, nq = q.shape
    s = kt.shape[3]
    groups = nq // (n_kv * _HEAD_DIM)
    gw = groups * _HEAD_DIM
    tq = _tile(l, 512)
    nt = l // tq
    tk = next(c for c in _KV_CHUNKS if s % c == 0)
    n_chunks = s // tk
    vw = 2 * _HEAD_DIM
    steps = b * n_kv * nt
    side_spec = lambda a: pl.BlockSpec((a.shape[0] // steps, a.shape[1]),
                                       lambda i, j, t: ((i * n_kv + j) * nt + t, 0))
    return pl.pallas_call(
        functools.partial(_attn_bounded_kernel, tq=tq, tk=tk, groups=groups, n_chunks=n_chunks),
        grid=(b, n_kv, nt),
        in_specs=[
            pl.BlockSpec((None, tq, gw), lambda i, j, t: (i, t, j)),
            pl.BlockSpec((None, None, _HEAD_DIM, s), lambda i, j, t: (i, j, 0, 0)),
            pl.BlockSpec((None, s, vw), lambda i, j, t: (i, 0, j)),
        ] + [side_spec(a) for a in side],
        out_specs=[pl.BlockSpec((None, tq, gw), lambda i, j, t: (i, t, j))] + [side_spec(a) for a in side],
        out_shape=[jax.ShapeDtypeStruct((b, l, nq), BF16)] + [jax.ShapeDtypeStruct(a.shape, BF16) for a in side],
        compiler_params=_params(("parallel", "parallel", "arbitrary"), 56),
        name="gqa_attention_bounded",
    )(q, kt, vx, *side)


def _attn_call(q, kt, vx, n_kv):
    b, l, nq = q.shape
    s = kt.shape[3]
    k = kt.transpose(0, 3, 1, 2).reshape(b, s, n_kv * _HEAD_DIM)
    v = vx.reshape(b, s, n_kv, 2 * _HEAD_DIM)[..., :_HEAD_DIM].reshape(b, s, n_kv * _HEAD_DIM)
    groups = nq // (n_kv * _HEAD_DIM)
    gw = groups * _HEAD_DIM
    tq = _tile(l, 256)
    tk = 768 if s % 768 == 0 else _tile(s, 512)
    n_chunks = s // tk
    m_rows = groups * tq
    return pl.pallas_call(
        functools.partial(_attn_kernel, tq=tq, tk=tk, groups=groups, n_chunks=n_chunks),
        grid=(b, n_kv, l // tq),
        in_specs=[
            pl.BlockSpec((None, tq, gw), lambda i, j, t: (i, t, j)),
            pl.BlockSpec((None, s, _HEAD_DIM), lambda i, j, t: (i, 0, j)),
            pl.BlockSpec((None, s, _HEAD_DIM), lambda i, j, t: (i, 0, j)),
        ],
        out_specs=pl.BlockSpec((None, tq, gw), lambda i, j, t: (i, t, j)),
        out_shape=jax.ShapeDtypeStruct((b, l, nq), BF16),
        scratch_shapes=[
            pltpu.VMEM((m_rows, _LANES), F32),
            pltpu.VMEM((m_rows, _LANES), F32),
            pltpu.VMEM((m_rows, _HEAD_DIM), F32),
        ],
        compiler_params=_params(("parallel", "parallel", "arbitrary"), 48),
        name="gqa_flash_attention",
    )(q, k, v)


_INFO_E0, _INFO_E1, _INFO_R0, _INFO_R1, _INFO_G0, _INFO_G1 = range(6)


def _route_kernel(o_ref, wo_ref, x_ref, g_ref, mod_ref, rw_ref, x1_ref, h_ref, info_ref, slots_ref, cnt_ref,
                  carry_ref, tri_ref, *, n_exp, tm):
    step = pl.program_id(0)

    ts = tri_ref.shape[0]

    @pl.when(step == 0)
    def _():
        carry_ref[...] = jnp.zeros_like(carry_ref)
        r = lax.broadcasted_iota(jnp.int32, (ts, ts), 0)
        cidx = lax.broadcasted_iota(jnp.int32, (ts, ts), 1)
        tri_ref[...] = jnp.where(r > cidx, 1.0, 0.0).astype(BF16)

    def sub_tile(rows, counted):
        x1 = x_ref[rows, :] + mod_ref[2:3, :] * _dot(o_ref[rows, :], wo_ref[...])
        x1_ref[rows, :] = x1
        h = _rms_mod(x1, g_ref[...], mod_ref[3:4, :], mod_ref[4:5, :])
        h_ref[rows] = pltpu.einshape("t(jc)->tjc", h, j=h_ref.shape[1])
        h_hi, h_lo = _split_bf16(h)
        both = _dot(h_hi, rw_ref[...])
        logits = both[:, :_LANES] + (both[:, _LANES:] + _dot(h_lo, rw_ref[:, :_LANES]))
        lane = lax.broadcasted_iota(jnp.int32, logits.shape, 1)
        neg = jnp.float32(-jnp.inf)
        lg = jnp.where(lane < n_exp, logits, neg)
        m0 = jnp.max(lg, axis=1, keepdims=True)
        i0 = jnp.min(jnp.where(lg == m0, lane, _LANES), axis=1, keepdims=True)
        oh0 = lane == i0
        lg1 = jnp.where(oh0, neg, lg)
        m1 = jnp.max(lg1, axis=1, keepdims=True)
        i1 = jnp.min(jnp.where(lg1 == m1, lane, _LANES), axis=1, keepdims=True)
        oh1 = lane == i1
        e = jnp.exp(m1 - m0)
        g0 = 1.0 / (1.0 + e)
        g1 = e * g0

        chosen = jnp.where(oh0 | oh1, 1.0, 0.0)
        before = _dot(tri_ref[...], chosen.astype(BF16)) + counted
        r0 = jnp.sum(jnp.where(oh0, before, 0.0), axis=1, keepdims=True)
        r1 = jnp.sum(jnp.where(oh1, before, 0.0), axis=1, keepdims=True)

        info = jnp.zeros(logits.shape, F32)
        for idx, val in ((_INFO_E0, i0.astype(F32)), (_INFO_E1, i1.astype(F32)), (_INFO_R0, r0), (_INFO_R1, r1),
                         (_INFO_G0, g0), (_INFO_G1, g1)):
            info = jnp.where(lane == idx, val, info)
        info_ref[rows, :] = info
        slots_ref[:, rows] = info.T[:_SUBLANES, :].astype(jnp.int32)
        return counted + jnp.sum(chosen, axis=0, keepdims=True)

    counted = carry_ref[...]
    for s in range(tm // ts):
        counted = sub_tile(slice(s * ts, (s + 1) * ts), counted)
    carry_ref[...] = counted
    cnt_ref[...] = counted


def _route_call(o, w_o, x, g, mod, router):
    b, l, d = x.shape
    k = o.shape[2]
    n_exp = router.shape[1]
    t = b * l
    tm = _tile(l, 1024)
    per_b = l // tm
    rw = jnp.pad(router, ((0, 0), (0, _LANES - n_exp)))
    rw = jnp.concatenate(_split_bf16(rw), axis=1)
    ts = tm // 2 if tm % 256 == 0 else tm
    return pl.pallas_call(
        functools.partial(_route_kernel, n_exp=n_exp, tm=tm),
        grid=(t // tm,),
        in_specs=[
            pl.BlockSpec((tm, k), lambda i: (i, 0)),
            pl.BlockSpec((k, d), lambda i: (0, 0), pipeline_mode=pl.Buffered(1)),
            pl.BlockSpec((tm, d), lambda i: (i, 0)),
            pl.BlockSpec((1, d), lambda i: (0, 0)),
            pl.BlockSpec((None, 6, d), lambda i: (i // per_b, 0, 0)),
            pl.BlockSpec((d, 2 * _LANES), lambda i: (0, 0)),
        ],
        out_specs=[
            pl.BlockSpec((tm, d), lambda i: (i, 0)),
            pl.BlockSpec((tm, d // _LANES, _LANES), lambda i: (i, 0, 0)),
            pl.BlockSpec((tm, _LANES), lambda i: (i, 0)),
            pl.BlockSpec((_SUBLANES, tm), lambda i: (0, i)),
            pl.BlockSpec((1, _LANES), lambda i: (0, 0)),
        ],
        out_shape=[
            jax.ShapeDtypeStruct((t, d), F32),
            jax.ShapeDtypeStruct((t, d // _LANES, _LANES), F32),
            jax.ShapeDtypeStruct((t, _LANES), F32),
            jax.ShapeDtypeStruct((_SUBLANES, t), jnp.int32),
            jax.ShapeDtypeStruct((1, _LANES), F32),
        ],
        scratch_shapes=[pltpu.VMEM((1, _LANES), F32), pltpu.VMEM((ts, ts), BF16)],
        compiler_params=_params(("arbitrary",), 48),
        name="moe_route",
    )(o.reshape(t, k), w_o, x.reshape(t, d), g.reshape(1, d), mod, rw)


def _pos_tiles(pos0, pos1, tt):
    n = pos0.shape[0] // tt
    return jnp.stack([pos0.reshape(n, tt), pos1.reshape(n, tt)], axis=1).reshape(n * 2 * tt)


def _row_copy(src, src_row, dst, dst_row, sem):
    return pltpu.make_async_copy(src.at[pl.ds(src_row, 1)], dst.at[pl.ds(dst_row, 1)], sem)


_ROW_DMA_UNROLL = 8


def _dispatch_kernel(pos_ref, h_ref, xs_in_ref, xs_ref, pos_smem, sem_s, sem_d, *, tt):
    del xs_in_ref
    cp = pltpu.make_async_copy(pos_ref, pos_smem, sem_s)
    cp.start()
    cp.wait()

    def body(r, carry):
        _row_copy(h_ref, r, xs_ref, pos_smem[r], sem_d).start(priority=0)
        _row_copy(h_ref, r, xs_ref, pos_smem[tt + r], sem_d).start(priority=1)
        return carry

    lax.fori_loop(0, tt, body, 0, unroll=_ROW_DMA_UNROLL)
    for _ in range(2 * tt):
        _row_copy(h_ref, 0, xs_ref, 0, sem_d).wait()


def _dispatch_call(h, pos_t, xs0, tt):
    t, nb, _ = h.shape
    p_rows = xs0.shape[0]
    return pl.pallas_call(
        functools.partial(_dispatch_kernel, tt=tt),
        grid=(t // tt,),
        in_specs=[
            pl.BlockSpec((2 * tt,), lambda i: (i,)),
            pl.BlockSpec((tt, nb, _LANES), lambda i: (i, 0, 0)),
            pl.BlockSpec(memory_space=pl.ANY),
        ],
        out_specs=pl.BlockSpec(memory_space=pl.ANY),
        out_shape=jax.ShapeDtypeStruct((p_rows, nb, _LANES), F32),
        scratch_shapes=[
            pltpu.SMEM((2 * tt,), jnp.int32),
            pltpu.SemaphoreType.DMA(()),
            pltpu.SemaphoreType.DMA(()),
        ],
        input_output_aliases={2: 0},
        compiler_params=_params(("arbitrary",), 32),
        name="moe_dispatch",
    )(pos_t, h, xs0)


def _moe_kernel(be_ref, bs_ref, bv_ref, x_ref, w1_ref, w3_ref, w2_ref, o_ref, acc_ref):
    del be_ref, bs_ref
    i = pl.program_id(0)
    f = pl.program_id(1)

    @pl.when((i == 0) & (f == 0))
    def _():
        acc_ref[...] = jnp.zeros_like(acc_ref)

    @pl.when(bv_ref[i] > 0)
    def _():
        xb = pltpu.einshape("tjc->t(jc)", x_ref[...]).astype(BF16)
        h1 = _dot(xb, w1_ref[...])
        h3 = _dot(xb, w3_ref[...])
        hid = (h1 * _sigmoid(h1)) * h3
        acc = jnp.where(f == 0, 0.0, acc_ref[...]) + _dot(hid.astype(BF16), w2_ref[...])
        acc_ref[...] = acc
        o_ref[...] = pltpu.einshape("t(jc)->tjc", acc, j=o_ref.shape[1])


def _moe_call(blk_e, blk_src, blk_valid, xs, w1, w3, w2, tmm):
    p_rows, nb, _ = xs.shape
    d = nb * _LANES
    ff = w1.shape[2]
    tf = 1792 if ff % 1792 == 0 else _tile(ff, 512)
    nf = ff // tf
    nblk = p_rows // tmm

    def fsel(i, f, bv):
        return jnp.where(bv[i] > 0, f, nf - 1)

    grid_spec = pltpu.PrefetchScalarGridSpec(
        num_scalar_prefetch=3,
        grid=(nblk, nf),
        in_specs=[
            pl.BlockSpec((tmm, nb, _LANES), lambda i, f, be, bs, bv: (bs[i], 0, 0)),
            pl.BlockSpec((None, d, tf), lambda i, f, be, bs, bv: (be[i], 0, fsel(i, f, bv))),
            pl.BlockSpec((None, d, tf), lambda i, f, be, bs, bv: (be[i], 0, fsel(i, f, bv))),
            pl.BlockSpec((None, tf, d), lambda i, f, be, bs, bv: (be[i], fsel(i, f, bv), 0)),
        ],
        out_specs=pl.BlockSpec((tmm, nb, _LANES), lambda i, f, be, bs, bv: (bs[i], 0, 0)),
        scratch_shapes=[pltpu.VMEM((tmm, d), F32)],
    )
    return pl.pallas_call(
        _moe_kernel,
        grid_spec=grid_spec,
        out_shape=jax.ShapeDtypeStruct((p_rows, nb, _LANES), F32),
        compiler_params=_params(("arbitrary", "arbitrary"), 58),
        name="moe_experts",
    )(blk_e, blk_src, blk_valid, xs, w1, w3, w2)


def _combine_kernel(pos_ref, pos_next_ref, x_ref, mod_ref, info_ref, fg_ref, y_ref, o_ref, pos_smem, ybuf,
                    sem_s, sem_d, *, tt, n_steps):
    i = pl.program_id(0)
    slot = lax.rem(i, 2)

    def start_gathers(p_ref, s):
        cp = pltpu.make_async_copy(p_ref, pos_smem, sem_s)
        cp.start()
        cp.wait()

        def body(r, carry):
            _row_copy(y_ref, pos_smem[r], ybuf.at[s, 0], r, sem_d.at[s]).start(priority=0)
            _row_copy(y_ref, pos_smem[tt + r], ybuf.at[s, 1], r, sem_d.at[s]).start(priority=1)
            return carry

        lax.fori_loop(0, tt, body, 0, unroll=_ROW_DMA_UNROLL)

    @pl.when(i == 0)
    def _():
        start_gathers(pos_ref, 0)

    @pl.when(i + 1 < n_steps)
    def _():
        start_gathers(pos_next_ref, 1 - slot)

    for _ in range(2 * tt):
        _row_copy(y_ref, 0, ybuf.at[slot, 0], 0, sem_d.at[slot]).wait()

    info = info_ref[...]
    g0 = info[:, _INFO_G0:_INFO_G0 + 1]
    g1 = info[:, _INFO_G1:_INFO_G1 + 1]
    y0 = pltpu.einshape("tjc->t(jc)", ybuf[slot, 0])
    y1 = pltpu.einshape("tjc->t(jc)", ybuf[slot, 1])
    ffn = g0 * y0 + g1 * y1
    xo = x_ref[...] + mod_ref[5:6, :] * ffn
    o_ref[...] = xo * lax.rsqrt(jnp.mean(xo * xo, axis=-1, keepdims=True) + _EPS) * fg_ref[...]


def _combine_call(pos_t, x, mod, info, final_g, y, tt):
    b, l, d = x.shape
    t = b * l
    per_b = l // tt
    n_steps = t // tt
    return pl.pallas_call(
        functools.partial(_combine_kernel, tt=tt, n_steps=n_steps),
        grid=(n_steps,),
        in_specs=[
            pl.BlockSpec((2 * tt,), lambda i: (i,)),
            pl.BlockSpec((2 * tt,), lambda i: (jnp.minimum(i + 1, n_steps - 1),)),
            pl.BlockSpec((tt, d), lambda i: (i, 0)),
            pl.BlockSpec((None, 6, d), lambda i: (i // per_b, 0, 0)),
            pl.BlockSpec((tt, _LANES), lambda i: (i, 0)),
            pl.BlockSpec((1, d), lambda i: (0, 0)),
            pl.BlockSpec(memory_space=pl.ANY),
        ],
        out_specs=pl.BlockSpec((tt, d), lambda i: (i, 0)),
        out_shape=jax.ShapeDtypeStruct((t, d), F32),
        scratch_shapes=[
            pltpu.SMEM((2 * tt,), jnp.int32),
            pltpu.VMEM((2, 2, tt, d // _LANES, _LANES), F32),
            pltpu.SemaphoreType.DMA(()),
            pltpu.SemaphoreType.DMA((2,)),
        ],
        compiler_params=_params(("arbitrary",), 40),
        name="moe_combine_norm",
    )(pos_t, pos_t, x.reshape(t, d), mod, info, final_g.reshape(1, d), y).reshape(b, l, d)


def _moe_block_rows(t):
    return 512 if t >= 4096 else 128


def _moe_sorted_rows(t, n_exp):
    tmm = _moe_block_rows(t)
    return ((t * 2) // tmm + n_exp) * tmm


def _moe_layer(o, w_o, x, g, mod, router, w1, w3, w2, final_g, xs0=None):
    b, l, d = x.shape
    t = b * l
    n_exp = router.shape[1]
    tmm = _moe_block_rows(t)
    tt = _tile(l, 512)
    x, h, info, slots, counts = _route_call(o, w_o, x, g, mod, router)
    x = x.reshape(b, l, d)

    counts = counts[0, :n_exp].astype(jnp.int32)
    padded = ((counts + tmm - 1) // tmm) * tmm
    pend = jnp.cumsum(padded)
    gstart = pend - padded
    pos0 = gstart[slots[_INFO_E0]] + slots[_INFO_R0]
    pos1 = gstart[slots[_INFO_E1]] + slots[_INFO_R1]
    pos_t = _pos_tiles(pos0, pos1, tt)

    p_rows = _moe_sorted_rows(t, n_exp)
    nblk = p_rows // tmm
    if xs0 is None:
        xs0 = jnp.zeros((p_rows, d // _LANES, _LANES), F32)
    bstart = jnp.arange(nblk, dtype=jnp.int32) * tmm
    used = bstart < pend[-1]
    last = pend[-1] // tmm - 1
    bidx = jnp.where(used, jnp.arange(nblk, dtype=jnp.int32), last)
    blk_e = jnp.minimum(jnp.sum(((bidx * tmm)[:, None] >= pend[None, :]).astype(jnp.int32), axis=1), n_exp - 1)
    blk_valid = jnp.where(used, jnp.clip(counts[blk_e] - (bidx * tmm - gstart[blk_e]), 0, tmm), 0).astype(jnp.int32)

    xs = _dispatch_call(h, pos_t, xs0, tt)
    y = _moe_call(blk_e, bidx.astype(jnp.int32), blk_valid, xs, w1, w3, w2, tmm)
    return _combine_call(pos_t, x, mod, info, final_g, y, tt)


def kernel(x, c, ctx, c_ctx, ada_w, ada_b, norm1_g, norm2_g, rg_w_in, rg_conv_w, rg_conv_b, rg_w_a, rg_b_a,
           rg_w_i, rg_b_i, rg_lam, rg_w_out, attn_w_qkv, attn_q_g, attn_k_g, attn_w_o, ffn_w1, ffn_w3, ffn_w2,
           moe_router, moe_w1, moe_w3, moe_w2, final_g):
    b, l, d = x.shape
    n_ctx = ctx.shape[1]
    c_dim = rg_w_out.shape[1]
    n_q = attn_w_o.shape[1] // _HEAD_DIM
    n_kv = (attn_w_qkv.shape[2] // _HEAD_DIM - n_q) // 2

    rows = 16
    c_rows = jnp.concatenate([c, c_ctx[None, :], jnp.zeros((rows - b - 1, d), F32)], axis=0)
    mod = _ada_call(c_rows, ada_w, ada_b).reshape(ada_w.shape[0], rows, 6, d)
    mod_l = [mod[i, :b] for i in range(2)]
    mod_c = [jnp.broadcast_to(mod[i, b], (b, 6, d)) for i in range(2)]

    w_in = rg_w_in[0].astype(BF16)
    w_out = rg_w_out[0].astype(BF16)
    w_a = (0.5 * rg_w_a[0]).astype(BF16)
    w_i = (0.5 * rg_w_i[0]).astype(BF16)
    half_b_a = 0.5 * rg_b_a[0]
    half_b_i = 0.5 * rg_b_i[0]
    xg_l = _nmm_call(x, norm1_g[0], mod_l[0], w_in, 0, 1, F32)
    xg_c = _nmm_call(ctx, norm1_g[0], mod_c[0], w_in, 0, 1, F32)
    zero_h = jnp.zeros((b, c_dim // _LANES, _LANES), F32)

    def lru(xg, h0, tail, direction, reverse):
        return _lru_call(xg, rg_conv_w[0], rg_conv_b[0], w_a[direction], half_b_a[direction], w_i[direction],
                         half_b_i[direction], rg_lam[0, direction], h0, tail, reverse=reverse)

    yc_rev, h0_rev = lru(xg_c, zero_h, None, 1, True)
    ctx, h0_fwd = lru(xg_c, zero_h, (yc_rev, w_out, ctx, mod_c[0]), 0, False)
    yl_rev, _ = lru(xg_l, h0_rev, None, 1, True)
    x, _ = lru(xg_l, h0_fwd, (yl_rev, w_out, x, mod_l[0]), 0, False)
    f1, f3, f2 = ffn_w1[0].astype(BF16), ffn_w3[0].astype(BF16), ffn_w2[0].astype(BF16)
    p_rows = _moe_sorted_rows(b * l, moe_router.shape[2])
    ffn_steps = b * (l // _tile(l, 512))
    if p_rows % ffn_steps == 0:
        x, xs0 = _ffn_call(x, norm2_g[0], mod_l[0], f1, f3, f2, zero_fill=(p_rows, d // _LANES, _LANES))
    else:
        x, xs0 = _ffn_call(x, norm2_g[0], mod_l[0], f1, f3, f2), None
    ctx = _ffn_call(ctx, norm2_g[0], mod_c[0], f1, f3, f2)

    w_qkv, q_g, k_g = _permute_qk_heads(attn_w_qkv[0], attn_q_g[0], attn_k_g[0], n_q, n_kv)
    w_qkv = w_qkv.astype(BF16)
    cos, sin = _rope_tables(l)
    s_all = l + n_ctx
    q, k_all, v_all = _qkv_call(x, norm1_g[1], mod_l[1], w_qkv, q_g, k_g, cos, sin, n_q, n_kv, s_total=s_all)
    ones = jnp.ones((n_ctx, _HEAD_DIM), F32)
    _, k_all, v_all = _qkv_call(ctx, norm1_g[1], mod_c[1], w_qkv, q_g, k_g, ones, jnp.zeros_like(ones), n_q, n_kv,
                                s_total=s_all, kv_into=(k_all, v_all), tok_offset=l)
    score_bound = _HEAD_DIM * jnp.max(jnp.abs(attn_q_g[0])) * jnp.max(jnp.abs(attn_k_g[0])) * _Q_SCALE
    ew = [w.reshape(-1, w.shape[-1]) for w in (moe_w1[0], moe_w3[0], moe_w2[0])]
    steps = b * n_kv * (l // _tile(l, 512))
    side = tuple(ew) if _side_cast_ok(ew, steps) else ()
    res = lax.cond(score_bound <= _SCORE_BOUND,
                   lambda: tuple(_attn_bounded_call(q, k_all, v_all, n_kv, side)),
                   lambda: (_attn_call(q, k_all, v_all, n_kv),) + tuple(w.astype(BF16) for w in side))
    o = res[0]
    ew = list(res[1:]) if side else [w.astype(BF16) for w in ew]
    ew = [w.reshape(m.shape) for w, m in zip(ew, (moe_w1[0], moe_w3[0], moe_w2[0]))]
    return _moe_layer(o, attn_w_o[0].astype(BF16), x, norm2_g[1], mod_l[1], moe_router[0], ew[0], ew[1], ew[2],
                      final_g, xs0)
```
